```python
import math
import jax
import jax.numpy as jnp
from jax import lax
import numpy as np

D_MODEL = 1024
BATCH = 16
SEQ = 2048
DEPTH = 4

N_MIXERS = 4
HEAD_DIM = 64
ROPE_THETA = 10000.0
NORM_EPS = 1e-6
PLE_DIM = 256
Q_BLOCK = 128
NEG_INF = -1e30

DIFF_HEADS = D_MODEL // (2 * HEAD_DIM)
DIFF_VDIM = 2 * HEAD_DIM
DIFF_IN = 4 * DIFF_HEADS * HEAD_DIM + DIFF_HEADS * DIFF_VDIM
DIFF_OUT = DIFF_HEADS * DIFF_VDIM
DIFF_LAMBDA_STD = 0.1

NSA_HEADS = D_MODEL // HEAD_DIM
NSA_GROUPS = 4
NSA_CMP_LEN = 32
NSA_CMP_STRIDE = 16
NSA_CMP_HIDDEN = 128
NSA_SLC_BLOCK = 64
NSA_TOPN = 16
NSA_N_LOCAL = 2
NSA_WINDOW = 512
NSA_Q_CHUNK = 16
NSA_FORCE_BONUS = 1e4
NSA_IN = NSA_HEADS * HEAD_DIM + 6 * NSA_GROUPS * HEAD_DIM + 3 * NSA_HEADS
NSA_OUT = NSA_HEADS * HEAD_DIM

MOBA_HEADS = D_MODEL // HEAD_DIM
MOBA_BLOCK = 256
MOBA_TOPK = 3
MOBA_Q_CHUNK = 8
MOBA_IN = 3 * MOBA_HEADS * HEAD_DIM
MOBA_OUT = MOBA_HEADS * HEAD_DIM

DIL_PATTERNS = ((128, 1), (512, 4), (2048, 16))
DIL_HEADS = 8
DIL_Q_BLOCK = 64
DIL_IN = len(DIL_PATTERNS) * 3 * DIL_HEADS * HEAD_DIM
DIL_OUT = DIL_HEADS * HEAD_DIM

D_FF = 2816
CONV_WIDTH = 3

kernel_name = "hybrid_interleaved_sparse_attn_trunk"


def _layers_of(kind):
    return len(range(kind, DEPTH, N_MIXERS))


def rms_norm(x, g):
    xf = x.astype(jnp.float32)
    y = xf * lax.rsqrt(jnp.mean(xf * xf, axis=-1, keepdims=True) + NORM_EPS)
    return (y * g.astype(jnp.float32)).astype(x.dtype)


def rope_tables(positions):
    inv = ROPE_THETA ** (-jnp.arange(0, HEAD_DIM, 2, dtype=jnp.float32) / HEAD_DIM)
    ang = positions.astype(jnp.float32)[..., None] * inv
    return jnp.cos(ang)[:, :, None, :], jnp.sin(ang)[:, :, None, :]


def apply_rope(x, cos, sin):
    xf = x.astype(jnp.float32)
    x1, x2 = jnp.split(xf, 2, axis=-1)
    return jnp.concatenate([x1 * cos - x2 * sin, x2 * cos + x1 * sin], axis=-1).astype(x.dtype)


def masked_softmax(s, mask):
    s = jnp.where(mask, s.astype(jnp.float32), NEG_INF)
    m = jnp.max(s, axis=-1, keepdims=True)
    e = jnp.where(mask, jnp.exp(s - m), 0.0)
    return e / jnp.maximum(jnp.sum(e, axis=-1, keepdims=True), 1e-30)


def diff_attention(h, w_in, w_out, lam_q1, lam_k1, lam_q2, lam_k2, subln_g, cos, sin, lambda_init):
    B, S, _ = h.shape
    H, dk, dv = DIFF_HEADS, HEAD_DIM, DIFF_VDIM
    q, k, v = jnp.split(h @ w_in, [2 * H * dk, 4 * H * dk], axis=-1)
    q = apply_rope(q.reshape(B, S, 2 * H, dk), cos, sin).reshape(B, S, H, 2, dk)
    k = apply_rope(k.reshape(B, S, 2 * H, dk), cos, sin).reshape(B, S, H, 2, dk)
    v = v.reshape(B, S, H, dv)
    f32 = jnp.float32
    lam = (jnp.exp(jnp.sum(lam_q1.astype(f32) * lam_k1.astype(f32)))
           - jnp.exp(jnp.sum(lam_q2.astype(f32) * lam_k2.astype(f32))) + lambda_init)
    scale = dk ** -0.5
    nblk = S // Q_BLOCK
    q_blocks = q.reshape(B, nblk, Q_BLOCK, H, 2, dk).transpose(1, 0, 2, 3, 4, 5)
    key_pos = jnp.arange(S)

    def block(args):
        qi, start = args
        s = jnp.einsum('bqhcd,bkhcd->bhcqk', qi, k) * scale
        qpos = start + jnp.arange(Q_BLOCK)
        p = masked_softmax(s, key_pos[None, :] <= qpos[:, None])
        a = p[:, :, 0] - lam * p[:, :, 1]
        return jnp.einsum('bhqk,bkhd->bqhd', a.astype(v.dtype), v)

    o = lax.map(block, (q_blocks, jnp.arange(nblk) * Q_BLOCK))
    o = o.transpose(1, 0, 2, 3, 4).reshape(B, S, H, dv)
    o = rms_norm(o, subln_g) * (1.0 - lambda_init)
    return o.reshape(B, S, H * dv) @ w_out


def nsa_attention(h, w_in, w_out, ck_pos, ck_w1, ck_w2, cv_pos, cv_w1, cv_w2, cos, sin):
    B, S, _ = h.shape
    H, G, dk = NSA_HEADS, NSA_GROUPS, HEAD_DIM
    R = H // G
    kvw = G * dk
    splits = np.cumsum([H * dk, kvw, kvw, kvw, kvw, kvw, kvw]).tolist()
    q, kc, vc, ks, vs, kw, vw, g_logits = jnp.split(h @ w_in, splits, axis=-1)
    q = apply_rope(q.reshape(B, S, H, dk), cos, sin)
    kc = apply_rope(kc.reshape(B, S, G, dk), cos, sin)
    ks = apply_rope(ks.reshape(B, S, G, dk), cos, sin)
    kw = apply_rope(kw.reshape(B, S, G, dk), cos, sin)
    vc, vs, vw = (t_.reshape(B, S, G, dk) for t_ in (vc, vs, vw))
    gates = jax.nn.sigmoid(g_logits.astype(jnp.float32)).reshape(B, S, G, R, 3)
    qg = q.reshape(B, S, G, R, dk)
    scale = dk ** -0.5
    t = jnp.arange(S)

    n_cmp = (S - NSA_CMP_LEN) // NSA_CMP_STRIDE + 1
    cidx = np.arange(n_cmp)[:, None] * NSA_CMP_STRIDE + np.arange(NSA_CMP_LEN)[None, :]

    def compress(z, pos, w1, w2):
        zb = z[:, cidx] + pos[:, None, :]
        hid = jax.nn.gelu(jnp.einsum('bnlgd,ldf->bngf', zb, w1))
        return jnp.einsum('bngf,fd->bngd', hid, w2)

    k_cmp = compress(kc, ck_pos, ck_w1, ck_w2)
    v_cmp = compress(vc, cv_pos, cv_w1, cv_w2)
    s_c = jnp.einsum('bsgrd,bngd->bgrsn', qg, k_cmp) * scale
    cmask = jnp.asarray(cidx[:, -1])[None, :] <= t[:, None]
    p_c = masked_softmax(s_c, cmask)
    o_c = jnp.einsum('bgrsn,bngd->bsgrd', p_c.astype(v_cmp.dtype), v_cmp)

    bs = NSA_SLC_BLOCK
    n_slc = S // bs
    s_start = np.arange(n_slc) * bs
    overlap = ((cidx[:, 0][:, None] <= (s_start + bs - 1)[None, :])
               & (cidx[:, -1][:, None] >= s_start[None, :])).astype(np.float32)
    imp = jnp.einsum('bgrsn,nj->bgsj', p_c, jnp.asarray(overlap))
    cur = t // bs
    jb = jnp.arange(n_slc)
    causal_ok = jb[None, :] <= cur[:, None]
    back = cur[:, None] - jb[None, :]
    forced = (jb[None, :] == 0) | ((back >= 0) & (back < NSA_N_LOCAL))
    score = jnp.where(causal_ok, imp + jnp.where(forced, NSA_FORCE_BONUS, 0.0), NEG_INF)
    n_top = min(NSA_TOPN, n_slc)
    top_val, top_idx = lax.top_k(score, n_top)
    top_ok = top_val > 0.5 * NEG_INF

    ks_blk = ks.reshape(B, n_slc, bs, G, dk).transpose(0, 3, 1, 2, 4)
    vs_blk = vs.reshape(B, n_slc, bs, G, dk).transpose(0, 3, 1, 2, 4)
    qc = NSA_Q_CHUNK
    nch = S // qc
    q_ch = qg.reshape(B, nch, qc, G, R, dk).transpose(1, 0, 3, 4, 2, 5)
    idx_ch = top_idx.reshape(B, G, nch, qc, n_top).transpose(2, 0, 1, 3, 4)
    ok_ch = top_ok.reshape(B, G, nch, qc, n_top).transpose(2, 0, 1, 3, 4)
    b_ix = jnp.arange(B)[:, None, None, None]
    g_ix = jnp.arange(G)[None, :, None, None]
    m_sel = n_top * bs

    def slc_chunk(args):
        qi, idx, ok, start = args
        kg = ks_blk[b_ix, g_ix, idx]
        vg = vs_blk[b_ix, g_ix, idx]
        s = jnp.einsum('bgrqd,bgqnkd->bgrqnk', qi, kg) * scale
        kpos = idx[..., None] * bs + jnp.arange(bs)
        qpos = start + jnp.arange(qc)
        mask = ok[..., None] & (kpos <= qpos[None, None, :, None, None])
        p = masked_softmax(s.reshape(B, G, R, qc, m_sel), mask.reshape(B, G, 1, qc, m_sel))
        return jnp.einsum('bgrqm,bgqmd->bqgrd', p.astype(vg.dtype), vg.reshape(B, G, qc, m_sel, dk))

    o_s = lax.map(slc_chunk, (q_ch, idx_ch, ok_ch, jnp.arange(nch) * qc))
    o_s = o_s.transpose(1, 0, 2, 3, 4, 5).reshape(B, S, G, R, dk)

    W = NSA_WINDOW
    kw_p = jnp.pad(kw, ((0, 0), (W, 0), (0, 0), (0, 0)))
    vw_p = jnp.pad(vw, ((0, 0), (W, 0), (0, 0), (0, 0)))
    nqb = S // Q_BLOCK
    q_blk = qg.reshape(B, nqb, Q_BLOCK, G, R, dk).transpose(1, 0, 2, 3, 4, 5)

    def win_block(args):
        qi, start = args
        kk = lax.dynamic_slice_in_dim(kw_p, start, W + Q_BLOCK, axis=1)
        vv = lax.dynamic_slice_in_dim(vw_p, start, W + Q_BLOCK, axis=1)
        s = jnp.einsum('bqgrd,bkgd->bgrqk', qi, kk) * scale
        qpos = start + jnp.arange(Q_BLOCK)
        kpos = start - W + jnp.arange(W + Q_BLOCK)
        dist = qpos[:, None] - kpos[None, :]
        mask = (dist >= 0) & (dist < W) & (kpos[None, :] >= 0)
        p = masked_softmax(s, mask)
        return jnp.einsum('bgrqk,bkgd->bqgrd', p.astype(vv.dtype), vv)

    o_w = lax.map(win_block, (q_blk, jnp.arange(nqb) * Q_BLOCK))
    o_w = o_w.transpose(1, 0, 2, 3, 4, 5).reshape(B, S, G, R, dk)

    o = (o_c * gates[..., 0:1] + o_s * gates[..., 1:2] + o_w * gates[..., 2:3]).astype(h.dtype)
    return o.reshape(B, S, H * dk) @ w_out


def moba_attention(h, w_in, w_out, cos, sin):
    B, S, _ = h.shape
    H, dk, L = MOBA_HEADS, HEAD_DIM, MOBA_BLOCK
    q, k, v = jnp.split(h @ w_in, 3, axis=-1)
    q = apply_rope(q.reshape(B, S, H, dk), cos, sin)
    k = apply_rope(k.reshape(B, S, H, dk), cos, sin)
    v = v.reshape(B, S, H, dk)
    scale = dk ** -0.5
    nb = -(-S // L)
    pad = nb * L - S
    k_p = jnp.pad(k, ((0, 0), (0, pad), (0, 0), (0, 0)))
    v_p = jnp.pad(v, ((0, 0), (0, pad), (0, 0), (0, 0)))
    k_blk = k_p.reshape(B, nb, L, H, dk).transpose(0, 3, 1, 2, 4)
    v_blk = v_p.reshape(B, nb, L, H, dk).transpose(0, 3, 1, 2, 4)
    k_mean = jnp.mean(k_blk.astype(jnp.float32), axis=3).astype(k.dtype)
    t = jnp.arange(S)
    past = jnp.arange(nb)[None, :] < (t // L)[:, None]
    gate = jnp.where(past, jnp.einsum('bshd,bhjd->bhsj', q, k_mean).astype(jnp.float32), NEG_INF)
    n_top = max(min(MOBA_TOPK, nb - 1), 1)
    top_val, top_idx = lax.top_k(gate, n_top)
    top_ok = top_val > 0.5 * NEG_INF
    qc = MOBA_Q_CHUNK
    nch = S // qc
    q_ch = q.reshape(B, nch, qc, H, dk).transpose(1, 0, 3, 2, 4)
    idx_ch = top_idx.reshape(B, H, nch, qc, n_top).transpose(2, 0, 1, 3, 4)
    ok_ch = top_ok.reshape(B, H, nch, qc, n_top).transpose(2, 0, 1, 3, 4)
    b_ix = jnp.arange(B)[:, None, None, None]
    h_ix = jnp.arange(H)[None, :, None, None]
    m_sel = n_top * L

    def chunk(args):
        qi, idx, ok, start = args
        kg = k_blk[b_ix, h_ix, idx]
        vg = v_blk[b_ix, h_ix, idx]
        s_sel = jnp.einsum('bhqd,bhqnkd->bhqnk', qi, kg).reshape(B, H, qc, m_sel) * scale
        mask_sel = jnp.broadcast_to(ok[..., None], (B, H, qc, n_top, L)).reshape(B, H, qc, m_sel)
        blk_start = (start // L) * L
        ko = lax.dynamic_slice_in_dim(k_p, blk_start, L, axis=1)
        vo = lax.dynamic_slice_in_dim(v_p, blk_start, L, axis=1)
        s_own = jnp.einsum('bhqd,bkhd->bhqk', qi, ko) * scale
        qpos = start + jnp.arange(qc)
        kpos = blk_start + jnp.arange(L)
        mask_own = jnp.broadcast_to(kpos[None, :] <= qpos[:, None], (B, H, qc, L))
        p = masked_softmax(jnp.concatenate([s_sel, s_own], axis=-1),
                           jnp.concatenate([mask_sel, mask_own], axis=-1)).astype(v.dtype)
        return (jnp.einsum('bhqm,bhqmd->bqhd', p[..., :m_sel], vg.reshape(B, H, qc, m_sel, dk))
                + jnp.einsum('bhqk,bkhd->bqhd', p[..., m_sel:], vo))

    o = lax.map(chunk, (q_ch, idx_ch, ok_ch, jnp.arange(nch) * qc))
    o = o.transpose(1, 0, 2, 3, 4).reshape(B, S, H * dk)
    return o @ w_out


def dilated_attention(h, w_in, w_out, cos, sin):
    B, S, _ = h.shape
    Hg, dk = DIL_HEADS, HEAD_DIM
    n_g = len(DIL_PATTERNS)
    proj = (h @ w_in).reshape(B, S, n_g, 3, Hg, dk)
    q = apply_rope(proj[:, :, :, 0].reshape(B, S, n_g * Hg, dk), cos, sin).reshape(B, S, n_g, Hg, dk)
    k = apply_rope(proj[:, :, :, 1].reshape(B, S, n_g * Hg, dk), cos, sin).reshape(B, S, n_g, Hg, dk)
    v = proj[:, :, :, 2]
    k_grp = [k[:, :, g] for g in range(n_g)]
    v_grp = [v[:, :, g] for g in range(n_g)]
    scale = dk ** -0.5
    qb = DIL_Q_BLOCK
    nqb = S // qb
    q_blk = q.reshape(B, nqb, qb, n_g, Hg, dk).transpose(1, 0, 2, 3, 4, 5)

    def block(args):
        qi, start = args
        qpos = start + jnp.arange(qb)
        outs, lses = [], []
        for g, (win, dil) in enumerate(DIL_PATTERNS):
            n_off = win // dil + 1
            kpos = qpos[:, None] - dil * jnp.arange(n_off)[None, :]
            valid = kpos >= 0
            kpc = jnp.maximum(kpos, 0)
            kg = k_grp[g][:, kpc]
            vg = v_grp[g][:, kpc]
            s = jnp.einsum('bqhd,bqjhd->bhqj', qi[:, :, g], kg).astype(jnp.float32) * scale
            s = jnp.where(valid, s, NEG_INF)
            m = jnp.max(s, axis=-1, keepdims=True)
            e = jnp.exp(s - m)
            den = jnp.sum(e, axis=-1, keepdims=True)
            outs.append(jnp.einsum('bhqj,bqjhd->bqhd', (e / den).astype(vg.dtype), vg))
            lses.append((m + jnp.log(den))[..., 0])
        alpha = jax.nn.softmax(jnp.stack(lses, axis=-1), axis=-1).transpose(0, 2, 1, 3)
        o = outs[0] * alpha[..., 0:1]
        for g in range(1, n_g):
            o = o + outs[g] * alpha[..., g:g + 1]
        return o.astype(h.dtype)

    o = lax.map(block, (q_blk, jnp.arange(nqb) * qb))
    o = o.transpose(1, 0, 2, 3, 4).reshape(B, S, Hg * dk)
    return o @ w_out


def conv_ffn(h, w_up, conv_w, conv_b, w_down):
    S = h.shape[1]
    u = h @ w_up
    u_pad = jnp.pad(u, ((0, 0), (CONV_WIDTH - 1, 0), (0, 0)))
    c = conv_b + u_pad[:, 0:S] * conv_w[0]
    for j in range(1, CONV_WIDTH):
        c = c + u_pad[:, j:j + S] * conv_w[j]
    gate, val = jnp.split(c, 2, axis=-1)
    return (jax.nn.gelu(gate, approximate=True) * val) @ w_down


def setup_inputs(seed: int = 0) -> dict:
    key = jax.random.key(seed)
    keys = iter(jax.random.split(key, 40))
    f32 = jnp.float32

    def dense(shape, fan_in):
        return jax.random.normal(next(keys), shape, f32) * (fan_in ** -0.5)

    def gain(shape):
        return 1.0 + 0.05 * jax.random.normal(next(keys), shape, f32)

    def small(shape, std):
        return std * jax.random.normal(next(keys), shape, f32)

    nA, nB, nC, nD = (_layers_of(kd) for kd in range(N_MIXERS))
    x = jax.random.normal(next(keys), (BATCH, SEQ, D_MODEL), f32)
    p = jax.random.normal(next(keys), (DEPTH, BATCH, SEQ, PLE_DIM), f32)
    start = jax.random.randint(next(keys), (BATCH, 1), 0, 4096, dtype=jnp.int32)
    positions = start + jnp.arange(SEQ, dtype=jnp.int32)[None, :]
    return {
        'x': x,
        'p': p,
        'positions': positions,
        'norm_gains': gain((DEPTH, 4, D_MODEL)),
        'ple_norm_g': gain((DEPTH, D_MODEL)),
        'ple_gate_w': dense((DEPTH, D_MODEL, D_MODEL), D_MODEL),
        'ple_proj_w': dense((DEPTH, PLE_DIM, D_MODEL), PLE_DIM),
        'ffn_w_up': dense((DEPTH, D_MODEL, 2 * D_FF), D_MODEL),
        'ffn_conv_w': dense((DEPTH, CONV_WIDTH, 2 * D_FF), CONV_WIDTH),
        'ffn_conv_b': small((DEPTH, 2 * D_FF), 0.01),
        'ffn_w_down': dense((DEPTH, D_FF, D_MODEL), D_FF),
        'diff_w_in': dense((nA, D_MODEL, DIFF_IN), D_MODEL),
        'diff_w_out': dense((nA, DIFF_OUT, D_MODEL), DIFF_OUT),
        'diff_lam_q1': small((nA, HEAD_DIM), DIFF_LAMBDA_STD),
        'diff_lam_k1': small((nA, HEAD_DIM), DIFF_LAMBDA_STD),
        'diff_lam_q2': small((nA, HEAD_DIM), DIFF_LAMBDA_STD),
        'diff_lam_k2': small((nA, HEAD_DIM), DIFF_LAMBDA_STD),
        'diff_subln_g': gain((nA, DIFF_VDIM)),
        'nsa_w_in': dense((nB, D_MODEL, NSA_IN), D_MODEL),
        'nsa_w_out': dense((nB, NSA_OUT, D_MODEL), NSA_OUT),
        'nsa_ck_pos': small((nB, NSA_CMP_LEN, HEAD_DIM), 0.1),
        'nsa_ck_w1': dense((nB, NSA_CMP_LEN, HEAD_DIM, NSA_CMP_HIDDEN), NSA_CMP_LEN * HEAD_DIM),
        'nsa_ck_w2': dense((nB, NSA_CMP_HIDDEN, HEAD_DIM), NSA_CMP_HIDDEN),
        'nsa_cv_pos': small((nB, NSA_CMP_LEN, HEAD_DIM), 0.1),
        'nsa_cv_w1': dense((nB, NSA_CMP_LEN, HEAD_DIM, NSA_CMP_HIDDEN), NSA_CMP_LEN * HEAD_DIM),
        'nsa_cv_w2': dense((nB, NSA_CMP_HIDDEN, HEAD_DIM), NSA_CMP_HIDDEN),
        'moba_w_in': dense((nC, D_MODEL, MOBA_IN), D_MODEL),
        'moba_w_out': dense((nC, MOBA_OUT, D_MODEL), MOBA_OUT),
        'dil_w_in': dense((nD, D_MODEL, DIL_IN), D_MODEL),
        'dil_w_out': dense((nD, DIL_OUT, D_MODEL), DIL_OUT),
    }


def reference(x, p, positions, norm_gains, ple_norm_g, ple_gate_w, ple_proj_w,
              ffn_w_up, ffn_conv_w, ffn_conv_b, ffn_w_down,
              diff_w_in, diff_w_out, diff_lam_q1, diff_lam_k1, diff_lam_q2, diff_lam_k2, diff_subln_g,
              nsa_w_in, nsa_w_out, nsa_ck_pos, nsa_ck_w1, nsa_ck_w2, nsa_cv_pos, nsa_cv_w1, nsa_cv_w2,
              moba_w_in, moba_w_out, dil_w_in, dil_w_out):
    cos, sin = rope_tables(positions)
    for i in range(DEPTH):
        kind, j = i % N_MIXERS, i // N_MIXERS
        hn = rms_norm(x, norm_gains[i, 0])
        if kind == 0:
            lambda_init = 0.8 - 0.6 * math.exp(-0.3 * i)
            y = diff_attention(hn, diff_w_in[j], diff_w_out[j], diff_lam_q1[j], diff_lam_k1[j],
                               diff_lam_q2[j], diff_lam_k2[j], diff_subln_g[j], cos, sin, lambda_init)
        elif kind == 1:
            y = nsa_attention(hn, nsa_w_in[j], nsa_w_out[j], nsa_ck_pos[j], nsa_ck_w1[j], nsa_ck_w2[j],
                              nsa_cv_pos[j], nsa_cv_w1[j], nsa_cv_w2[j], cos, sin)
        elif kind == 2:
            y = moba_attention(hn, moba_w_in[j], moba_w_out[j], cos, sin)
        else:
            y = dilated_attention(hn, dil_w_in[j], dil_w_out[j], cos, sin)
        x = x + rms_norm(y, norm_gains[i, 1])
        hn = rms_norm(x, norm_gains[i, 2])
        f = conv_ffn(hn, ffn_w_up[i], ffn_conv_w[i], ffn_conv_b[i], ffn_w_down[i])
        x = x + rms_norm(f, norm_gains[i, 3])
        gate = jax.nn.sigmoid(rms_norm(x, ple_norm_g[i]) @ ple_gate_w[i])
        x = x + gate * (p[i] @ ple_proj_w[i])
    return x
```

```python
import functools
import math

import jax
import jax.numpy as jnp
from jax import lax
from jax.experimental import pallas as pl
from jax.experimental.pallas import tpu as pltpu

F32 = jnp.float32
BF16 = jnp.bfloat16

D_MODEL = 1024
DEPTH = 4
N_MIXERS = 4
HEAD_DIM = 64
ROPE_THETA = 10000.0
NORM_EPS = 1e-6
PLE_DIM = 256
NEG = -1e30

DIFF_HEADS = D_MODEL // (2 * HEAD_DIM)

NSA_HEADS = D_MODEL // HEAD_DIM
NSA_GROUPS = 4
NSA_REP = NSA_HEADS // NSA_GROUPS
NSA_CMP_LEN = 32
NSA_CMP_STRIDE = 16
NSA_CMP_HIDDEN = 128
NSA_SLC_BLOCK = 64
NSA_TOPN = 16
NSA_N_LOCAL = 2
NSA_WINDOW = 512
NSA_FORCE_BONUS = 1e4

MOBA_HEADS = D_MODEL // HEAD_DIM
MOBA_BLOCK = 256
MOBA_TOPK = 3

DIL_PATTERNS = ((128, 1), (512, 4), (2048, 16))
DIL_HEADS = 8

D_FF = 2816
CONV_WIDTH = 3

LANES = 128
MXU_COLS = 256
TQ = 256
LOG2E = 1.4426950408889634
Q_SCALE = (HEAD_DIM ** -0.5) * LOG2E
VMEM_LIMIT = 56 * 1024 * 1024


def _cparams(sem):
    return pltpu.CompilerParams(dimension_semantics=sem, vmem_limit_bytes=VMEM_LIMIT)


def _rms(x, g):
    return x * lax.rsqrt(jnp.mean(x * x, axis=-1, keepdims=True) + NORM_EPS) * g


def _dot(a, b):
    return jnp.dot(a, b, preferred_element_type=F32)


def _dot_nt(a, b, precision=None):
    return lax.dot_general(a, b, (((1,), (1,)), ((), ())), preferred_element_type=F32,
                           precision=precision)


def _lane_iota(shape):
    return lax.broadcasted_iota(jnp.int32, shape, 1)


def _row_iota(shape):
    return lax.broadcasted_iota(jnp.int32, shape, 0)


def _rope_kernel(pos_ref, inv_ref, cos_ref, sin_ref):
    ang = pos_ref[...].astype(F32) * inv_ref[...]
    lane = _lane_iota(ang.shape)
    cos_ref[...] = jnp.cos(ang)
    sin_ref[...] = jnp.where((lane & 63) < 32, -jnp.sin(ang), jnp.sin(ang))


def _rope_tables(positions):
    t = positions.size
    inv = ROPE_THETA ** (-jnp.arange(0, HEAD_DIM, 2, dtype=F32) / HEAD_DIM)
    inv = jnp.tile(inv, 4)[None, :]
    tm = 2048
    return pl.pallas_call(
        _rope_kernel,
        grid=(t // tm,),
        in_specs=[pl.BlockSpec((tm, 1), lambda i: (i, 0)),
                  pl.BlockSpec((1, LANES), lambda i: (0, 0))],
        out_specs=[pl.BlockSpec((tm, LANES), lambda i: (i, 0))] * 2,
        out_shape=[jax.ShapeDtypeStruct((t, LANES), F32)] * 2,
        compiler_params=_cparams(("parallel",)),
        name="rope_tables",
    )(positions.reshape(t, 1), inv)


def _proj_kernel(x_ref, g_ref, w_ref, cos_ref, sin_ref, *o_refs, blocks):
    hn = _rms(x_ref[...], g_ref[...]).astype(BF16)
    cos = cos_ref[...]
    sin = sin_ref[...]
    first = (_lane_iota(cos.shape) & 63) < 32
    n_chunks = len(blocks) // 2
    for c in range(n_chunks):
        acc = _dot(hn, w_ref[:, c * MXU_COLS:(c + 1) * MXU_COLS])
        for s in range(2):
            out_idx, out_blk, rope, scale = blocks[2 * c + s]
            if out_idx is None:
                continue
            a = acc[:, s * LANES:(s + 1) * LANES]
            if rope:
                rot = jnp.where(first, pltpu.roll(a, 96, 1), pltpu.roll(a, 32, 1))
                a = a * cos + rot * sin
            if scale != 1.0:
                a = a * scale
            o_ref = o_refs[out_idx]
            o_ref[:, out_blk * LANES:(out_blk + 1) * LANES] = a.astype(o_ref.dtype)


def _proj(x, g, w, cos, sin, blocks, outs, name):
    t = x.shape[0]
    n = w.shape[1]
    assert n == len(blocks) * LANES and len(blocks) % 2 == 0
    tm = 512
    return pl.pallas_call(
        functools.partial(_proj_kernel, blocks=tuple(blocks)),
        grid=(t // tm,),
        in_specs=[pl.BlockSpec((tm, D_MODEL), lambda i: (i, 0)),
                  pl.BlockSpec((1, D_MODEL), lambda i: (0, 0)),
                  pl.BlockSpec((D_MODEL, n), lambda i: (0, 0)),
                  pl.BlockSpec((tm, LANES), lambda i: (i, 0)),
                  pl.BlockSpec((tm, LANES), lambda i: (i, 0))],
        out_specs=[pl.BlockSpec((tm, nb * LANES), lambda i: (i, 0)) for nb, _ in outs],
        out_shape=[jax.ShapeDtypeStruct((t, nb * LANES), dt) for nb, dt in outs],
        compiler_params=_cparams(("parallel",)),
        name=name,
    )(x, g.reshape(1, D_MODEL), w, cos, sin)


def _out_kernel(o_ref, w_ref, x_ref, g_ref, y_ref):
    y = _dot(o_ref[...], w_ref[...])
    y_ref[...] = x_ref[...] + _rms(y, g_ref[...])


def _out_proj(o, w, x, g, name):
    t, k = o.shape
    tm = 512
    return pl.pallas_call(
        _out_kernel,
        grid=(t // tm,),
        in_specs=[pl.BlockSpec((tm, k), lambda i: (i, 0)),
                  pl.BlockSpec((k, D_MODEL), lambda i: (0, 0)),
                  pl.BlockSpec((tm, D_MODEL), lambda i: (i, 0)),
                  pl.BlockSpec((1, D_MODEL), lambda i: (0, 0))],
        out_specs=pl.BlockSpec((tm, D_MODEL), lambda i: (i, 0)),
        out_shape=jax.ShapeDtypeStruct((t, D_MODEL), F32),
        compiler_params=_cparams(("parallel",)),
        name=name,
    )(o, w, x, g.reshape(1, D_MODEL))


FF_CHUNK = 256
FF_TM = 1024
FF_HALO = 16


def _gelu_tanh(x):
    return 0.5 * x * (1.0 + jnp.tanh(math.sqrt(2.0 / math.pi) * (x + 0.044715 * (x * x * x))))


def _ffn_kernel(x_ref, xh_ref, g2_ref, wg_ref, wv_ref, cwg_ref, cwv_ref, cbg_ref, cbv_ref, wd_ref,
                g3_ref, pg_ref, pgw_ref, ppw_ref, p_ref, y_ref, hn_s, acc_s, ug_s, uv_s, *, seq):
    i = pl.program_id(0)
    c = pl.program_id(1)
    tm = x_ref.shape[0]

    @pl.when(c == 0)
    def _():
        hn_s[FF_HALO:, :] = _rms(x_ref[...], g2_ref[...]).astype(BF16)
        halo = _rms(xh_ref[...], g2_ref[...])
        seq_start = (i * tm) % seq == 0
        hn_s[:FF_HALO, :] = jnp.where(seq_start, 0.0, halo).astype(BF16)
        acc_s[...] = jnp.zeros_like(acc_s)

    hn = hn_s[...]
    ug_s[...] = _dot(hn, wg_ref[...])
    uv_s[...] = _dot(hn, wv_ref[...])

    def conv(u_s, cw_ref, cb_ref):
        out = cb_ref[...] + u_s[pl.ds(FF_HALO - 2, tm), :] * cw_ref[0:1, :]
        out = out + u_s[pl.ds(FF_HALO - 1, tm), :] * cw_ref[1:2, :]
        return out + u_s[pl.ds(FF_HALO, tm), :] * cw_ref[2:3, :]

    act = _gelu_tanh(conv(ug_s, cwg_ref, cbg_ref)) * conv(uv_s, cwv_ref, cbv_ref)
    acc_s[...] += _dot(act.astype(BF16), wd_ref[...])

    @pl.when(c == pl.num_programs(1) - 1)
    def _():
        x2 = x_ref[...] + _rms(acc_s[...], g3_ref[...])
        gate = jax.nn.sigmoid(_dot(_rms(x2, pg_ref[...]).astype(BF16), pgw_ref[...]))
        y_ref[...] = x2 + gate * _dot(p_ref[...].astype(BF16), ppw_ref[...])


def _ffn(x, g2, w_up, conv_w, conv_b, w_down, g3, ple_g, ple_gate_w, ple_proj_w, p, seq, name):
    t = x.shape[0]
    tm = FF_TM
    n_c = D_FF // FF_CHUNK
    hb = tm // FF_HALO
    row = lambda v: v.reshape(1, -1)
    const = lambda i, c: (0, 0)
    return pl.pallas_call(
        functools.partial(_ffn_kernel, seq=seq),
        grid=(t // tm, n_c),
        in_specs=[pl.BlockSpec((tm, D_MODEL), lambda i, c: (i, 0)),
                  pl.BlockSpec((FF_HALO, D_MODEL), lambda i, c: (jnp.maximum(i * hb - 1, 0), 0)),
                  pl.BlockSpec((1, D_MODEL), const),
                  pl.BlockSpec((D_MODEL, FF_CHUNK), lambda i, c: (0, c)),
                  pl.BlockSpec((D_MODEL, FF_CHUNK), lambda i, c: (0, n_c + c)),
                  pl.BlockSpec((CONV_WIDTH, FF_CHUNK), lambda i, c: (0, c)),
                  pl.BlockSpec((CONV_WIDTH, FF_CHUNK), lambda i, c: (0, n_c + c)),
                  pl.BlockSpec((1, FF_CHUNK), lambda i, c: (0, c)),
                  pl.BlockSpec((1, FF_CHUNK), lambda i, c: (0, n_c + c)),
                  pl.BlockSpec((FF_CHUNK, D_MODEL), lambda i, c: (c, 0)),
                  pl.BlockSpec((1, D_MODEL), const),
                  pl.BlockSpec((1, D_MODEL), const),
                  pl.BlockSpec((D_MODEL, D_MODEL), const),
                  pl.BlockSpec((PLE_DIM, D_MODEL), const),
                  pl.BlockSpec((tm, PLE_DIM), lambda i, c: (i, 0))],
        out_specs=pl.BlockSpec((tm, D_MODEL), lambda i, c: (i, 0)),
        out_shape=jax.ShapeDtypeStruct((t, D_MODEL), F32),
        scratch_shapes=[pltpu.VMEM((tm + FF_HALO, D_MODEL), BF16),
                        pltpu.VMEM((tm, D_MODEL), F32),
                        pltpu.VMEM((tm + FF_HALO, FF_CHUNK), F32),
                        pltpu.VMEM((tm + FF_HALO, FF_CHUNK), F32)],
        compiler_params=_cparams(("parallel", "arbitrary")),
        name=name,
    )(x, x, row(g2), w_up, w_up, conv_w, conv_w, row(conv_b), row(conv_b), w_down,
      row(g3), row(ple_g), ple_gate_w, ple_proj_w, p)


def _half_masks(shape):
    lane = _lane_iota(shape)
    return lane < HEAD_DIM, lane >= HEAD_DIM


def _split_heads(q):
    lo, _ = _half_masks(q.shape)
    qf = q.astype(F32)
    return jnp.where(lo, qf, 0.0).astype(q.dtype), jnp.where(lo, 0.0, qf).astype(q.dtype)


def _softmax_init(m_s, l_s, acc_s, idx):
    m_s[idx] = jnp.full(m_s.shape[1:], NEG, F32)
    l_s[idx] = jnp.zeros(l_s.shape[1:], F32)
    acc_s[idx] = jnp.zeros(acc_s.shape[1:], F32)


def _softmax_step(s, v, m_s, l_s, acc_s, idx):
    reps = s.shape[1] // LANES
    m_prev = m_s[idx]
    m_next = jnp.maximum(m_prev, jnp.max(s, axis=1, keepdims=True))
    p = jnp.exp2(s - jnp.concatenate([m_next] * reps, axis=1))
    alpha = jnp.exp2(m_prev - m_next)
    l_s[idx] = alpha * l_s[idx] + jnp.sum(p, axis=1, keepdims=True)
    acc_s[idx] = alpha * acc_s[idx] + _dot(p.astype(BF16), v)
    m_s[idx] = m_next


def _tile_rc(tq, tk):
    return _row_iota((tq, tk)), _lane_iota((tq, tk))


def _diff_kernel(q_ref, k_ref, v_ref, lam_ref, sg_ref, o_ref, m_s, l_s, acc_s, *, lambda_init):
    i = pl.program_id(2)
    q0, q1 = _split_heads(q_ref[...])
    row, col = _tile_rc(TQ, TQ)
    causal = col <= row

    def tile(j, diag):
        start = pl.multiple_of(j * TQ, TQ)
        k = k_ref[pl.ds(start, TQ), :]
        v = v_ref[pl.ds(start, TQ), :]
        for c, qc in enumerate((q0, q1)):
            s = _dot_nt(qc, k)
            if diag:
                s = jnp.where(causal, s, NEG)
            _softmax_step(s, v, m_s, l_s, acc_s, c)

    for c in range(2):
        _softmax_init(m_s, l_s, acc_s, c)
    tile(i, True)
    lax.fori_loop(0, i, lambda j, _: tile(j, False), None)

    lamv = lam_ref[...]
    lam = (jnp.exp(jnp.sum(lamv[0:1] * lamv[1:2], axis=1, keepdims=True))
           - jnp.exp(jnp.sum(lamv[2:3] * lamv[3:4], axis=1, keepdims=True)) + lambda_init)
    o = acc_s[0] / l_s[0] - lam * (acc_s[1] / l_s[1])
    o_ref[...] = (_rms(o, sg_ref[...]) * (1.0 - lambda_init)).astype(o_ref.dtype)


def _diff_attention(qkv, lam_rows, subln_g, batch, seq, lambda_init):
    t = batch * seq
    nq = seq // TQ
    h = DIFF_HEADS
    return pl.pallas_call(
        functools.partial(_diff_kernel, lambda_init=lambda_init),
        grid=(batch, h, nq),
        in_specs=[pl.BlockSpec((TQ, LANES), lambda b, hh, i: (b * nq + i, hh)),
                  pl.BlockSpec((seq, LANES), lambda b, hh, i: (b, h + hh)),
                  pl.BlockSpec((seq, LANES), lambda b, hh, i: (b, 2 * h + hh)),
                  pl.BlockSpec((8, LANES), lambda b, hh, i: (0, 0)),
                  pl.BlockSpec((1, LANES), lambda b, hh, i: (0, 0))],
        out_specs=pl.BlockSpec((TQ, LANES), lambda b, hh, i: (b * nq + i, hh)),
        out_shape=jax.ShapeDtypeStruct((t, h * LANES), BF16),
        scratch_shapes=[pltpu.VMEM((2, TQ, LANES), F32)] * 3,
        compiler_params=_cparams(("parallel", "parallel", "arbitrary")),
        name="diff_attention",
    )(qkv, qkv, qkv, lam_rows, subln_g.reshape(1, LANES))


def _rank_rows(score, n_rows, valid_fn):
    ridx = _row_iota(score.shape)
    cnt = jnp.zeros(score.shape, F32)
    for j in range(n_rows):
        other = score[j:j + 1, :]
        ahead = jnp.where(other > score, 1.0, jnp.where(other == score, jnp.where(ridx > j, 1.0, 0.0), 0.0))
        v = valid_fn(j)
        cnt = cnt + (ahead if v is None else jnp.where(v, ahead, 0.0))
    return cnt


def _transpose_01(sel_t, eye):
    r = sel_t.shape[0]
    padded = jnp.concatenate([sel_t, jnp.zeros((LANES - r, sel_t.shape[1]), sel_t.dtype)], axis=0)
    return _dot_nt(eye, padded.astype(BF16))


def _eye(n):
    row, col = _tile_rc(n, n)
    return jnp.where(row == col, 1.0, 0.0).astype(BF16)


def _moba_kernel(q_ref, k_ref, v_ref, o_ref, km_s, m_s, l_s, acc_s):
    i = pl.program_id(2)
    nb = k_ref.shape[0] // MOBA_BLOCK

    @pl.when(i == 0)
    def _():
        rows = [jnp.mean(k_ref[j * MOBA_BLOCK:(j + 1) * MOBA_BLOCK, :].astype(F32), axis=0, keepdims=True)
                for j in range(nb)]
        km_s[...] = jnp.concatenate(rows, axis=0)

    q = q_ref[...]
    heads = _split_heads(q)
    row, col = _tile_rc(TQ, TQ)
    causal = col <= row
    eye = _eye(TQ)
    sel = []
    for a in range(2):
        gate_t = _dot_nt(km_s[...], heads[a].astype(F32), precision=lax.Precision.HIGHEST)
        rank = _rank_rows(gate_t, nb, lambda j: j < i)
        ridx = _row_iota(gate_t.shape)
        sel_t = jnp.where(ridx < i, jnp.where(rank < MOBA_TOPK, 1.0, 0.0), 0.0)
        sel.append(_transpose_01(sel_t, eye).astype(BF16))
        _softmax_init(m_s, l_s, acc_s, a)

    def tile(j, diag):
        start = pl.multiple_of(j * TQ, TQ)
        k = k_ref[pl.ds(start, TQ), :]
        v = v_ref[pl.ds(start, TQ), :]
        if not diag:
            pick = jnp.where(_row_iota((LANES, LANES)) == j, 1.0, 0.0).astype(BF16)
        for a in range(2):
            s = _dot_nt(heads[a], k)
            if diag:
                s = jnp.where(causal, s, NEG)
            else:
                on = _dot(sel[a], pick)
                s = jnp.where(jnp.concatenate([on] * (TQ // LANES), axis=1) > 0.5, s, NEG)
            _softmax_step(s, v, m_s, l_s, acc_s, a)

    tile(i, True)
    lax.fori_loop(0, i, lambda j, _: tile(j, False), None)
    lo, _ = _half_masks((TQ, LANES))
    o_ref[...] = jnp.where(lo, acc_s[0] / l_s[0], acc_s[1] / l_s[1]).astype(o_ref.dtype)


def _moba_attention(qkv, batch, seq):
    assert MOBA_BLOCK == TQ and seq % MOBA_BLOCK == 0
    t = batch * seq
    nq = seq // TQ
    hp = MOBA_HEADS // 2
    return pl.pallas_call(
        _moba_kernel,
        grid=(batch, hp, nq),
        in_specs=[pl.BlockSpec((TQ, LANES), lambda b, hh, i: (b * nq + i, hh)),
                  pl.BlockSpec((seq, LANES), lambda b, hh, i: (b, hp + hh)),
                  pl.BlockSpec((seq, LANES), lambda b, hh, i: (b, 2 * hp + hh))],
        out_specs=pl.BlockSpec((TQ, LANES), lambda b, hh, i: (b * nq + i, hh)),
        out_shape=jax.ShapeDtypeStruct((t, hp * LANES), BF16),
        scratch_shapes=[pltpu.VMEM((seq // MOBA_BLOCK, LANES), F32)]
        + [pltpu.VMEM((2, TQ, LANES), F32)] * 3,
        compiler_params=_cparams(("parallel", "parallel", "arbitrary")),
        name="moba_attention",
    )(qkv, qkv, qkv)


def _dil_kernel(q0_ref, q1_ref, q2_ref, k0_ref, k1_ref, k2_ref, v0_ref, v1_ref, v2_ref, o_ref, m_s, l_s, acc_s):
    i = pl.program_id(2)
    row, col = _tile_rc(TQ, TQ)
    diff = row - col
    (w0, _), (w1, d1), (w2, d2) = DIL_PATTERNS
    assert w0 < TQ and w1 == 2 * TQ and w2 >= k0_ref.shape[0]
    heads = [_split_heads(r[...]) for r in (q0_ref, q1_ref, q2_ref)]

    def tile(g, j, mask):
        k_ref, v_ref = ((k0_ref, v0_ref), (k1_ref, v1_ref), (k2_ref, v2_ref))[g]
        start = pl.multiple_of(j * TQ, TQ)
        k = k_ref[pl.ds(start, TQ), :]
        v = v_ref[pl.ds(start, TQ), :]
        for a in range(2):
            s = jnp.where(mask, _dot_nt(heads[g][a], k), NEG)
            _softmax_step(s, v, m_s, l_s, acc_s, a)

    for a in range(2):
        _softmax_init(m_s, l_s, acc_s, a)
    on4 = (diff & (d1 - 1)) == 0
    on16 = (diff & (d2 - 1)) == 0
    causal = diff >= 0
    tile(0, i, jnp.logical_and(causal, diff <= w0))

    @pl.when(i >= 1)
    def _():
        tile(0, i - 1, diff + TQ <= w0)
        tile(1, i - 1, on4)

    tile(1, i, jnp.logical_and(causal, on4))

    @pl.when(i >= 2)
    def _():
        tile(1, i - 2, jnp.logical_and(on4, diff <= 0))

    tile(2, i, jnp.logical_and(causal, on16))
    lax.fori_loop(0, i, lambda j, _: tile(2, j, on16), None)
    lo, _ = _half_masks((TQ, LANES))
    o_ref[...] = jnp.where(lo, acc_s[0] / l_s[0], acc_s[1] / l_s[1]).astype(o_ref.dtype)


def _dil_attention(qkv, batch, seq):
    t = batch * seq
    nq = seq // TQ
    hp = DIL_HEADS // 2
    per_g = 3 * hp

    def qspec(g):
        return pl.BlockSpec((TQ, LANES), lambda b, hh, i: (b * nq + i, g * per_g + hh))

    def kvspec(g, which):
        return pl.BlockSpec((seq, LANES), lambda b, hh, i: (b, g * per_g + which * hp + hh))

    return pl.pallas_call(
        _dil_kernel,
        grid=(batch, hp, nq),
        in_specs=[qspec(0), qspec(1), qspec(2), kvspec(0, 1), kvspec(1, 1), kvspec(2, 1),
                  kvspec(0, 2), kvspec(1, 2), kvspec(2, 2)],
        out_specs=pl.BlockSpec((TQ, LANES), lambda b, hh, i: (b * nq + i, hh)),
        out_shape=jax.ShapeDtypeStruct((t, hp * LANES), BF16),
        scratch_shapes=[pltpu.VMEM((2, TQ, LANES), F32)] * 3,
        compiler_params=_cparams(("parallel", "parallel", "arbitrary")),
        name="dilated_attention",
    )(*([qkv] * 9))


N_CMP_PAD = 128


def _cmp_kernel(zk_ref, zv_ref, w1k_ref, w1v_ref, pk_ref, pv_ref, w1fk_ref, w1fv_ref, w2k_ref, w2v_ref,
                ko_ref, vo_ref):
    def one(z_ref, w1_ref, pos_ref, w1f_ref, w2_ref, o_ref):
        z = z_ref[...]
        a = _dot(z, w1_ref[0])
        b = _dot(z, w1_ref[1])
        b_next = jnp.concatenate([b[1:], jnp.zeros_like(b[:1])], axis=0)
        const = _dot(pos_ref[...], w1f_ref[...])[0:1]
        hid = a + b_next + jnp.concatenate([const] * NSA_GROUPS, axis=1)
        o_ref[...] = _dot(jax.nn.gelu(hid, approximate=True).astype(BF16), w2_ref[...]).astype(o_ref.dtype)

    one(zk_ref, w1k_ref, pk_ref, w1fk_ref, w2k_ref, ko_ref)
    one(zv_ref, w1v_ref, pv_ref, w1fv_ref, w2v_ref, vo_ref)


def _nsa_compress(kc, vc, ck_pos, ck_w1, ck_w2, cv_pos, cv_w1, cv_w2, batch, seq):
    g, dk, f = NSA_GROUPS, HEAD_DIM, NSA_CMP_HIDDEN
    half = NSA_CMP_LEN // 2
    assert NSA_CMP_STRIDE == half and seq // half == N_CMP_PAD
    zk = kc.reshape(batch * N_CMP_PAD, half * g * dk)
    zv = vc.reshape(batch * N_CMP_PAD, half * g * dk)

    def expand_w1(w1):
        w = w1.reshape(2, half, dk, f)
        eye = jnp.eye(g, dtype=w1.dtype)
        big = jnp.einsum('acdf,gh->acgdhf', w, eye)
        return big.reshape(2, half * g * dk, g * f).astype(BF16)

    def expand_w2(w2):
        dup = jnp.concatenate([w2, w2], axis=1)
        eye = jnp.eye(g, dtype=w2.dtype)
        return jnp.einsum('fd,gh->gfhd', dup, eye).reshape(g * f, g * LANES).astype(BF16)

    def pos_rows(pos):
        flat = pos.reshape(1, NSA_CMP_LEN * dk)
        return jnp.concatenate([flat, jnp.zeros((7, NSA_CMP_LEN * dk), pos.dtype)], axis=0).astype(BF16)

    kd = half * g * dk
    const2 = lambda b: (0, 0)
    const3 = lambda b: (0, 0, 0)
    return pl.pallas_call(
        _cmp_kernel,
        grid=(batch,),
        in_specs=[pl.BlockSpec((N_CMP_PAD, kd), lambda b: (b, 0)),
                  pl.BlockSpec((N_CMP_PAD, kd), lambda b: (b, 0)),
                  pl.BlockSpec((2, kd, g * f), const3),
                  pl.BlockSpec((2, kd, g * f), const3),
                  pl.BlockSpec((8, NSA_CMP_LEN * dk), const2),
                  pl.BlockSpec((8, NSA_CMP_LEN * dk), const2),
                  pl.BlockSpec((NSA_CMP_LEN * dk, f), const2),
                  pl.BlockSpec((NSA_CMP_LEN * dk, f), const2),
                  pl.BlockSpec((g * f, g * LANES), const2),
                  pl.BlockSpec((g * f, g * LANES), const2)],
        out_specs=[pl.BlockSpec((N_CMP_PAD, g * LANES), lambda b: (b, 0))] * 2,
        out_shape=[jax.ShapeDtypeStruct((batch * N_CMP_PAD, g * LANES), BF16)] * 2,
        compiler_params=_cparams(("parallel",)),
        name="nsa_compress",
    )(zk, zv, expand_w1(ck_w1), expand_w1(cv_w1), pos_rows(ck_pos), pos_rows(cv_pos),
      ck_w1.reshape(NSA_CMP_LEN * dk, f).astype(BF16), cv_w1.reshape(NSA_CMP_LEN * dk, f).astype(BF16),
      expand_w2(ck_w2), expand_w2(cv_w2))


def _nsa_kernel(q_ref, ks_ref, vs_ref, kw_ref, vw_ref, kc_ref, vc_ref, gate_ref, ov_ref, o_ref,
                m_s, l_s, acc_s, oc_s):
    i = pl.program_id(2)
    n_slc = ks_ref.shape[0] // NSA_SLC_BLOCK
    per_tile = TQ // NSA_SLC_BLOCK
    slc_shift = NSA_SLC_BLOCK.bit_length() - 1
    assert 1 << slc_shift == NSA_SLC_BLOCK
    row, col = _tile_rc(TQ, TQ)
    diff = row - col
    causal = diff >= 0
    lo, _ = _half_masks((TQ, LANES))
    heads = []
    for pr in range(NSA_REP // 2):
        heads.extend(_split_heads(q_ref[:, pr * LANES:(pr + 1) * LANES]))

    kc = kc_ref[...]
    vc = vc_ref[...]
    t_row = i * TQ + _row_iota((TQ, N_CMP_PAD))
    n_col = _lane_iota((TQ, N_CMP_PAD))
    cmask = jnp.logical_and(n_col * NSA_CMP_STRIDE + (NSA_CMP_LEN - 1) <= t_row, n_col < N_CMP_PAD - 1)
    p_sum = jnp.zeros((TQ, N_CMP_PAD), F32)
    for r in range(NSA_REP):
        s = jnp.where(cmask, _dot_nt(heads[r], kc), NEG)
        e = jnp.where(cmask, jnp.exp2(s - jnp.max(s, axis=1, keepdims=True)), 0.0)
        p = e / jnp.maximum(jnp.sum(e, axis=1, keepdims=True), 1e-30)
        p_sum = p_sum + p
        oc_s[r] = _dot(p.astype(BF16), vc)

    imp_t = _dot_nt(ov_ref[...], p_sum, precision=lax.Precision.HIGHEST)
    jb = _row_iota((n_slc, TQ))
    cur = (i * TQ + _lane_iota((n_slc, TQ))) >> slc_shift
    back = cur - jb
    ok = back >= 0
    forced = jnp.logical_or(jb == 0, jnp.logical_and(ok, back < NSA_N_LOCAL))
    score = jnp.where(ok, imp_t + jnp.where(forced, NSA_FORCE_BONUS, 0.0), NEG)
    rank = _rank_rows(score, n_slc, lambda j: None)
    sel_t = jnp.where(ok, jnp.where(rank < min(NSA_TOPN, n_slc), 1.0, 0.0), 0.0)
    sel = _transpose_01(sel_t, _eye(TQ)).astype(BF16)

    for r in range(NSA_REP):
        _softmax_init(m_s, l_s, acc_s, r)

    def slc_tile(j, diag):
        start = pl.multiple_of(j * TQ, TQ)
        k = ks_ref[pl.ds(start, TQ), :]
        v = vs_ref[pl.ds(start, TQ), :]
        expand = jnp.where(_row_iota((LANES, TQ)) == j * per_tile + (_lane_iota((LANES, TQ)) >> slc_shift),
                           1.0, 0.0).astype(BF16)
        on = _dot(sel, expand)
        if diag:
            on = jnp.where(causal, on, 0.0)
        mask = on > 0.5
        for r in range(NSA_REP):
            s = jnp.where(mask, _dot_nt(heads[r], k), NEG)
            _softmax_step(s, v, m_s, l_s, acc_s, r)

    slc_tile(i, True)
    lax.fori_loop(0, i, lambda j, _: slc_tile(j, False), None)
    gates = jax.nn.sigmoid(gate_ref[...])
    outs = []
    for r in range(NSA_REP):
        gate_c = gates[:, 3 * r:3 * r + 1]
        gate_s = gates[:, 3 * r + 1:3 * r + 2]
        outs.append(oc_s[r] * gate_c + (acc_s[r] / l_s[r]) * gate_s)

    for r in range(NSA_REP):
        _softmax_init(m_s, l_s, acc_s, r)

    def win_tile(j, mask):
        start = pl.multiple_of(j * TQ, TQ)
        k = kw_ref[pl.ds(start, TQ), :]
        v = vw_ref[pl.ds(start, TQ), :]
        for r in range(NSA_REP):
            s = _dot_nt(heads[r], k)
            if mask is not None:
                s = jnp.where(mask, s, NEG)
            _softmax_step(s, v, m_s, l_s, acc_s, r)

    assert NSA_WINDOW == 2 * TQ
    win_tile(i, causal)

    @pl.when(i >= 1)
    def _():
        win_tile(i - 1, None)

    @pl.when(i >= 2)
    def _():
        win_tile(i - 2, diff < 0)

    for pr in range(NSA_REP // 2):
        pair = []
        for r in (2 * pr, 2 * pr + 1):
            pair.append(outs[r] + (acc_s[r] / l_s[r]) * gates[:, 3 * r + 2:3 * r + 3])
        o_ref[:, pr * LANES:(pr + 1) * LANES] = jnp.where(lo, pair[0], pair[1]).astype(o_ref.dtype)


def _nsa_overlap_t(seq):
    n_slc = seq // NSA_SLC_BLOCK
    n = jnp.arange(N_CMP_PAD)
    first = n * NSA_CMP_STRIDE
    last = first + NSA_CMP_LEN - 1
    s_start = jnp.arange(n_slc) * NSA_SLC_BLOCK
    ov = (first[None, :] <= (s_start + NSA_SLC_BLOCK - 1)[:, None]) & (last[None, :] >= s_start[:, None])
    ov = ov & (n[None, :] < (seq - NSA_CMP_LEN) // NSA_CMP_STRIDE + 1)
    return ov.astype(F32)


def _nsa_attention(q, ks, vs, kw, vw, kcmp, vcmp, gates, batch, seq):
    t = batch * seq
    nq = seq // TQ
    g = NSA_GROUPS
    qb = NSA_REP // 2
    n_slc = seq // NSA_SLC_BLOCK
    kv = lambda: pl.BlockSpec((seq, LANES), lambda b, gg, i: (b, gg))
    cm = lambda: pl.BlockSpec((N_CMP_PAD, LANES), lambda b, gg, i: (b, gg))
    return pl.pallas_call(
        _nsa_kernel,
        grid=(batch, g, nq),
        in_specs=[pl.BlockSpec((TQ, qb * LANES), lambda b, gg, i: (b * nq + i, gg)),
                  kv(), kv(), kv(), kv(), cm(), cm(),
                  pl.BlockSpec((TQ, LANES), lambda b, gg, i: (b * nq + i, gg)),
                  pl.BlockSpec((n_slc, N_CMP_PAD), lambda b, gg, i: (0, 0))],
        out_specs=pl.BlockSpec((TQ, qb * LANES), lambda b, gg, i: (b * nq + i, gg)),
        out_shape=jax.ShapeDtypeStruct((t, NSA_HEADS * HEAD_DIM), BF16),
        scratch_shapes=[pltpu.VMEM((NSA_REP, TQ, LANES), F32)] * 4,
        compiler_params=_cparams(("parallel", "parallel", "arbitrary")),
        name="nsa_attention",
    )(q, ks, vs, kw, vw, kcmp, vcmp, gates, _nsa_overlap_t(seq))


def _plain_blocks(n_total, n_rope, n_q):
    return [(0, b, b < n_rope, Q_SCALE if b < n_q else 1.0) for b in range(n_total)]


def _mixer_diff(x, g, w_in, lam_q1, lam_k1, lam_q2, lam_k2, subln_g, cos, sin, batch, seq, lambda_init):
    h = DIFF_HEADS
    blocks = _plain_blocks(3 * h, 2 * h, h)
    (qkv,) = _proj(x, g, w_in.astype(BF16), cos, sin, blocks, [(3 * h, BF16)], "diff_in_proj")
    pad = lambda v: jnp.pad(v.astype(F32), (0, LANES - HEAD_DIM))
    lam_rows = jnp.stack([pad(lam_q1), pad(lam_k1), pad(lam_q2), pad(lam_k2)]
                         + [jnp.zeros((LANES,), F32)] * 4)
    return _diff_attention(qkv, lam_rows, subln_g, batch, seq, lambda_init)


def _mixer_moba(x, g, w_in, cos, sin, batch, seq):
    hp = MOBA_HEADS // 2
    blocks = _plain_blocks(3 * hp, 2 * hp, hp)
    (qkv,) = _proj(x, g, w_in.astype(BF16), cos, sin, blocks, [(3 * hp, BF16)], "moba_in_proj")
    return _moba_attention(qkv, batch, seq)


def _mixer_dil(x, g, w_in, cos, sin, batch, seq):
    hp = DIL_HEADS // 2
    blocks = []
    for gi in range(len(DIL_PATTERNS)):
        for which in range(3):
            for b in range(hp):
                blocks.append((0, len(blocks), which < 2, Q_SCALE if which == 0 else 1.0))
    (qkv,) = _proj(x, g, w_in.astype(BF16), cos, sin, blocks, [(len(blocks), BF16)], "dil_in_proj")
    return _dil_attention(qkv, batch, seq)


def _mixer_nsa(x, g, w_in, ck_pos, ck_w1, ck_w2, cv_pos, cv_w1, cv_w2, cos, sin, batch, seq):
    hq, gr, dk = NSA_HEADS, NSA_GROUPS, HEAD_DIM
    kvw = gr * dk
    o_q = 0
    o_kc, o_vc, o_ks, o_vs, o_kw, o_vw, o_g = (hq * dk + n * kvw for n in range(7))
    n_gate = 3 * hq

    def dup(off):
        cols = w_in[:, off:off + kvw].reshape(D_MODEL, gr, dk)
        return jnp.concatenate([cols, cols], axis=2).reshape(D_MODEL, gr * LANES)

    per_group = n_gate // gr
    gate_cols = jnp.pad(w_in[:, o_g:o_g + n_gate].reshape(D_MODEL, gr, per_group),
                        ((0, 0), (0, 0), (0, LANES - per_group))).reshape(D_MODEL, gr * LANES)
    w = jnp.concatenate([w_in[:, o_q:o_q + hq * dk], dup(o_ks), dup(o_kw), w_in[:, o_kc:o_kc + kvw],
                         dup(o_vs), dup(o_vw), w_in[:, o_vc:o_vc + kvw], gate_cols], axis=1).astype(BF16)
    nqb = hq * dk // LANES
    blocks = [(0, b, True, Q_SCALE) for b in range(nqb)]
    blocks += [(1, b, True, 1.0) for b in range(gr)]
    blocks += [(2, b, True, 1.0) for b in range(gr)]
    blocks += [(3, b, True, 1.0) for b in range(kvw // LANES)]
    blocks += [(4, b, False, 1.0) for b in range(gr)]
    blocks += [(5, b, False, 1.0) for b in range(gr)]
    blocks += [(6, b, False, 1.0) for b in range(kvw // LANES)]
    blocks += [(7, b, False, 1.0) for b in range(gr)]
    outs = [(nqb, BF16), (gr, BF16), (gr, BF16), (kvw // LANES, BF16), (gr, BF16), (gr, BF16),
            (kvw // LANES, BF16), (gr, F32)]
    q, ks, kw, kc, vs, vw, vc, gates = _proj(x, g, w, cos, sin, blocks, outs, "nsa_in_proj")
    kcmp, vcmp = _nsa_compress(kc, vc, ck_pos, ck_w1, ck_w2, cv_pos, cv_w1, cv_w2, batch, seq)
    return _nsa_attention(q, ks, vs, kw, vw, kcmp, vcmp, gates, batch, seq)


def kernel(x, p, positions, norm_gains, ple_norm_g, ple_gate_w, ple_proj_w, ffn_w_up, ffn_conv_w, ffn_conv_b, ffn_w_down, diff_w_in, diff_w_out, diff_lam_q1, diff_lam_k1, diff_lam_q2, diff_lam_k2, diff_subln_g, nsa_w_in, nsa_w_out, nsa_ck_pos, nsa_ck_w1, nsa_ck_w2, nsa_cv_pos, nsa_cv_w1, nsa_cv_w2, moba_w_in, moba_w_out, dil_w_in, dil_w_out):
    batch, seq, d = x.shape
    t = batch * seq
    cos, sin = _rope_tables(positions)
    xt = x.reshape(t, d)
    for i in range(DEPTH):
        kind, j = i % N_MIXERS, i // N_MIXERS
        g0 = norm_gains[i, 0]
        if kind == 0:
            lambda_init = 0.8 - 0.6 * math.exp(-0.3 * i)
            o = _mixer_diff(xt, g0, diff_w_in[j], diff_lam_q1[j], diff_lam_k1[j], diff_lam_q2[j],
                            diff_lam_k2[j], diff_subln_g[j], cos, sin, batch, seq, lambda_init)
            w_out = diff_w_out[j]
        elif kind == 1:
            o = _mixer_nsa(xt, g0, nsa_w_in[j], nsa_ck_pos[j], nsa_ck_w1[j], nsa_ck_w2[j],
                           nsa_cv_pos[j], nsa_cv_w1[j], nsa_cv_w2[j], cos, sin, batch, seq)
            w_out = nsa_w_out[j]
        elif kind == 2:
            o = _mixer_moba(xt, g0, moba_w_in[j], cos, sin, batch, seq)
            w_out = moba_w_out[j]
        else:
            o = _mixer_dil(xt, g0, dil_w_in[j], cos, sin, batch, seq)
            w_out = dil_w_out[j]
        xt = _out_proj(o, w_out.astype(BF16), xt, norm_gains[i, 1], "out_proj_%d" % i)
        xt = _ffn(xt, norm_gains[i, 2], ffn_w_up[i].astype(BF16), ffn_conv_w[i], ffn_conv_b[i],
                  ffn_w_down[i].astype(BF16), norm_gains[i, 3], ple_norm_g[i],
                  ple_gate_w[i].astype(BF16), ple_proj_w[i].astype(BF16), p[i].reshape(t, PLE_DIM),
                  seq, "ffn_%d" % i)
    return xt.reshape(batch, seq, d)
```

```python
import functools
import math

import jax
import jax.numpy as jnp
from jax import lax
from jax.experimental import pallas as pl
from jax.experimental.pallas import tpu as pltpu

F32 = jnp.float32
BF16 = jnp.bfloat16

D_MODEL = 1024
DEPTH = 4
N_MIXERS = 4
HEAD_DIM = 64
ROPE_THETA = 10000.0
NORM_EPS = 1e-6
PLE_DIM = 256
NEG = -1e30

DIFF_HEADS = D_MODEL // (2 * HEAD_DIM)

NSA_HEADS = D_MODEL // HEAD_DIM
NSA_GROUPS = 4
NSA_REP = NSA_HEADS // NSA_GROUPS
NSA_CMP_LEN = 32
NSA_CMP_STRIDE = 16
NSA_CMP_HIDDEN = 128
NSA_SLC_BLOCK = 64
NSA_TOPN = 16
NSA_N_LOCAL = 2
NSA_WINDOW = 512
NSA_FORCE_BONUS = 1e4

MOBA_HEADS = D_MODEL // HEAD_DIM
MOBA_BLOCK = 256
MOBA_TOPK = 3

DIL_PATTERNS = ((128, 1), (512, 4), (2048, 16))
DIL_HEADS = 8

D_FF = 2816
CONV_WIDTH = 3

LANES = 128
MXU_COLS = 256
TQ = 256
LOG2E = 1.4426950408889634
Q_SCALE = (HEAD_DIM ** -0.5) * LOG2E
VMEM_LIMIT = 56 * 1024 * 1024


def _cparams(sem):
    return pltpu.CompilerParams(dimension_semantics=sem, vmem_limit_bytes=VMEM_LIMIT)


def _rms(x, g):
    return x * lax.rsqrt(jnp.mean(x * x, axis=-1, keepdims=True) + NORM_EPS) * g


def _dot(a, b):
    return jnp.dot(a, b, preferred_element_type=F32)


def _dot_nt(a, b, precision=None):
    return lax.dot_general(a, b, (((1,), (1,)), ((), ())), preferred_element_type=F32,
                           precision=precision)


def _lane_iota(shape):
    return lax.broadcasted_iota(jnp.int32, shape, 1)


def _row_iota(shape):
    return lax.broadcasted_iota(jnp.int32, shape, 0)


def _rope_kernel(pos_ref, inv_ref, cos_ref, sin_ref):
    ang = pos_ref[...].astype(F32) * inv_ref[...]
    lane = _lane_iota(ang.shape)
    cos_ref[...] = jnp.cos(ang)
    sin_ref[...] = jnp.where((lane & 63) < 32, -jnp.sin(ang), jnp.sin(ang))


def _rope_tables(positions):
    t = positions.size
    inv = ROPE_THETA ** (-jnp.arange(0, HEAD_DIM, 2, dtype=F32) / HEAD_DIM)
    inv = jnp.tile(inv, 4)[None, :]
    tm = 2048
    return pl.pallas_call(
        _rope_kernel,
        grid=(t // tm,),
        in_specs=[pl.BlockSpec((tm, 1), lambda i: (i, 0)),
                  pl.BlockSpec((1, LANES), lambda i: (0, 0))],
        out_specs=[pl.BlockSpec((tm, LANES), lambda i: (i, 0))] * 2,
        out_shape=[jax.ShapeDtypeStruct((t, LANES), F32)] * 2,
        compiler_params=_cparams(("parallel",)),
        name="rope_tables",
    )(positions.reshape(t, 1), inv)


def _proj_kernel(x_ref, g_ref, w_ref, cos_ref, sin_ref, *o_refs, blocks):
    hn = _rms(x_ref[...], g_ref[...]).astype(BF16)
    cos = cos_ref[...]
    sin = sin_ref[...]
    first = (_lane_iota(cos.shape) & 63) < 32
    n_chunks = len(blocks) // 2
    for c in range(n_chunks):
        acc = _dot(hn, w_ref[:, c * MXU_COLS:(c + 1) * MXU_COLS])
        for s in range(2):
            out_idx, out_blk, rope, scale = blocks[2 * c + s]
            if out_idx is None:
                continue
            a = acc[:, s * LANES:(s + 1) * LANES]
            if rope:
                rot = jnp.where(first, pltpu.roll(a, 96, 1), pltpu.roll(a, 32, 1))
                a = a * cos + rot * sin
            if scale != 1.0:
                a = a * scale
            o_ref = o_refs[out_idx]
            o_ref[:, out_blk * LANES:(out_blk + 1) * LANES] = a.astype(o_ref.dtype)


def _proj(x, g, w, cos, sin, blocks, outs, name):
    t = x.shape[0]
    n = w.shape[1]
    assert n == len(blocks) * LANES and len(blocks) % 2 == 0
    tm = 512
    return pl.pallas_call(
        functools.partial(_proj_kernel, blocks=tuple(blocks)),
        grid=(t // tm,),
        in_specs=[pl.BlockSpec((tm, D_MODEL), lambda i: (i, 0)),
                  pl.BlockSpec((1, D_MODEL), lambda i: (0, 0)),
                  pl.BlockSpec((D_MODEL, n), lambda i: (0, 0)),
                  pl.BlockSpec((tm, LANES), lambda i: (i, 0)),
                  pl.BlockSpec((tm, LANES), lambda i: (i, 0))],
        out_specs=[pl.BlockSpec((tm, nb * LANES), lambda i: (i, 0)) for nb, _ in outs],
        out_shape=[jax.ShapeDtypeStruct((t, nb * LANES), dt) for nb, dt in outs],
        compiler_params=_cparams(("parallel",)),
        name=name,
    )(x, g.reshape(1, D_MODEL), w, cos, sin)


def _out_kernel(o_ref, w_ref, x_ref, g_ref, y_ref):
    y = _dot(o_ref[...], w_ref[...])
    y_ref[...] = x_ref[...] + _rms(y, g_ref[...])


def _out_proj(o, w, x, g, name):
    t, k = o.shape
    tm = 512
    return pl.pallas_call(
        _out_kernel,
        grid=(t // tm,),
        in_specs=[pl.BlockSpec((tm, k), lambda i: (i, 0)),
                  pl.BlockSpec((k, D_MODEL), lambda i: (0, 0)),
                  pl.BlockSpec((tm, D_MODEL), lambda i: (i, 0)),
                  pl.BlockSpec((1, D_MODEL), lambda i: (0, 0))],
        out_specs=pl.BlockSpec((tm, D_MODEL), lambda i: (i, 0)),
        out_shape=jax.ShapeDtypeStruct((t, D_MODEL), F32),
        compiler_params=_cparams(("parallel",)),
        name=name,
    )(o, w, x, g.reshape(1, D_MODEL))


FF_CHUNK = 256
FF_TM = 1024
FF_HALO = 16


def _gelu_tanh(x):
    return 0.5 * x * (1.0 + jnp.tanh(math.sqrt(2.0 / math.pi) * (x + 0.044715 * (x * x * x))))


def _ffn_kernel(x_ref, xh_ref, g2_ref, wg_ref, wv_ref, cwg_ref, cwv_ref, cbg_ref, cbv_ref, wd_ref,
                g3_ref, pg_ref, pgw_ref, ppw_ref, p_ref, y_ref, hn_s, acc_s, ug_s, uv_s, *, seq):
    i = pl.program_id(0)
    c = pl.program_id(1)
    tm = x_ref.shape[0]

    @pl.when(c == 0)
    def _():
        hn_s[FF_HALO:, :] = _rms(x_ref[...], g2_ref[...]).astype(BF16)
        halo = _rms(xh_ref[...], g2_ref[...])
        seq_start = (i * tm) % seq == 0
        hn_s[:FF_HALO, :] = jnp.where(seq_start, 0.0, halo).astype(BF16)
        acc_s[...] = jnp.zeros_like(acc_s)

    hn = hn_s[...]
    ug_s[...] = _dot(hn, wg_ref[...])
    uv_s[...] = _dot(hn, wv_ref[...])

    def conv(u_s, cw_ref, cb_ref):
        out = cb_ref[...] + u_s[pl.ds(FF_HALO - 2, tm), :] * cw_ref[0:1, :]
        out = out + u_s[pl.ds(FF_HALO - 1, tm), :] * cw_ref[1:2, :]
        return out + u_s[pl.ds(FF_HALO, tm), :] * cw_ref[2:3, :]

    act = _gelu_tanh(conv(ug_s, cwg_ref, cbg_ref)) * conv(uv_s, cwv_ref, cbv_ref)
    acc_s[...] += _dot(act.astype(BF16), wd_ref[...])

    @pl.when(c == pl.num_programs(1) - 1)
    def _():
        x2 = x_ref[...] + _rms(acc_s[...], g3_ref[...])
        gate = jax.nn.sigmoid(_dot(_rms(x2, pg_ref[...]).astype(BF16), pgw_ref[...]))
        y_ref[...] = x2 + gate * _dot(p_ref[...].astype(BF16), ppw_ref[...])


def _ffn(x, g2, w_up, conv_w, conv_b, w_down, g3, ple_g, ple_gate_w, ple_proj_w, p, seq, name):
    t = x.shape[0]
    tm = FF_TM
    n_c = D_FF // FF_CHUNK
    hb = tm // FF_HALO
    row = lambda v: v.reshape(1, -1)
    const = lambda i, c: (0, 0)
    return pl.pallas_call(
        functools.partial(_ffn_kernel, seq=seq),
        grid=(t // tm, n_c),
        in_specs=[pl.BlockSpec((tm, D_MODEL), lambda i, c: (i, 0)),
                  pl.BlockSpec((FF_HALO, D_MODEL), lambda i, c: (jnp.maximum(i * hb - 1, 0), 0)),
                  pl.BlockSpec((1, D_MODEL), const),
                  pl.BlockSpec((D_MODEL, FF_CHUNK), lambda i, c: (0, c)),
                  pl.BlockSpec((D_MODEL, FF_CHUNK), lambda i, c: (0, n_c + c)),
                  pl.BlockSpec((CONV_WIDTH, FF_CHUNK), lambda i, c: (0, c)),
                  pl.BlockSpec((CONV_WIDTH, FF_CHUNK), lambda i, c: (0, n_c + c)),
                  pl.BlockSpec((1, FF_CHUNK), lambda i, c: (0, c)),
                  pl.BlockSpec((1, FF_CHUNK), lambda i, c: (0, n_c + c)),
                  pl.BlockSpec((FF_CHUNK, D_MODEL), lambda i, c: (c, 0)),
                  pl.BlockSpec((1, D_MODEL), const),
                  pl.BlockSpec((1, D_MODEL), const),
                  pl.BlockSpec((D_MODEL, D_MODEL), const),
                  pl.BlockSpec((PLE_DIM, D_MODEL), const),
                  pl.BlockSpec((tm, PLE_DIM), lambda i, c: (i, 0))],
        out_specs=pl.BlockSpec((tm, D_MODEL), lambda i, c: (i, 0)),
        out_shape=jax.ShapeDtypeStruct((t, D_MODEL), F32),
        scratch_shapes=[pltpu.VMEM((tm + FF_HALO, D_MODEL), BF16),
                        pltpu.VMEM((tm, D_MODEL), F32),
                        pltpu.VMEM((tm + FF_HALO, FF_CHUNK), F32),
                        pltpu.VMEM((tm + FF_HALO, FF_CHUNK), F32)],
        compiler_params=_cparams(("parallel", "arbitrary")),
        name=name,
    )(x, x, row(g2), w_up, w_up, conv_w, conv_w, row(conv_b), row(conv_b), w_down,
      row(g3), row(ple_g), ple_gate_w, ple_proj_w, p)


def _lo_half(shape):
    return _lane_iota(shape) < HEAD_DIM


def _split_heads(q):
    lo = _lo_half(q.shape)
    qf = q.astype(F32)
    return jnp.where(lo, qf, 0.0).astype(q.dtype), jnp.where(lo, 0.0, qf).astype(q.dtype)


def _tile(ref, j):
    return ref[j * TQ:(j + 1) * TQ, :]


def _lane_chunks(x):
    return [x[:, h * LANES:(h + 1) * LANES] for h in range(x.shape[1] // LANES)]


def _softmax_tiles(entries):
    s_list = []
    mt = None
    for q_op, k, mask in entries:
        s = _dot_nt(q_op, k)
        if mask is not None:
            s = jnp.where(mask, s, NEG)
        s_list.append(s)
        for part in _lane_chunks(s):
            mt = part if mt is None else jnp.maximum(mt, part)
    m = jnp.max(mt, axis=1, keepdims=True)
    e_list = []
    lt = None
    for s in s_list:
        e = jnp.exp2(s - m)
        e_list.append(e)
        for part in _lane_chunks(e):
            lt = part if lt is None else lt + part
    return e_list, jnp.sum(lt, axis=1, keepdims=True)


def _attend(entries):
    e_list, l = _softmax_tiles([(q, k, mask) for q, k, _, mask in entries])
    acc = None
    for e, (_, _, v, _) in zip(e_list, entries):
        d = _dot(e.astype(BF16), v)
        acc = d if acc is None else acc + d
    return acc / l


def _causal_masks():
    diff = _row_iota((TQ, TQ)) - _lane_iota((TQ, TQ))
    return diff, diff >= 0


def _rank_rows(score, n_valid):
    ridx = _row_iota(score.shape)
    cnt = jnp.zeros(score.shape, F32)
    for j in range(n_valid):
        other = score[j:j + 1, :]
        cnt = cnt + jnp.where(other > score, 1.0,
                              jnp.where(other == score, jnp.where(ridx > j, 1.0, 0.0), 0.0))
    return cnt


def _bias_lanes(bias_t, first_lane):
    r, tq = bias_t.shape
    parts = []
    if first_lane:
        parts.append(jnp.zeros((first_lane, tq), F32))
    parts.append(bias_t)
    if LANES - first_lane - r:
        parts.append(jnp.zeros((LANES - first_lane - r, tq), F32))
    return jnp.concatenate(parts, axis=0).T


def _diff_kernel(q_ref, k_ref, v_ref, lam_ref, sg_ref, o_ref, *, lambda_init):
    nq = q_ref.shape[0] // TQ
    _, causal = _causal_masks()
    lamv = lam_ref[...]
    lam = (jnp.exp(jnp.sum(lamv[0:1] * lamv[1:2], axis=1, keepdims=True))
           - jnp.exp(jnp.sum(lamv[2:3] * lamv[3:4], axis=1, keepdims=True)) + lambda_init)
    sg = sg_ref[...]
    for i in range(nq):
        q0, q1 = _split_heads(_tile(q_ref, i))
        keys = [(_tile(k_ref, j), causal if j == i else None) for j in range(i + 1)]
        e0, l0 = _softmax_tiles([(q0, k, mk) for k, mk in keys])
        e1, l1 = _softmax_tiles([(q1, k, mk) for k, mk in keys])
        r0 = 1.0 / l0
        r1 = lam / l1
        acc = None
        for j in range(i + 1):
            d = _dot((e0[j] * r0 - e1[j] * r1).astype(BF16), _tile(v_ref, j))
            acc = d if acc is None else acc + d
        o_ref[i * TQ:(i + 1) * TQ, :] = (_rms(acc, sg) * (1.0 - lambda_init)).astype(o_ref.dtype)


def _diff_attention(qkv, lam_rows, subln_g, batch, seq, lambda_init):
    t = batch * seq
    h = DIFF_HEADS
    return pl.pallas_call(
        functools.partial(_diff_kernel, lambda_init=lambda_init),
        grid=(batch, h),
        in_specs=[pl.BlockSpec((seq, LANES), lambda b, hh: (b, hh)),
                  pl.BlockSpec((seq, LANES), lambda b, hh: (b, h + hh)),
                  pl.BlockSpec((seq, LANES), lambda b, hh: (b, 2 * h + hh)),
                  pl.BlockSpec((8, LANES), lambda b, hh: (0, 0)),
                  pl.BlockSpec((1, LANES), lambda b, hh: (0, 0))],
        out_specs=pl.BlockSpec((seq, LANES), lambda b, hh: (b, hh)),
        out_shape=jax.ShapeDtypeStruct((t, h * LANES), BF16),
        compiler_params=_cparams(("parallel", "parallel")),
        name="diff_attention",
    )(qkv, qkv, qkv, lam_rows, subln_g.reshape(1, LANES))


def _moba_kernel(q_ref, k_ref, v_ref, o_ref, ka_s):
    seq = k_ref.shape[0]
    nb = seq // MOBA_BLOCK
    lane = _lane_iota((TQ, LANES))
    lo = lane < HEAD_DIM
    _, causal = _causal_masks()
    km = jnp.concatenate([jnp.mean(_tile(k_ref, j).astype(F32), axis=0, keepdims=True) for j in range(nb)],
                         axis=0)
    for a in range(2):
        spare = HEAD_DIM * (1 - a)
        for j in range(nb):
            kf = _tile(k_ref, j).astype(F32)
            ind = jnp.where(lane == spare + j, 1.0, 0.0)
            ka_s[a, j * TQ:(j + 1) * TQ, :] = (jnp.where(lo, kf, ind) if a == 0
                                               else jnp.where(lo, ind, kf)).astype(BF16)
    q_aug = {}
    for i in range(nb):
        heads = _split_heads(_tile(q_ref, i))
        for a in range(2):
            q_aug[i, a] = heads[a]
            if i > 0:
                gate_t = _dot_nt(km, heads[a].astype(F32), precision=lax.Precision.HIGHEST)
                rank = _rank_rows(gate_t, i)
                drop = jnp.logical_and(_row_iota(gate_t.shape) < i, rank >= MOBA_TOPK)
                bias = _bias_lanes(jnp.where(drop, NEG, 0.0), HEAD_DIM * (1 - a))
                q_aug[i, a] = (heads[a].astype(F32) + bias).astype(BF16)
    for i in range(nb):
        outs = []
        for a in range(2):
            entries = [(q_aug[i, a], ka_s[a, j * TQ:(j + 1) * TQ, :], _tile(v_ref, j), causal if j == i else None)
                       for j in range(i + 1)]
            outs.append(_attend(entries))
        o_ref[i * TQ:(i + 1) * TQ, :] = jnp.where(lo, outs[0], outs[1]).astype(o_ref.dtype)


def _moba_attention(qkv, batch, seq):
    assert MOBA_BLOCK == TQ and seq % MOBA_BLOCK == 0 and seq // MOBA_BLOCK <= HEAD_DIM
    t = batch * seq
    hp = MOBA_HEADS // 2
    return pl.pallas_call(
        _moba_kernel,
        grid=(batch, hp),
        in_specs=[pl.BlockSpec((seq, LANES), lambda b, hh: (b, hh)),
                  pl.BlockSpec((seq, LANES), lambda b, hh: (b, hp + hh)),
                  pl.BlockSpec((seq, LANES), lambda b, hh: (b, 2 * hp + hh))],
        out_specs=pl.BlockSpec((seq, LANES), lambda b, hh: (b, hh)),
        out_shape=jax.ShapeDtypeStruct((t, hp * LANES), BF16),
        scratch_shapes=[pltpu.VMEM((2, seq, LANES), BF16)],
        compiler_params=_cparams(("parallel", "parallel")),
        name="moba_attention",
    )(qkv, qkv, qkv)


def _dil_kernel(q0_ref, q1_ref, q2_ref, k0_ref, k1_ref, k2_ref, v0_ref, v1_ref, v2_ref, o_ref):
    seq = k0_ref.shape[0]
    nq = seq // TQ
    (w0, _), (w1, d1), (w2, d2) = DIL_PATTERNS
    assert w0 < TQ and w1 == 2 * TQ and w2 >= seq
    diff, causal = _causal_masks()
    on1 = (diff & (d1 - 1)) == 0
    on2 = (diff & (d2 - 1)) == 0
    m0_diag = jnp.logical_and(causal, diff <= w0)
    m0_prev = diff + TQ <= w0
    m1_diag = jnp.logical_and(causal, on1)
    m1_prev2 = jnp.logical_and(on1, diff <= 0)
    m2_diag = jnp.logical_and(causal, on2)
    lo = _lo_half((TQ, LANES))
    for i in range(nq):
        hs = [_split_heads(_tile(r, i)) for r in (q0_ref, q1_ref, q2_ref)]
        outs = []
        for a in range(2):
            ent = [(hs[0][a], _tile(k0_ref, i), _tile(v0_ref, i), m0_diag),
                   (hs[1][a], _tile(k1_ref, i), _tile(v1_ref, i), m1_diag),
                   (hs[2][a], _tile(k2_ref, i), _tile(v2_ref, i), m2_diag)]
            if i >= 1:
                ent.append((hs[0][a], _tile(k0_ref, i - 1), _tile(v0_ref, i - 1), m0_prev))
                ent.append((hs[1][a], _tile(k1_ref, i - 1), _tile(v1_ref, i - 1), on1))
            if i >= 2:
                ent.append((hs[1][a], _tile(k1_ref, i - 2), _tile(v1_ref, i - 2), m1_prev2))
            ent += [(hs[2][a], _tile(k2_ref, j), _tile(v2_ref, j), on2) for j in range(i)]
            outs.append(_attend(ent))
        o_ref[i * TQ:(i + 1) * TQ, :] = jnp.where(lo, outs[0], outs[1]).astype(o_ref.dtype)


def _dil_attention(qkv, batch, seq):
    t = batch * seq
    hp = DIL_HEADS // 2
    per_g = 3 * hp

    def spec(g, which):
        return pl.BlockSpec((seq, LANES), lambda b, hh: (b, g * per_g + which * hp + hh))

    return pl.pallas_call(
        _dil_kernel,
        grid=(batch, hp),
        in_specs=[spec(0, 0), spec(1, 0), spec(2, 0), spec(0, 1), spec(1, 1), spec(2, 1),
                  spec(0, 2), spec(1, 2), spec(2, 2)],
        out_specs=pl.BlockSpec((seq, LANES), lambda b, hh: (b, hh)),
        out_shape=jax.ShapeDtypeStruct((t, hp * LANES), BF16),
        compiler_params=_cparams(("parallel", "parallel")),
        name="dilated_attention",
    )(*([qkv] * 9))


N_CMP_PAD = 128


def _cmp_kernel(zk_ref, zv_ref, w1k_ref, w1v_ref, pk_ref, pv_ref, w1fk_ref, w1fv_ref, w2k_ref, w2v_ref,
                ko_ref, vo_ref):
    def one(z_ref, w1_ref, pos_ref, w1f_ref, w2_ref, o_ref):
        z = z_ref[...]
        a = _dot(z, w1_ref[0])
        b = _dot(z, w1_ref[1])
        b_next = jnp.concatenate([b[1:], jnp.zeros_like(b[:1])], axis=0)
        const = _dot(pos_ref[...], w1f_ref[...])[0:1]
        hid = a + b_next + jnp.concatenate([const] * NSA_GROUPS, axis=1)
        o_ref[...] = _dot(jax.nn.gelu(hid, approximate=True).astype(BF16), w2_ref[...]).astype(o_ref.dtype)

    one(zk_ref, w1k_ref, pk_ref, w1fk_ref, w2k_ref, ko_ref)
    one(zv_ref, w1v_ref, pv_ref, w1fv_ref, w2v_ref, vo_ref)


def _nsa_compress(kc, vc, ck_pos, ck_w1, ck_w2, cv_pos, cv_w1, cv_w2, batch, seq):
    g, dk, f = NSA_GROUPS, HEAD_DIM, NSA_CMP_HIDDEN
    half = NSA_CMP_LEN // 2
    assert NSA_CMP_STRIDE == half and seq // half == N_CMP_PAD
    zk = kc.reshape(batch * N_CMP_PAD, half * g * dk)
    zv = vc.reshape(batch * N_CMP_PAD, half * g * dk)

    def expand_w1(w1):
        w = w1.reshape(2, half, dk, f)
        eye = jnp.eye(g, dtype=w1.dtype)
        big = jnp.einsum('acdf,gh->acgdhf', w, eye)
        return big.reshape(2, half * g * dk, g * f).astype(BF16)

    def expand_w2(w2):
        dup = jnp.concatenate([w2, w2], axis=1)
        eye = jnp.eye(g, dtype=w2.dtype)
        return jnp.einsum('fd,gh->gfhd', dup, eye).reshape(g * f, g * LANES).astype(BF16)

    def pos_rows(pos):
        flat = pos.reshape(1, NSA_CMP_LEN * dk)
        return jnp.concatenate([flat, jnp.zeros((7, NSA_CMP_LEN * dk), pos.dtype)], axis=0).astype(BF16)

    kd = half * g * dk
    const2 = lambda b: (0, 0)
    const3 = lambda b: (0, 0, 0)
    return pl.pallas_call(
        _cmp_kernel,
        grid=(batch,),
        in_specs=[pl.BlockSpec((N_CMP_PAD, kd), lambda b: (b, 0)),
                  pl.BlockSpec((N_CMP_PAD, kd), lambda b: (b, 0)),
                  pl.BlockSpec((2, kd, g * f), const3),
                  pl.BlockSpec((2, kd, g * f), const3),
                  pl.BlockSpec((8, NSA_CMP_LEN * dk), const2),
                  pl.BlockSpec((8, NSA_CMP_LEN * dk), const2),
                  pl.BlockSpec((NSA_CMP_LEN * dk, f), const2),
                  pl.BlockSpec((NSA_CMP_LEN * dk, f), const2),
                  pl.BlockSpec((g * f, g * LANES), const2),
                  pl.BlockSpec((g * f, g * LANES), const2)],
        out_specs=[pl.BlockSpec((N_CMP_PAD, g * LANES), lambda b: (b, 0))] * 2,
        out_shape=[jax.ShapeDtypeStruct((batch * N_CMP_PAD, g * LANES), BF16)] * 2,
        compiler_params=_cparams(("parallel",)),
        name="nsa_compress",
    )(zk, zv, expand_w1(ck_w1), expand_w1(cv_w1), pos_rows(ck_pos), pos_rows(cv_pos),
      ck_w1.reshape(NSA_CMP_LEN * dk, f).astype(BF16), cv_w1.reshape(NSA_CMP_LEN * dk, f).astype(BF16),
      expand_w2(ck_w2), expand_w2(cv_w2))


SLC_SHIFT = NSA_SLC_BLOCK.bit_length() - 1


def _nsa_heads(q_pair):
    lo = _lo_half(q_pair.shape)
    qf = q_pair.astype(F32)
    return jnp.where(lo, qf, 0.0), jnp.where(lo, pltpu.roll(qf, HEAD_DIM, 1), 0.0)


def _lane_column(x, c):
    return jnp.sum(jnp.where(_lane_iota(x.shape) == c, x, 0.0), axis=1, keepdims=True)


def _nsa_select_kernel(q_ref, kc_ref, vc_ref, gate_ref, ov_ref, oc_ref, bias_ref):
    seq = q_ref.shape[0]
    nq = seq // TQ
    n_slc = seq // NSA_SLC_BLOCK
    assert 1 << SLC_SHIFT == NSA_SLC_BLOCK and HEAD_DIM + n_slc <= LANES
    lane = _lane_iota((TQ, LANES))
    row = _row_iota((TQ, LANES))
    lo = lane < HEAD_DIM
    kc = kc_ref[...]
    vc = vc_ref[...]
    jb = _row_iota((n_slc, TQ))
    for i in range(nq):
        heads_bf = []
        for pr in range(NSA_REP // 2):
            heads_bf.extend(h.astype(BF16) for h in
                            _nsa_heads(q_ref[i * TQ:(i + 1) * TQ, pr * LANES:(pr + 1) * LANES]))

        cmask = jnp.logical_and(lane * NSA_CMP_STRIDE + (NSA_CMP_LEN - 1) <= i * TQ + row, lane < N_CMP_PAD - 1)
        p_sum = jnp.zeros((TQ, N_CMP_PAD), F32)
        o_cmp = []
        for r in range(NSA_REP):
            s = jnp.where(cmask, _dot_nt(heads_bf[r], kc), NEG)
            e = jnp.where(cmask, jnp.exp2(s - jnp.max(s, axis=1, keepdims=True)), 0.0)
            p = e / jnp.maximum(jnp.sum(e, axis=1, keepdims=True), 1e-30)
            p_sum = p_sum + p
            o_cmp.append(_dot(p.astype(BF16), vc))

        imp_t = _dot_nt(ov_ref[...], p_sum, precision=lax.Precision.HIGHEST)
        back = ((i * TQ + _lane_iota((n_slc, TQ))) >> SLC_SHIFT) - jb
        ok = back >= 0
        forced = jnp.logical_or(jb == 0, jnp.logical_and(ok, back < NSA_N_LOCAL))
        score = jnp.where(ok, imp_t + jnp.where(forced, NSA_FORCE_BONUS, 0.0), NEG)
        rank = _rank_rows(score, n_slc)
        keep = jnp.logical_and(ok, rank < min(NSA_TOPN, n_slc))
        bias_ref[i * TQ:(i + 1) * TQ, :] = _bias_lanes(jnp.where(keep, 0.0, NEG), HEAD_DIM).astype(bias_ref.dtype)

        gates = jax.nn.sigmoid(gate_ref[i * TQ:(i + 1) * TQ, :])
        for pr in range(NSA_REP // 2):
            r0, r1 = 2 * pr, 2 * pr + 1
            oc_ref[i * TQ:(i + 1) * TQ, pr * LANES:(pr + 1) * LANES] = jnp.where(
                lo, o_cmp[r0] * _lane_column(gates, 3 * r0), o_cmp[r1] * _lane_column(gates, 3 * r1)
            ).astype(oc_ref.dtype)


def _nsa_attn_kernel(q_ref, ks_ref, vs_ref, kw_ref, vw_ref, bias_ref, gate_ref, oc_ref, o_ref, ksa_s):
    seq = ks_ref.shape[0]
    nq = seq // TQ
    per_tile = TQ // NSA_SLC_BLOCK
    assert NSA_WINDOW == 2 * TQ
    pr = pl.program_id(2)
    lane = _lane_iota((TQ, LANES))
    row = _row_iota((TQ, LANES))
    lo = lane < HEAD_DIM
    diff, causal = _causal_masks()
    for j in range(nq):
        ind = jnp.where(lane - HEAD_DIM == j * per_tile + (row >> SLC_SHIFT), 1.0, 0.0)
        ksa_s[j * TQ:(j + 1) * TQ, :] = jnp.where(lo, _tile(ks_ref, j).astype(F32), ind).astype(BF16)
    for i in range(nq):
        heads = _nsa_heads(_tile(q_ref, i))
        bias = _tile(bias_ref, i).astype(F32)
        gates = jax.nn.sigmoid(_tile(gate_ref, i))
        outs = []
        for h in range(2):
            q_aug = (heads[h] + bias).astype(BF16)
            q_bf = heads[h].astype(BF16)
            o_slc = _attend([(q_aug, _tile(ksa_s, j), _tile(vs_ref, j), causal if j == i else None)
                             for j in range(i + 1)])
            win = [(q_bf, _tile(kw_ref, i), _tile(vw_ref, i), causal)]
            if i >= 1:
                win.append((q_bf, _tile(kw_ref, i - 1), _tile(vw_ref, i - 1), None))
            if i >= 2:
                win.append((q_bf, _tile(kw_ref, i - 2), _tile(vw_ref, i - 2), diff < 0))
            o_win = _attend(win)
            c0 = 3 * (2 * pr + h)
            outs.append(o_slc * _lane_column(gates, c0 + 1) + o_win * _lane_column(gates, c0 + 2))
        o_ref[i * TQ:(i + 1) * TQ, :] = (_tile(oc_ref, i).astype(F32)
                                         + jnp.where(lo, outs[0], outs[1])).astype(o_ref.dtype)


def _nsa_overlap_t(seq):
    n_slc = seq // NSA_SLC_BLOCK
    n = jnp.arange(N_CMP_PAD)
    first = n * NSA_CMP_STRIDE
    last = first + NSA_CMP_LEN - 1
    s_start = jnp.arange(n_slc) * NSA_SLC_BLOCK
    ov = (first[None, :] <= (s_start + NSA_SLC_BLOCK - 1)[:, None]) & (last[None, :] >= s_start[:, None])
    ov = ov & (n[None, :] < (seq - NSA_CMP_LEN) // NSA_CMP_STRIDE + 1)
    return ov.astype(F32)


def _nsa_attention(q, ks, vs, kw, vw, kcmp, vcmp, gates, batch, seq):
    t = batch * seq
    g = NSA_GROUPS
    qb = NSA_REP // 2
    n_slc = seq // NSA_SLC_BLOCK
    grp = lambda: pl.BlockSpec((seq, LANES), lambda b, gg: (b, gg))
    cm = lambda: pl.BlockSpec((N_CMP_PAD, LANES), lambda b, gg: (b, gg))
    o_cmp, bias = pl.pallas_call(
        _nsa_select_kernel,
        grid=(batch, g),
        in_specs=[pl.BlockSpec((seq, qb * LANES), lambda b, gg: (b, gg)), cm(), cm(), grp(),
                  pl.BlockSpec((n_slc, N_CMP_PAD), lambda b, gg: (0, 0))],
        out_specs=[pl.BlockSpec((seq, qb * LANES), lambda b, gg: (b, gg)), grp()],
        out_shape=[jax.ShapeDtypeStruct((t, NSA_HEADS * HEAD_DIM), BF16),
                   jax.ShapeDtypeStruct((t, g * LANES), BF16)],
        compiler_params=_cparams(("parallel", "parallel")),
        name="nsa_select",
    )(q, kcmp, vcmp, gates, _nsa_overlap_t(seq))
    kv = lambda: pl.BlockSpec((seq, LANES), lambda b, gg, pr: (b, gg))
    pair = lambda: pl.BlockSpec((seq, LANES), lambda b, gg, pr: (b, gg * qb + pr))
    return pl.pallas_call(
        _nsa_attn_kernel,
        grid=(batch, g, qb),
        in_specs=[pair(), kv(), kv(), kv(), kv(), kv(), kv(), pair()],
        out_specs=pair(),
        out_shape=jax.ShapeDtypeStruct((t, NSA_HEADS * HEAD_DIM), BF16),
        scratch_shapes=[pltpu.VMEM((seq, LANES), BF16)],
        compiler_params=_cparams(("parallel", "parallel", "arbitrary")),
        name="nsa_attention",
    )(q, ks, vs, kw, vw, bias, gates, o_cmp)


def _plain_blocks(n_total, n_rope, n_q):
    return [(0, b, b < n_rope, Q_SCALE if b < n_q else 1.0) for b in range(n_total)]


def _mixer_diff(x, g, w_in, lam_q1, lam_k1, lam_q2, lam_k2, subln_g, cos, sin, batch, seq, lambda_init):
    h = DIFF_HEADS
    blocks = _plain_blocks(3 * h, 2 * h, h)
    (qkv,) = _proj(x, g, w_in.astype(BF16), cos, sin, blocks, [(3 * h, BF16)], "diff_in_proj")
    pad = lambda v: jnp.pad(v.astype(F32), (0, LANES - HEAD_DIM))
    lam_rows = jnp.stack([pad(lam_q1), pad(lam_k1), pad(lam_q2), pad(lam_k2)]
                         + [jnp.zeros((LANES,), F32)] * 4)
    return _diff_attention(qkv, lam_rows, subln_g, batch, seq, lambda_init)


def _mixer_moba(x, g, w_in, cos, sin, batch, seq):
    hp = MOBA_HEADS // 2
    blocks = _plain_blocks(3 * hp, 2 * hp, hp)
    (qkv,) = _proj(x, g, w_in.astype(BF16), cos, sin, blocks, [(3 * hp, BF16)], "moba_in_proj")
    return _moba_attention(qkv, batch, seq)


def _mixer_dil(x, g, w_in, cos, sin, batch, seq):
    hp = DIL_HEADS // 2
    blocks = []
    for gi in range(len(DIL_PATTERNS)):
        for which in range(3):
            for b in range(hp):
                blocks.append((0, len(blocks), which < 2, Q_SCALE if which == 0 else 1.0))
    (qkv,) = _proj(x, g, w_in.astype(BF16), cos, sin, blocks, [(len(blocks), BF16)], "dil_in_proj")
    return _dil_attention(qkv, batch, seq)


def _mixer_nsa(x, g, w_in, ck_pos, ck_w1, ck_w2, cv_pos, cv_w1, cv_w2, cos, sin, batch, seq):
    hq, gr, dk = NSA_HEADS, NSA_GROUPS, HEAD_DIM
    kvw = gr * dk
    o_q = 0
    o_kc, o_vc, o_ks, o_vs, o_kw, o_vw, o_g = (hq * dk + n * kvw for n in range(7))
    n_gate = 3 * hq

    def dup(off):
        cols = w_in[:, off:off + kvw].reshape(D_MODEL, gr, dk)
        return jnp.concatenate([cols, cols], axis=2).reshape(D_MODEL, gr * LANES)

    per_group = n_gate // gr
    gate_cols = jnp.pad(w_in[:, o_g:o_g + n_gate].reshape(D_MODEL, gr, per_group),
                        ((0, 0), (0, 0), (0, LANES - per_group))).reshape(D_MODEL, gr * LANES)
    w = jnp.concatenate([w_in[:, o_q:o_q + hq * dk], dup(o_ks), dup(o_kw), w_in[:, o_kc:o_kc + kvw],
                         dup(o_vs), dup(o_vw), w_in[:, o_vc:o_vc + kvw], gate_cols], axis=1).astype(BF16)
    nqb = hq * dk // LANES
    blocks = [(0, b, True, Q_SCALE) for b in range(nqb)]
    blocks += [(1, b, True, 1.0) for b in range(gr)]
    blocks += [(2, b, True, 1.0) for b in range(gr)]
    blocks += [(3, b, True, 1.0) for b in range(kvw // LANES)]
    blocks += [(4, b, False, 1.0) for b in range(gr)]
    blocks += [(5, b, False, 1.0) for b in range(gr)]
    blocks += [(6, b, False, 1.0) for b in range(kvw // LANES)]
    blocks += [(7, b, False, 1.0) for b in range(gr)]
    outs = [(nqb, BF16), (gr, BF16), (gr, BF16), (kvw // LANES, BF16), (gr, BF16), (gr, BF16),
            (kvw // LANES, BF16), (gr, F32)]
    q, ks, kw, kc, vs, vw, vc, gates = _proj(x, g, w, cos, sin, blocks, outs, "nsa_in_proj")
    kcmp, vcmp = _nsa_compress(kc, vc, ck_pos, ck_w1, ck_w2, cv_pos, cv_w1, cv_w2, batch, seq)
    return _nsa_attention(q, ks, vs, kw, vw, kcmp, vcmp, gates, batch, seq)


def kernel(x, p, positions, norm_gains, ple_norm_g, ple_gate_w, ple_proj_w, ffn_w_up, ffn_conv_w, ffn_conv_b, ffn_w_down, diff_w_in, diff_w_out, diff_lam_q1, diff_lam_k1, diff_lam_q2, diff_lam_k2, diff_subln_g, nsa_w_in, nsa_w_out, nsa_ck_pos, nsa_ck_w1, nsa_ck_w2, nsa_cv_pos, nsa_cv_w1, nsa_cv_w2, moba_w_in, moba_w_out, dil_w_in, dil_w_out):
    batch, seq, d = x.shape
    t = batch * seq
    cos, sin = _rope_tables(positions)
    xt = x.reshape(t, d)
    for i in range(DEPTH):
        kind, j = i % N_MIXERS, i // N_MIXERS
        g0 = norm_gains[i, 0]
        if kind == 0:
            lambda_init = 0.8 - 0.6 * math.exp(-0.3 * i)
            o = _mixer_diff(xt, g0, diff_w_in[j], diff_lam_q1[j], diff_lam_k1[j], diff_lam_q2[j],
                            diff_lam_k2[j], diff_subln_g[j], cos, sin, batch, seq, lambda_init)
            w_out = diff_w_out[j]
        elif kind == 1:
            o = _mixer_nsa(xt, g0, nsa_w_in[j], nsa_ck_pos[j], nsa_ck_w1[j], nsa_ck_w2[j],
                           nsa_cv_pos[j], nsa_cv_w1[j], nsa_cv_w2[j], cos, sin, batch, seq)
            w_out = nsa_w_out[j]
        elif kind == 2:
            o = _mixer_moba(xt, g0, moba_w_in[j], cos, sin, batch, seq)
            w_out = moba_w_out[j]
        else:
            o = _mixer_dil(xt, g0, dil_w_in[j], cos, sin, batch, seq)
            w_out = dil_w_out[j]
        xt = _out_proj(o, w_out.astype(BF16), xt, norm_gains[i, 1], "out_proj_%d" % i)
        xt = _ffn(xt, norm_gains[i, 2], ffn_w_up[i].astype(BF16), ffn_conv_w[i], ffn_conv_b[i],
                  ffn_w_down[i].astype(BF16), norm_gains[i, 3], ple_norm_g[i],
                  ple_gate_w[i].astype(BF16), ple_proj_w[i].astype(BF16), p[i].reshape(t, PLE_DIM),
                  seq, "ffn_%d" % i)
    return xt.reshape(batch, seq, d)
```

```python
import functools
import math

import jax
import jax.numpy as jnp
from jax import lax
from jax.experimental import pallas as pl
from jax.experimental.pallas import tpu as pltpu

F32 = jnp.float32
BF16 = jnp.bfloat16

D_MODEL = 1024
DEPTH = 4
N_MIXERS = 4
HEAD_DIM = 64
ROPE_THETA = 10000.0
NORM_EPS = 1e-6
PLE_DIM = 256
NEG = -1e30

DIFF_HEADS = D_MODEL // (2 * HEAD_DIM)

NSA_HEADS = D_MODEL // HEAD_DIM
NSA_GROUPS = 4
NSA_REP = NSA_HEADS // NSA_GROUPS
NSA_CMP_LEN = 32
NSA_CMP_STRIDE = 16
NSA_CMP_HIDDEN = 128
NSA_SLC_BLOCK = 64
NSA_TOPN = 16
NSA_N_LOCAL = 2
NSA_WINDOW = 512
NSA_FORCE_BONUS = 1e4

MOBA_HEADS = D_MODEL // HEAD_DIM
MOBA_BLOCK = 256
MOBA_TOPK = 3

DIL_PATTERNS = ((128, 1), (512, 4), (2048, 16))
DIL_HEADS = 8

D_FF = 2816
CONV_WIDTH = 3

LANES = 128
MXU_COLS = 256
TQ = 256
LOG2E = 1.4426950408889634
Q_SCALE = (HEAD_DIM ** -0.5) * LOG2E
VMEM_LIMIT = 56 * 1024 * 1024


def _cparams(sem):
    return pltpu.CompilerParams(dimension_semantics=sem, vmem_limit_bytes=VMEM_LIMIT)


def _rms(x, g):
    return x * lax.rsqrt(jnp.mean(x * x, axis=-1, keepdims=True) + NORM_EPS) * g


def _dot(a, b):
    return jnp.dot(a, b, preferred_element_type=F32)


def _dot_nt(a, b, precision=None):
    return lax.dot_general(a, b, (((1,), (1,)), ((), ())), preferred_element_type=F32,
                           precision=precision)


def _lane_iota(shape):
    return lax.broadcasted_iota(jnp.int32, shape, 1)


def _row_iota(shape):
    return lax.broadcasted_iota(jnp.int32, shape, 0)


def _rope_kernel(pos_ref, inv_ref, cos_ref, sin_ref):
    ang = pos_ref[...].astype(F32) * inv_ref[...]
    lane = _lane_iota(ang.shape)
    cos_ref[...] = jnp.cos(ang)
    sin_ref[...] = jnp.where((lane & 63) < 32, -jnp.sin(ang), jnp.sin(ang))


def _rope_tables(positions):
    t = positions.size
    inv = ROPE_THETA ** (-jnp.arange(0, HEAD_DIM, 2, dtype=F32) / HEAD_DIM)
    inv = jnp.tile(inv, 4)[None, :]
    tm = 2048
    return pl.pallas_call(
        _rope_kernel,
        grid=(t // tm,),
        in_specs=[pl.BlockSpec((tm, 1), lambda i: (i, 0)),
                  pl.BlockSpec((1, LANES), lambda i: (0, 0))],
        out_specs=[pl.BlockSpec((tm, LANES), lambda i: (i, 0))] * 2,
        out_shape=[jax.ShapeDtypeStruct((t, LANES), F32)] * 2,
        compiler_params=_cparams(("parallel",)),
        name="rope_tables",
    )(positions.reshape(t, 1), inv)


def _proj_kernel(x_ref, g_ref, w_ref, cos_ref, sin_ref, *o_refs, blocks):
    hn = _rms(x_ref[...], g_ref[...]).astype(BF16)
    cos = cos_ref[...]
    sin = sin_ref[...]
    first = (_lane_iota(cos.shape) & 63) < 32
    n_chunks = len(blocks) // 2
    for c in range(n_chunks):
        acc = _dot(hn, w_ref[:, c * MXU_COLS:(c + 1) * MXU_COLS])
        for s in range(2):
            out_idx, out_blk, rope, scale = blocks[2 * c + s]
            if out_idx is None:
                continue
            a = acc[:, s * LANES:(s + 1) * LANES]
            if rope:
                rot = jnp.where(first, pltpu.roll(a, 96, 1), pltpu.roll(a, 32, 1))
                a = a * cos + rot * sin
            if scale != 1.0:
                a = a * scale
            o_ref = o_refs[out_idx]
            o_ref[:, out_blk * LANES:(out_blk + 1) * LANES] = a.astype(o_ref.dtype)


def _proj(x, g, w, cos, sin, blocks, outs, name):
    t = x.shape[0]
    n = w.shape[1]
    assert n == len(blocks) * LANES and len(blocks) % 2 == 0
    tm = 512
    return pl.pallas_call(
        functools.partial(_proj_kernel, blocks=tuple(blocks)),
        grid=(t // tm,),
        in_specs=[pl.BlockSpec((tm, D_MODEL), lambda i: (i, 0)),
                  pl.BlockSpec((1, D_MODEL), lambda i: (0, 0)),
                  pl.BlockSpec((D_MODEL, n), lambda i: (0, 0)),
                  pl.BlockSpec((tm, LANES), lambda i: (i, 0)),
                  pl.BlockSpec((tm, LANES), lambda i: (i, 0))],
        out_specs=[pl.BlockSpec((tm, nb * LANES), lambda i: (i, 0)) for nb, _ in outs],
        out_shape=[jax.ShapeDtypeStruct((t, nb * LANES), dt) for nb, dt in outs],
        compiler_params=_cparams(("parallel",)),
        name=name,
    )(x, g.reshape(1, D_MODEL), w, cos, sin)


def _out_kernel(o_ref, w_ref, x_ref, g_ref, y_ref):
    y = _dot(o_ref[...], w_ref[...])
    y_ref[...] = x_ref[...] + _rms(y, g_ref[...])


def _out_proj(o, w, x, g, name):
    t, k = o.shape
    tm = 512
    return pl.pallas_call(
        _out_kernel,
        grid=(t // tm,),
        in_specs=[pl.BlockSpec((tm, k), lambda i: (i, 0)),
                  pl.BlockSpec((k, D_MODEL), lambda i: (0, 0)),
                  pl.BlockSpec((tm, D_MODEL), lambda i: (i, 0)),
                  pl.BlockSpec((1, D_MODEL), lambda i: (0, 0))],
        out_specs=pl.BlockSpec((tm, D_MODEL), lambda i: (i, 0)),
        out_shape=jax.ShapeDtypeStruct((t, D_MODEL), F32),
        compiler_params=_cparams(("parallel",)),
        name=name,
    )(o, w, x, g.reshape(1, D_MODEL))


FF_CHUNK = 256
FF_TM = 512
FF_HALO = 16


def _gelu_tanh(x):
    k = math.sqrt(2.0 / math.pi)
    h = 0.5 * x
    return h + h * jnp.tanh(x * (k + (k * 0.044715) * (x * x)))


def _ffn_kernel(x_ref, xh_ref, g2_ref, wu_ref, cw_ref, cb_ref, wd_ref, g3_ref, pg_ref, pgw_ref, ppw_ref,
                p_ref, y_ref, hn_s, *, seq):
    i = pl.program_id(0)
    tm = x_ref.shape[0]
    n_c = D_FF // FF_CHUNK
    x = x_ref[...]
    hn_s[FF_HALO:, :] = _rms(x, g2_ref[...]).astype(BF16)
    halo = _rms(xh_ref[...], g2_ref[...])
    seq_start = (i * tm) % seq == 0
    hn_s[:FF_HALO, :] = jnp.where(seq_start, 0.0, halo).astype(BF16)
    hn = hn_s[...]

    def conv(u, col):
        cols = slice(col, col + FF_CHUNK)
        out = cb_ref[:, cols] + pltpu.roll(u, 2, 0)[FF_HALO:] * cw_ref[0:1, cols]
        out = out + pltpu.roll(u, 1, 0)[FF_HALO:] * cw_ref[1:2, cols]
        return out + u[FF_HALO:] * cw_ref[2:3, cols]

    def up(c):
        gcol, vcol = c * FF_CHUNK, D_FF + c * FF_CHUNK
        return (_dot(hn, wu_ref[:, gcol:gcol + FF_CHUNK]), _dot(hn, wu_ref[:, vcol:vcol + FF_CHUNK]))

    def down(c, act, acc):
        d = _dot(act, wd_ref[c * FF_CHUNK:(c + 1) * FF_CHUNK, :])
        return d if acc is None else acc + d

    acc = None
    pending = None
    u_next = up(0)
    for c in range(n_c):
        ug, uv = u_next
        if c + 1 < n_c:
            u_next = up(c + 1)
        act = (_gelu_tanh(conv(ug, c * FF_CHUNK)) * conv(uv, D_FF + c * FF_CHUNK)).astype(BF16)
        if pending is not None:
            acc = down(c - 1, pending, acc)
        pending = act
    acc = down(n_c - 1, pending, acc)

    x2 = x + _rms(acc, g3_ref[...])
    gate = jax.nn.sigmoid(_dot(_rms(x2, pg_ref[...]).astype(BF16), pgw_ref[...]))
    y_ref[...] = x2 + gate * _dot(p_ref[...].astype(BF16), ppw_ref[...])


def _ffn(x, g2, w_up, conv_w, conv_b, w_down, g3, ple_g, ple_gate_w, ple_proj_w, p, seq, name):
    t = x.shape[0]
    tm = FF_TM
    hb = tm // FF_HALO
    row = lambda v: v.reshape(1, -1)
    const = lambda i: (0, 0)
    resident = lambda shape: pl.BlockSpec(shape, const, pipeline_mode=pl.Buffered(1))
    return pl.pallas_call(
        functools.partial(_ffn_kernel, seq=seq),
        grid=(t // tm,),
        in_specs=[pl.BlockSpec((tm, D_MODEL), lambda i: (i, 0)),
                  pl.BlockSpec((FF_HALO, D_MODEL), lambda i: (jnp.maximum(i * hb - 1, 0), 0)),
                  pl.BlockSpec((1, D_MODEL), const),
                  resident((D_MODEL, 2 * D_FF)),
                  pl.BlockSpec((CONV_WIDTH, 2 * D_FF), const),
                  pl.BlockSpec((1, 2 * D_FF), const),
                  resident((D_FF, D_MODEL)),
                  pl.BlockSpec((1, D_MODEL), const),
                  pl.BlockSpec((1, D_MODEL), const),
                  resident((D_MODEL, D_MODEL)),
                  resident((PLE_DIM, D_MODEL)),
                  pl.BlockSpec((tm, PLE_DIM), lambda i: (i, 0))],
        out_specs=pl.BlockSpec((tm, D_MODEL), lambda i: (i, 0)),
        out_shape=jax.ShapeDtypeStruct((t, D_MODEL), F32),
        scratch_shapes=[pltpu.VMEM((tm + FF_HALO, D_MODEL), BF16)],
        compiler_params=_cparams(("parallel",)),
        name=name,
    )(x, x, row(g2), w_up, conv_w, row(conv_b), w_down, row(g3), row(ple_g), ple_gate_w, ple_proj_w, p)


def _lo_half(shape):
    return _lane_iota(shape) < HEAD_DIM


def _split_heads(q):
    lo = _lo_half(q.shape)
    qf = q.astype(F32)
    return jnp.where(lo, qf, 0.0).astype(q.dtype), jnp.where(lo, 0.0, qf).astype(q.dtype)


def _tile(ref, j):
    return ref[j * TQ:(j + 1) * TQ, :]


def _lane_chunks(x):
    return [x[:, h * LANES:(h + 1) * LANES] for h in range(x.shape[1] // LANES)]


def _scores(entries):
    s_list = []
    for q_op, k, mask in (e[:3] for e in entries):
        s = _dot_nt(q_op, k)
        s_list.append(s if mask is None else jnp.where(mask, s, NEG))
    return s_list


def _softmax(s_list):
    mt = None
    for s in s_list:
        for part in _lane_chunks(s):
            mt = part if mt is None else jnp.maximum(mt, part)
    m = jnp.max(mt, axis=1, keepdims=True)
    e_list = []
    lt = None
    for s in s_list:
        e = jnp.exp2(s - m)
        e_list.append(e)
        for part in _lane_chunks(e):
            lt = part if lt is None else lt + part
    return e_list, jnp.sum(lt, axis=1, keepdims=True)


def _weighted_sum(e_list, values):
    acc = None
    for e, v in zip(e_list, values):
        d = _dot(e.astype(BF16), v)
        acc = d if acc is None else acc + d
    return acc


def _attend(s_list, values):
    e_list, l = _softmax(s_list)
    return _weighted_sum(e_list, values) / l


def _run_pipelined(stages):
    prev = None
    for score_fn, finish_fn in stages:
        state = score_fn()
        if prev is not None:
            prev[1](prev[0])
        prev = (state, finish_fn)
    if prev is not None:
        prev[1](prev[0])


def _causal_masks():
    diff = _row_iota((TQ, TQ)) - _lane_iota((TQ, TQ))
    return diff, diff >= 0


def _rank_rows(score, n_valid):
    ridx = _row_iota(score.shape)
    cnt = jnp.zeros(score.shape, F32)
    for j in range(n_valid):
        other = score[j:j + 1, :]
        cnt = cnt + jnp.where(other > score, 1.0,
                              jnp.where(other == score, jnp.where(ridx > j, 1.0, 0.0), 0.0))
    return cnt


def _bias_lanes(bias_t, first_lane):
    r, tq = bias_t.shape
    parts = []
    if first_lane:
        parts.append(jnp.zeros((first_lane, tq), F32))
    parts.append(bias_t)
    if LANES - first_lane - r:
        parts.append(jnp.zeros((LANES - first_lane - r, tq), F32))
    return jnp.concatenate(parts, axis=0).T


def _diff_kernel(q_ref, k_ref, v_ref, lam_ref, sg_ref, o_ref, *, lambda_init):
    nq = q_ref.shape[0] // TQ
    _, causal = _causal_masks()
    lamv = lam_ref[...]
    lam = (jnp.exp(jnp.sum(lamv[0:1] * lamv[1:2], axis=1, keepdims=True))
           - jnp.exp(jnp.sum(lamv[2:3] * lamv[3:4], axis=1, keepdims=True)) + lambda_init)
    sg = sg_ref[...]

    def score(i):
        q0, q1 = _split_heads(_tile(q_ref, i))
        keys = [(_tile(k_ref, j), causal if j == i else None) for j in range(i + 1)]
        return _scores([(q0, k, mk) for k, mk in keys]), _scores([(q1, k, mk) for k, mk in keys])

    def finish(i, state):
        e0, l0 = _softmax(state[0])
        e1, l1 = _softmax(state[1])
        r0 = 1.0 / l0
        r1 = lam / l1
        acc = None
        for j in range(i + 1):
            d = _dot((e0[j] * r0 - e1[j] * r1).astype(BF16), _tile(v_ref, j))
            acc = d if acc is None else acc + d
        o_ref[i * TQ:(i + 1) * TQ, :] = (_rms(acc, sg) * (1.0 - lambda_init)).astype(o_ref.dtype)

    _run_pipelined([(functools.partial(score, i), functools.partial(finish, i)) for i in range(nq)])


def _diff_attention(qkv, lam_rows, subln_g, batch, seq, lambda_init):
    t = batch * seq
    h = DIFF_HEADS
    return pl.pallas_call(
        functools.partial(_diff_kernel, lambda_init=lambda_init),
        grid=(batch, h),
        in_specs=[pl.BlockSpec((seq, LANES), lambda b, hh: (b, hh)),
                  pl.BlockSpec((seq, LANES), lambda b, hh: (b, h + hh)),
                  pl.BlockSpec((seq, LANES), lambda b, hh: (b, 2 * h + hh)),
                  pl.BlockSpec((8, LANES), lambda b, hh: (0, 0)),
                  pl.BlockSpec((1, LANES), lambda b, hh: (0, 0))],
        out_specs=pl.BlockSpec((seq, LANES), lambda b, hh: (b, hh)),
        out_shape=jax.ShapeDtypeStruct((t, h * LANES), BF16),
        compiler_params=_cparams(("parallel", "parallel")),
        name="diff_attention",
    )(qkv, qkv, qkv, lam_rows, subln_g.reshape(1, LANES))


def _moba_kernel(q_ref, k_ref, v_ref, o_ref, ka_s):
    seq = k_ref.shape[0]
    nb = seq // MOBA_BLOCK
    lane = _lane_iota((TQ, LANES))
    lo = lane < HEAD_DIM
    _, causal = _causal_masks()
    km = jnp.concatenate([jnp.mean(_tile(k_ref, j).astype(F32), axis=0, keepdims=True) for j in range(nb)],
                         axis=0)
    for a in range(2):
        spare = HEAD_DIM * (1 - a)
        for j in range(nb):
            kf = _tile(k_ref, j).astype(F32)
            ind = jnp.where(lane == spare + j, 1.0, 0.0)
            ka_s[a, j * TQ:(j + 1) * TQ, :] = (jnp.where(lo, kf, ind) if a == 0
                                               else jnp.where(lo, ind, kf)).astype(BF16)
    q_aug = {}
    for i in range(nb):
        heads = _split_heads(_tile(q_ref, i))
        for a in range(2):
            q_aug[i, a] = heads[a]
            if i > 0:
                gate_t = _dot_nt(km, heads[a].astype(F32), precision=lax.Precision.HIGHEST)
                rank = _rank_rows(gate_t, i)
                drop = jnp.logical_and(_row_iota(gate_t.shape) < i, rank >= MOBA_TOPK)
                bias = _bias_lanes(jnp.where(drop, NEG, 0.0), HEAD_DIM * (1 - a))
                q_aug[i, a] = (heads[a].astype(F32) + bias).astype(BF16)
    outs = {}

    def score(i, a):
        return _scores([(q_aug[i, a], ka_s[a, j * TQ:(j + 1) * TQ, :], causal if j == i else None)
                        for j in range(i + 1)])

    def finish(i, a, s_list):
        outs[i, a] = _attend(s_list, [_tile(v_ref, j) for j in range(i + 1)])
        if a == 1:
            o_ref[i * TQ:(i + 1) * TQ, :] = jnp.where(lo, outs[i, 0], outs[i, 1]).astype(o_ref.dtype)

    _run_pipelined([(functools.partial(score, i, a), functools.partial(finish, i, a))
                    for i in reversed(range(nb)) for a in range(2)])


def _moba_attention(qkv, batch, seq):
    assert MOBA_BLOCK == TQ and seq % MOBA_BLOCK == 0 and seq // MOBA_BLOCK <= HEAD_DIM
    t = batch * seq
    hp = MOBA_HEADS // 2
    return pl.pallas_call(
        _moba_kernel,
        grid=(batch, hp),
        in_specs=[pl.BlockSpec((seq, LANES), lambda b, hh: (b, hh)),
                  pl.BlockSpec((seq, LANES), lambda b, hh: (b, hp + hh)),
                  pl.BlockSpec((seq, LANES), lambda b, hh: (b, 2 * hp + hh))],
        out_specs=pl.BlockSpec((seq, LANES), lambda b, hh: (b, hh)),
        out_shape=jax.ShapeDtypeStruct((t, hp * LANES), BF16),
        scratch_shapes=[pltpu.VMEM((2, seq, LANES), BF16)],
        compiler_params=_cparams(("parallel", "parallel")),
        name="moba_attention",
    )(qkv, qkv, qkv)


def _dil_kernel(q0_ref, q1_ref, q2_ref, k0_ref, k1_ref, k2_ref, v0_ref, v1_ref, v2_ref, o_ref):
    seq = k0_ref.shape[0]
    nq = seq // TQ
    (w0, _), (w1, d1), (w2, d2) = DIL_PATTERNS
    assert w0 < TQ and w1 == 2 * TQ and w2 >= seq
    diff, causal = _causal_masks()
    on1 = (diff & (d1 - 1)) == 0
    on2 = (diff & (d2 - 1)) == 0
    m0_diag = jnp.logical_and(causal, diff <= w0)
    m0_prev = diff + TQ <= w0
    m1_diag = jnp.logical_and(causal, on1)
    m1_prev2 = jnp.logical_and(on1, diff <= 0)
    m2_diag = jnp.logical_and(causal, on2)
    lo = _lo_half((TQ, LANES))
    k_refs = (k0_ref, k1_ref, k2_ref)
    v_refs = (v0_ref, v1_ref, v2_ref)
    q_refs = (q0_ref, q1_ref, q2_ref)
    outs = {}

    def tiles(i):
        t = [(0, i, m0_diag), (1, i, m1_diag), (2, i, m2_diag)]
        if i >= 1:
            t += [(0, i - 1, m0_prev), (1, i - 1, on1)]
        if i >= 2:
            t.append((1, i - 2, m1_prev2))
        return t + [(2, j, on2) for j in range(i)]

    def score(i, a):
        hs = [_split_heads(_tile(r, i))[a] for r in q_refs]
        return _scores([(hs[g], _tile(k_refs[g], j), mask) for g, j, mask in tiles(i)])

    def finish(i, a, s_list):
        outs[i, a] = _attend(s_list, [_tile(v_refs[g], j) for g, j, _ in tiles(i)])
        if a == 1:
            o_ref[i * TQ:(i + 1) * TQ, :] = jnp.where(lo, outs[i, 0], outs[i, 1]).astype(o_ref.dtype)

    _run_pipelined([(functools.partial(score, i, a), functools.partial(finish, i, a))
                    for i in reversed(range(nq)) for a in range(2)])


def _dil_attention(qkv, batch, seq):
    t = batch * seq
    hp = DIL_HEADS // 2
    per_g = 3 * hp

    def spec(g, which):
        return pl.BlockSpec((seq, LANES), lambda b, hh: (b, g * per_g + which * hp + hh))

    return pl.pallas_call(
        _dil_kernel,
        grid=(batch, hp),
        in_specs=[spec(0, 0), spec(1, 0), spec(2, 0), spec(0, 1), spec(1, 1), spec(2, 1),
                  spec(0, 2), spec(1, 2), spec(2, 2)],
        out_specs=pl.BlockSpec((seq, LANES), lambda b, hh: (b, hh)),
        out_shape=jax.ShapeDtypeStruct((t, hp * LANES), BF16),
        compiler_params=_cparams(("parallel", "parallel")),
        name="dilated_attention",
    )(*([qkv] * 9))


N_CMP_PAD = 128


def _cmp_kernel(zk_ref, zv_ref, w1k_ref, w1v_ref, pk_ref, pv_ref, w1fk_ref, w1fv_ref, w2k_ref, w2v_ref,
                ko_ref, vo_ref):
    def one(z_ref, w1_ref, pos_ref, w1f_ref, w2_ref, o_ref):
        z = z_ref[...]
        a = _dot(z, w1_ref[0])
        b = _dot(z, w1_ref[1])
        b_next = jnp.concatenate([b[1:], jnp.zeros_like(b[:1])], axis=0)
        const = _dot(pos_ref[...], w1f_ref[...])[0:1]
        hid = a + b_next + jnp.concatenate([const] * NSA_GROUPS, axis=1)
        o_ref[...] = _dot(jax.nn.gelu(hid, approximate=True).astype(BF16), w2_ref[...]).astype(o_ref.dtype)

    one(zk_ref, w1k_ref, pk_ref, w1fk_ref, w2k_ref, ko_ref)
    one(zv_ref, w1v_ref, pv_ref, w1fv_ref, w2v_ref, vo_ref)


def _nsa_compress(kc, vc, ck_pos, ck_w1, ck_w2, cv_pos, cv_w1, cv_w2, batch, seq):
    g, dk, f = NSA_GROUPS, HEAD_DIM, NSA_CMP_HIDDEN
    half = NSA_CMP_LEN // 2
    assert NSA_CMP_STRIDE == half and seq // half == N_CMP_PAD
    zk = kc.reshape(batch * N_CMP_PAD, half * g * dk)
    zv = vc.reshape(batch * N_CMP_PAD, half * g * dk)

    def expand_w1(w1):
        w = w1.reshape(2, half, dk, f)
        eye = jnp.eye(g, dtype=w1.dtype)
        big = jnp.einsum('acdf,gh->acgdhf', w, eye)
        return big.reshape(2, half * g * dk, g * f).astype(BF16)

    def expand_w2(w2):
        dup = jnp.concatenate([w2, w2], axis=1)
        eye = jnp.eye(g, dtype=w2.dtype)
        return jnp.einsum('fd,gh->gfhd', dup, eye).reshape(g * f, g * LANES).astype(BF16)

    def pos_rows(pos):
        flat = pos.reshape(1, NSA_CMP_LEN * dk)
        return jnp.concatenate([flat, jnp.zeros((7, NSA_CMP_LEN * dk), pos.dtype)], axis=0).astype(BF16)

    kd = half * g * dk
    const2 = lambda b: (0, 0)
    const3 = lambda b: (0, 0, 0)
    return pl.pallas_call(
        _cmp_kernel,
        grid=(batch,),
        in_specs=[pl.BlockSpec((N_CMP_PAD, kd), lambda b: (b, 0)),
                  pl.BlockSpec((N_CMP_PAD, kd), lambda b: (b, 0)),
                  pl.BlockSpec((2, kd, g * f), const3),
                  pl.BlockSpec((2, kd, g * f), const3),
                  pl.BlockSpec((8, NSA_CMP_LEN * dk), const2),
                  pl.BlockSpec((8, NSA_CMP_LEN * dk), const2),
                  pl.BlockSpec((NSA_CMP_LEN * dk, f), const2),
                  pl.BlockSpec((NSA_CMP_LEN * dk, f), const2),
                  pl.BlockSpec((g * f, g * LANES), const2),
                  pl.BlockSpec((g * f, g * LANES), const2)],
        out_specs=[pl.BlockSpec((N_CMP_PAD, g * LANES), lambda b: (b, 0))] * 2,
        out_shape=[jax.ShapeDtypeStruct((batch * N_CMP_PAD, g * LANES), BF16)] * 2,
        compiler_params=_cparams(("parallel",)),
        name="nsa_compress",
    )(zk, zv, expand_w1(ck_w1), expand_w1(cv_w1), pos_rows(ck_pos), pos_rows(cv_pos),
      ck_w1.reshape(NSA_CMP_LEN * dk, f).astype(BF16), cv_w1.reshape(NSA_CMP_LEN * dk, f).astype(BF16),
      expand_w2(ck_w2), expand_w2(cv_w2))


SLC_SHIFT = NSA_SLC_BLOCK.bit_length() - 1


def _nsa_heads(q_pair):
    lo = _lo_half(q_pair.shape)
    qf = q_pair.astype(F32)
    return jnp.where(lo, qf, 0.0), jnp.where(lo, pltpu.roll(qf, HEAD_DIM, 1), 0.0)


def _lane_column(x, c):
    return jnp.sum(jnp.where(_lane_iota(x.shape) == c, x, 0.0), axis=1, keepdims=True)


def _nsa_select_kernel(q_ref, kc_ref, vc_ref, gate_ref, ov_ref, oc_ref, bias_ref):
    seq = q_ref.shape[0]
    nq = seq // TQ
    n_slc = seq // NSA_SLC_BLOCK
    assert 1 << SLC_SHIFT == NSA_SLC_BLOCK and HEAD_DIM + n_slc <= LANES
    lane = _lane_iota((TQ, LANES))
    row = _row_iota((TQ, LANES))
    lo = lane < HEAD_DIM
    kc = kc_ref[...]
    vc = vc_ref[...]
    jb = _row_iota((n_slc, TQ))
    cmask, scores = [], []
    for i in range(nq):
        cmask.append(jnp.logical_and(lane * NSA_CMP_STRIDE + (NSA_CMP_LEN - 1) <= i * TQ + row,
                                     lane < N_CMP_PAD - 1))
        heads_bf = []
        for pr in range(NSA_REP // 2):
            heads_bf.extend(h.astype(BF16) for h in
                            _nsa_heads(q_ref[i * TQ:(i + 1) * TQ, pr * LANES:(pr + 1) * LANES]))
        scores.append([jnp.where(cmask[i], _dot_nt(heads_bf[r], kc), NEG) for r in range(NSA_REP)])
    p_sums, o_cmps = [], []
    for i in range(nq):
        p_sum = jnp.zeros((TQ, N_CMP_PAD), F32)
        o_cmp = []
        for s in scores[i]:
            e = jnp.where(cmask[i], jnp.exp2(s - jnp.max(s, axis=1, keepdims=True)), 0.0)
            p = e / jnp.maximum(jnp.sum(e, axis=1, keepdims=True), 1e-30)
            p_sum = p_sum + p
            o_cmp.append(_dot(p.astype(BF16), vc))
        p_sums.append(p_sum)
        o_cmps.append(o_cmp)
    for i in range(nq):
        o_cmp = o_cmps[i]
        imp_t = _dot_nt(ov_ref[...], p_sums[i], precision=lax.Precision.HIGHEST)
        back = ((i * TQ + _lane_iota((n_slc, TQ))) >> SLC_SHIFT) - jb
        ok = back >= 0
        forced = jnp.logical_or(jb == 0, jnp.logical_and(ok, back < NSA_N_LOCAL))
        score = jnp.where(ok, imp_t + jnp.where(forced, NSA_FORCE_BONUS, 0.0), NEG)
        rank = _rank_rows(score, n_slc)
        keep = jnp.logical_and(ok, rank < min(NSA_TOPN, n_slc))
        bias_ref[i * TQ:(i + 1) * TQ, :] = _bias_lanes(jnp.where(keep, 0.0, NEG), HEAD_DIM).astype(bias_ref.dtype)

        gates = jax.nn.sigmoid(gate_ref[i * TQ:(i + 1) * TQ, :])
        for pr in range(NSA_REP // 2):
            r0, r1 = 2 * pr, 2 * pr + 1
            oc_ref[i * TQ:(i + 1) * TQ, pr * LANES:(pr + 1) * LANES] = jnp.where(
                lo, o_cmp[r0] * _lane_column(gates, 3 * r0), o_cmp[r1] * _lane_column(gates, 3 * r1)
            ).astype(oc_ref.dtype)


def _nsa_attn_kernel(q_ref, ks_ref, vs_ref, kw_ref, vw_ref, bias_ref, gate_ref, oc_ref, o_ref, ksa_s):
    seq = ks_ref.shape[0]
    nq = seq // TQ
    per_tile = TQ // NSA_SLC_BLOCK
    assert NSA_WINDOW == 2 * TQ
    pr = pl.program_id(2)
    lane = _lane_iota((TQ, LANES))
    row = _row_iota((TQ, LANES))
    lo = lane < HEAD_DIM
    diff, causal = _causal_masks()
    for j in range(nq):
        ind = jnp.where(lane - HEAD_DIM == j * per_tile + (row >> SLC_SHIFT), 1.0, 0.0)
        ksa_s[j * TQ:(j + 1) * TQ, :] = jnp.where(lo, _tile(ks_ref, j).astype(F32), ind).astype(BF16)
    outs = {}

    def win_tiles(i):
        t = [(i, causal)]
        if i >= 1:
            t.append((i - 1, None))
        if i >= 2:
            t.append((i - 2, diff < 0))
        return t

    def score(i, h):
        head = _nsa_heads(_tile(q_ref, i))[h]
        q_aug = (head + _tile(bias_ref, i).astype(F32)).astype(BF16)
        q_bf = head.astype(BF16)
        return (_scores([(q_aug, _tile(ksa_s, j), causal if j == i else None) for j in range(i + 1)]),
                _scores([(q_bf, _tile(kw_ref, j), mask) for j, mask in win_tiles(i)]))

    def finish(i, h, state):
        o_slc = _attend(state[0], [_tile(vs_ref, j) for j in range(i + 1)])
        o_win = _attend(state[1], [_tile(vw_ref, j) for j, _ in win_tiles(i)])
        gates = jax.nn.sigmoid(_tile(gate_ref, i))
        c0 = 3 * (2 * pr + h)
        outs[i, h] = o_slc * _lane_column(gates, c0 + 1) + o_win * _lane_column(gates, c0 + 2)
        if h == 1:
            o_ref[i * TQ:(i + 1) * TQ, :] = (_tile(oc_ref, i).astype(F32)
                                             + jnp.where(lo, outs[i, 0], outs[i, 1])).astype(o_ref.dtype)

    _run_pipelined([(functools.partial(score, i, h), functools.partial(finish, i, h))
                    for i in reversed(range(nq)) for h in range(2)])


def _nsa_overlap_t(seq):
    n_slc = seq // NSA_SLC_BLOCK
    n = jnp.arange(N_CMP_PAD)
    first = n * NSA_CMP_STRIDE
    last = first + NSA_CMP_LEN - 1
    s_start = jnp.arange(n_slc) * NSA_SLC_BLOCK
    ov = (first[None, :] <= (s_start + NSA_SLC_BLOCK - 1)[:, None]) & (last[None, :] >= s_start[:, None])
    ov = ov & (n[None, :] < (seq - NSA_CMP_LEN) // NSA_CMP_STRIDE + 1)
    return ov.astype(F32)


def _nsa_attention(q, ks, vs, kw, vw, kcmp, vcmp, gates, batch, seq):
    t = batch * seq
    g = NSA_GROUPS
    qb = NSA_REP // 2
    n_slc = seq // NSA_SLC_BLOCK
    grp = lambda: pl.BlockSpec((seq, LANES), lambda b, gg: (b, gg))
    cm = lambda: pl.BlockSpec((N_CMP_PAD, LANES), lambda b, gg: (b, gg))
    o_cmp, bias = pl.pallas_call(
        _nsa_select_kernel,
        grid=(batch, g),
        in_specs=[pl.BlockSpec((seq, qb * LANES), lambda b, gg: (b, gg)), cm(), cm(), grp(),
                  pl.BlockSpec((n_slc, N_CMP_PAD), lambda b, gg: (0, 0))],
        out_specs=[pl.BlockSpec((seq, qb * LANES), lambda b, gg: (b, gg)), grp()],
        out_shape=[jax.ShapeDtypeStruct((t, NSA_HEADS * HEAD_DIM), BF16),
                   jax.ShapeDtypeStruct((t, g * LANES), BF16)],
        compiler_params=_cparams(("parallel", "parallel")),
        name="nsa_select",
    )(q, kcmp, vcmp, gates, _nsa_overlap_t(seq))
    kv = lambda: pl.BlockSpec((seq, LANES), lambda b, gg, pr: (b, gg))
    pair = lambda: pl.BlockSpec((seq, LANES), lambda b, gg, pr: (b, gg * qb + pr))
    return pl.pallas_call(
        _nsa_attn_kernel,
        grid=(batch, g, qb),
        in_specs=[pair(), kv(), kv(), kv(), kv(), kv(), kv(), pair()],
        out_specs=pair(),
        out_shape=jax.ShapeDtypeStruct((t, NSA_HEADS * HEAD_DIM), BF16),
        scratch_shapes=[pltpu.VMEM((seq, LANES), BF16)],
        compiler_params=_cparams(("parallel", "parallel", "arbitrary")),
        name="nsa_attention",
    )(q, ks, vs, kw, vw, bias, gates, o_cmp)


def _plain_blocks(n_total, n_rope, n_q):
    return [(0, b, b < n_rope, Q_SCALE if b < n_q else 1.0) for b in range(n_total)]


def _mixer_diff(x, g, w_in, lam_q1, lam_k1, lam_q2, lam_k2, subln_g, cos, sin, batch, seq, lambda_init):
    h = DIFF_HEADS
    blocks = _plain_blocks(3 * h, 2 * h, h)
    (qkv,) = _proj(x, g, w_in.astype(BF16), cos, sin, blocks, [(3 * h, BF16)], "diff_in_proj")
    pad = lambda v: jnp.pad(v.astype(F32), (0, LANES - HEAD_DIM))
    lam_rows = jnp.stack([pad(lam_q1), pad(lam_k1), pad(lam_q2), pad(lam_k2)]
                         + [jnp.zeros((LANES,), F32)] * 4)
    return _diff_attention(qkv, lam_rows, subln_g, batch, seq, lambda_init)


def _mixer_moba(x, g, w_in, cos, sin, batch, seq):
    hp = MOBA_HEADS // 2
    blocks = _plain_blocks(3 * hp, 2 * hp, hp)
    (qkv,) = _proj(x, g, w_in.astype(BF16), cos, sin, blocks, [(3 * hp, BF16)], "moba_in_proj")
    return _moba_attention(qkv, batch, seq)


def _mixer_dil(x, g, w_in, cos, sin, batch, seq):
    hp = DIL_HEADS // 2
    blocks = []
    for gi in range(len(DIL_PATTERNS)):
        for which in range(3):
            for b in range(hp):
                blocks.append((0, len(blocks), which < 2, Q_SCALE if which == 0 else 1.0))
    (qkv,) = _proj(x, g, w_in.astype(BF16), cos, sin, blocks, [(len(blocks), BF16)], "dil_in_proj")
    return _dil_attention(qkv, batch, seq)


def _mixer_nsa(x, g, w_in, ck_pos, ck_w1, ck_w2, cv_pos, cv_w1, cv_w2, cos, sin, batch, seq):
    hq, gr, dk = NSA_HEADS, NSA_GROUPS, HEAD_DIM
    kvw = gr * dk
    o_q = 0
    o_kc, o_vc, o_ks, o_vs, o_kw, o_vw, o_g = (hq * dk + n * kvw for n in range(7))
    n_gate = 3 * hq

    def dup(off):
        cols = w_in[:, off:off + kvw].reshape(D_MODEL, gr, dk)
        return jnp.concatenate([cols, cols], axis=2).reshape(D_MODEL, gr * LANES)

    per_group = n_gate // gr
    gate_cols = jnp.pad(w_in[:, o_g:o_g + n_gate].reshape(D_MODEL, gr, per_group),
                        ((0, 0), (0, 0), (0, LANES - per_group))).reshape(D_MODEL, gr * LANES)
    w = jnp.concatenate([w_in[:, o_q:o_q + hq * dk], dup(o_ks), dup(o_kw), w_in[:, o_kc:o_kc + kvw],
                         dup(o_vs), dup(o_vw), w_in[:, o_vc:o_vc + kvw], gate_cols], axis=1).astype(BF16)
    nqb = hq * dk // LANES
    blocks = [(0, b, True, Q_SCALE) for b in range(nqb)]
    blocks += [(1, b, True, 1.0) for b in range(gr)]
    blocks += [(2, b, True, 1.0) for b in range(gr)]
    blocks += [(3, b, True, 1.0) for b in range(kvw // LANES)]
    blocks += [(4, b, False, 1.0) for b in range(gr)]
    blocks += [(5, b, False, 1.0) for b in range(gr)]
    blocks += [(6, b, False, 1.0) for b in range(kvw // LANES)]
    blocks += [(7, b, False, 1.0) for b in range(gr)]
    outs = [(nqb, BF16), (gr, BF16), (gr, BF16), (kvw // LANES, BF16), (gr, BF16), (gr, BF16),
            (kvw // LANES, BF16), (gr, F32)]
    q, ks, kw, kc, vs, vw, vc, gates = _proj(x, g, w, cos, sin, blocks, outs, "nsa_in_proj")
    kcmp, vcmp = _nsa_compress(kc, vc, ck_pos, ck_w1, ck_w2, cv_pos, cv_w1, cv_w2, batch, seq)
    return _nsa_attention(q, ks, vs, kw, vw, kcmp, vcmp, gates, batch, seq)


def kernel(x, p, positions, norm_gains, ple_norm_g, ple_gate_w, ple_proj_w, ffn_w_up, ffn_conv_w, ffn_conv_b, ffn_w_down, diff_w_in, diff_w_out, diff_lam_q1, diff_lam_k1, diff_lam_q2, diff_lam_k2, diff_subln_g, nsa_w_in, nsa_w_out, nsa_ck_pos, nsa_ck_w1, nsa_ck_w2, nsa_cv_pos, nsa_cv_w1, nsa_cv_w2, moba_w_in, moba_w_out, dil_w_in, dil_w_out):
    batch, seq, d = x.shape
    t = batch * seq
    cos, sin = _rope_tables(positions)
    xt = x.reshape(t, d)
    for i in range(DEPTH):
        kind, j = i % N_MIXERS, i // N_MIXERS
        g0 = norm_gains[i, 0]
        if kind == 0:
            lambda_init = 0.8 - 0.6 * math.exp(-0.3 * i)
            o = _mixer_diff(xt, g0, diff_w_in[j], diff_lam_q1[j], diff_lam_k1[j], diff_lam_q2[j],
                            diff_lam_k2[j], diff_subln_g[j], cos, sin, batch, seq, lambda_init)
            w_out = diff_w_out[j]
        elif kind == 1:
            o = _mixer_nsa(xt, g0, nsa_w_in[j], nsa_ck_pos[j], nsa_ck_w1[j], nsa_ck_w2[j],
                           nsa_cv_pos[j], nsa_cv_w1[j], nsa_cv_w2[j], cos, sin, batch, seq)
            w_out = nsa_w_out[j]
        elif kind == 2:
            o = _mixer_moba(xt, g0, moba_w_in[j], cos, sin, batch, seq)
            w_out = moba_w_out[j]
        else:
            o = _mixer_dil(xt, g0, dil_w_in[j], cos, sin, batch, seq)
            w_out = dil_w_out[j]
        xt = _out_proj(o, w_out.astype(BF16), xt, norm_gains[i, 1], "out_proj_%d" % i)
        xt = _ffn(xt, norm_gains[i, 2], ffn_w_up[i].astype(BF16), ffn_conv_w[i], ffn_conv_b[i],
                  ffn_w_down[i].astype(BF16), norm_gains[i, 3], ple_norm_g[i],
                  ple_gate_w[i].astype(BF16), ple_proj_w[i].astype(BF16), p[i].reshape(t, PLE_DIM),
                  seq, "ffn_%d" % i)
    return xt.reshape(batch, seq, d)
```

```python
import functools
import math

import jax
import jax.numpy as jnp
from jax import lax
from jax.experimental import pallas as pl
from jax.experimental.pallas import tpu as pltpu

F32 = jnp.float32
BF16 = jnp.bfloat16

D_MODEL = 1024
DEPTH = 4
N_MIXERS = 4
HEAD_DIM = 64
ROPE_THETA = 10000.0
NORM_EPS = 1e-6
PLE_DIM = 256
NEG = -1e30

DIFF_HEADS = D_MODEL // (2 * HEAD_DIM)

NSA_HEADS = D_MODEL // HEAD_DIM
NSA_GROUPS = 4
NSA_REP = NSA_HEADS // NSA_GROUPS
NSA_CMP_LEN = 32
NSA_CMP_STRIDE = 16
NSA_CMP_HIDDEN = 128
NSA_SLC_BLOCK = 64
NSA_TOPN = 16
NSA_N_LOCAL = 2
NSA_WINDOW = 512
NSA_FORCE_BONUS = 1e4

MOBA_HEADS = D_MODEL // HEAD_DIM
MOBA_BLOCK = 256
MOBA_TOPK = 3

DIL_PATTERNS = ((128, 1), (512, 4), (2048, 16))
DIL_HEADS = 8

D_FF = 2816
CONV_WIDTH = 3

LANES = 128
MXU_COLS = 256
TQ = 256
LOG2E = 1.4426950408889634
Q_SCALE = (HEAD_DIM ** -0.5) * LOG2E
VMEM_LIMIT = 56 * 1024 * 1024


def _cparams(sem):
    return pltpu.CompilerParams(dimension_semantics=sem, vmem_limit_bytes=VMEM_LIMIT)


def _rms(x, g):
    return x * lax.rsqrt(jnp.mean(x * x, axis=-1, keepdims=True) + NORM_EPS) * g


def _dot(a, b):
    return jnp.dot(a, b, preferred_element_type=F32)


def _dot_nt(a, b, precision=None):
    return lax.dot_general(a, b, (((1,), (1,)), ((), ())), preferred_element_type=F32,
                           precision=precision)


def _lane_iota(shape):
    return lax.broadcasted_iota(jnp.int32, shape, 1)


def _row_iota(shape):
    return lax.broadcasted_iota(jnp.int32, shape, 0)


def _rope_kernel(pos_ref, inv_ref, cos_ref, sin_ref):
    ang = pos_ref[...].astype(F32) * inv_ref[...]
    lane = _lane_iota(ang.shape)
    cos_ref[...] = jnp.cos(ang)
    sin_ref[...] = jnp.where((lane & 63) < 32, -jnp.sin(ang), jnp.sin(ang))


def _rope_tables(positions):
    t = positions.size
    inv = ROPE_THETA ** (-jnp.arange(0, HEAD_DIM, 2, dtype=F32) / HEAD_DIM)
    inv = jnp.tile(inv, 4)[None, :]
    tm = 2048
    return pl.pallas_call(
        _rope_kernel,
        grid=(t // tm,),
        in_specs=[pl.BlockSpec((tm, 1), lambda i: (i, 0)),
                  pl.BlockSpec((1, LANES), lambda i: (0, 0))],
        out_specs=[pl.BlockSpec((tm, LANES), lambda i: (i, 0))] * 2,
        out_shape=[jax.ShapeDtypeStruct((t, LANES), F32)] * 2,
        compiler_params=_cparams(("parallel",)),
        name="rope_tables",
    )(positions.reshape(t, 1), inv)


def _proj_kernel(x_ref, g_ref, w_ref, cos_ref, sin_ref, *o_refs, blocks):
    hn = _rms(x_ref[...], g_ref[...]).astype(BF16)
    cos = cos_ref[...]
    sin = sin_ref[...]
    first = (_lane_iota(cos.shape) & 63) < 32
    n_chunks = len(blocks) // 2
    for c in range(n_chunks):
        acc = _dot(hn, w_ref[:, c * MXU_COLS:(c + 1) * MXU_COLS])
        for s in range(2):
            out_idx, out_blk, rope, scale = blocks[2 * c + s]
            if out_idx is None:
                continue
            a = acc[:, s * LANES:(s + 1) * LANES]
            if rope:
                rot = jnp.where(first, pltpu.roll(a, 96, 1), pltpu.roll(a, 32, 1))
                a = a * cos + rot * sin
            if scale != 1.0:
                a = a * scale
            o_ref = o_refs[out_idx]
            o_ref[:, out_blk * LANES:(out_blk + 1) * LANES] = a.astype(o_ref.dtype)


def _proj(x, g, w, cos, sin, blocks, outs, name):
    t = x.shape[0]
    n = w.shape[1]
    assert n == len(blocks) * LANES and len(blocks) % 2 == 0
    tm = 512
    return pl.pallas_call(
        functools.partial(_proj_kernel, blocks=tuple(blocks)),
        grid=(t // tm,),
        in_specs=[pl.BlockSpec((tm, D_MODEL), lambda i: (i, 0)),
                  pl.BlockSpec((1, D_MODEL), lambda i: (0, 0)),
                  pl.BlockSpec((D_MODEL, n), lambda i: (0, 0)),
                  pl.BlockSpec((tm, LANES), lambda i: (i, 0)),
                  pl.BlockSpec((tm, LANES), lambda i: (i, 0))],
        out_specs=[pl.BlockSpec((tm, nb * LANES), lambda i: (i, 0)) for nb, _ in outs],
        out_shape=[jax.ShapeDtypeStruct((t, nb * LANES), dt) for nb, dt in outs],
        compiler_params=_cparams(("parallel",)),
        name=name,
    )(x, g.reshape(1, D_MODEL), w, cos, sin)


def _out_kernel(o_ref, w_ref, x_ref, g_ref, y_ref):
    y = _dot(o_ref[...], w_ref[...])
    y_ref[...] = x_ref[...] + _rms(y, g_ref[...])


def _out_proj(o, w, x, g, name):
    t, k = o.shape
    tm = 512
    return pl.pallas_call(
        _out_kernel,
        grid=(t // tm,),
        in_specs=[pl.BlockSpec((tm, k), lambda i: (i, 0)),
                  pl.BlockSpec((k, D_MODEL), lambda i: (0, 0)),
                  pl.BlockSpec((tm, D_MODEL), lambda i: (i, 0)),
                  pl.BlockSpec((1, D_MODEL), lambda i: (0, 0))],
        out_specs=pl.BlockSpec((tm, D_MODEL), lambda i: (i, 0)),
        out_shape=jax.ShapeDtypeStruct((t, D_MODEL), F32),
        compiler_params=_cparams(("parallel",)),
        name=name,
    )(o, w, x, g.reshape(1, D_MODEL))


FF_CHUNK = 256
FF_TM = 512
FF_HALO = 16


def _gelu_tanh(x):
    k = math.sqrt(2.0 / math.pi)
    h = 0.5 * x
    return h + h * jnp.tanh(x * (k + (k * 0.044715) * (x * x)))


def _ffn_kernel(x_ref, xh_ref, g2_ref, wu_ref, cw_ref, cb_ref, wd_ref, g3_ref, pg_ref, pgw_ref, ppw_ref,
                p_ref, y_ref, hn_s, *, seq):
    i = pl.program_id(0)
    tm = FF_TM
    n_sub = x_ref.shape[0] // tm
    n_c = D_FF // FF_CHUNK

    def rows(s):
        return slice(s * tm, (s + 1) * tm)

    def prologue(s):
        hn_s[s, FF_HALO:, :] = _rms(x_ref[rows(s), :], g2_ref[...]).astype(BF16)
        if s == 0:
            halo = _rms(xh_ref[...], g2_ref[...])
            seq_start = (i * x_ref.shape[0]) % seq == 0
            halo = jnp.where(seq_start, 0.0, halo)
        else:
            halo = _rms(x_ref[s * tm - FF_HALO:s * tm, :], g2_ref[...])
        hn_s[s, :FF_HALO, :] = halo.astype(BF16)

    def conv(u, col):
        cols = slice(col, col + FF_CHUNK)
        out = cb_ref[:, cols] + pltpu.roll(u, 2, 0)[FF_HALO:] * cw_ref[0:1, cols]
        out = out + pltpu.roll(u, 1, 0)[FF_HALO:] * cw_ref[1:2, cols]
        return out + u[FF_HALO:] * cw_ref[2:3, cols]

    def up(s, c):
        hn = hn_s[s]
        gcol, vcol = c * FF_CHUNK, D_FF + c * FF_CHUNK
        return (_dot(hn, wu_ref[:, gcol:gcol + FF_CHUNK]), _dot(hn, wu_ref[:, vcol:vcol + FF_CHUNK]))

    def down(c, act, acc):
        d = _dot(act, wd_ref[c * FF_CHUNK:(c + 1) * FF_CHUNK, :])
        return d if acc is None else acc + d

    def mlp(s, hook):
        acc = None
        pending = None
        u_next = up(s, 0)
        for c in range(n_c):
            ug, uv = u_next
            if c + 1 < n_c:
                u_next = up(s, c + 1)
            if c == 2 and hook is not None:
                hook()
            act = (_gelu_tanh(conv(ug, c * FF_CHUNK)) * conv(uv, D_FF + c * FF_CHUNK)).astype(BF16)
            if pending is not None:
                acc = down(c - 1, pending, acc)
            pending = act
        return down(n_c - 1, pending, acc)

    for s in range(n_sub):
        prologue(s)
    hook = None
    for s in range(n_sub):
        acc = mlp(s, hook)
        x2 = x_ref[rows(s), :] + _rms(acc, g3_ref[...])
        hq = _rms(x2, pg_ref[...]).astype(BF16)
        state = {}

        def gate_matmuls(s=s, hq=hq, state=state):
            state["g"] = _dot(hq, pgw_ref[...])
            state["p"] = _dot(p_ref[rows(s), :].astype(BF16), ppw_ref[...])

        def finish(s=s, x2=x2, state=state):
            y_ref[rows(s), :] = x2 + jax.nn.sigmoid(state["g"]) * state["p"]

        if s + 1 < n_sub:
            def hook(gate_matmuls=gate_matmuls, finish=finish):
                gate_matmuls()
                finish()
        else:
            gate_matmuls()
            finish()


FF_SUB = 1


def _ffn(x, g2, w_up, conv_w, conv_b, w_down, g3, ple_g, ple_gate_w, ple_proj_w, p, seq, name):
    t = x.shape[0]
    tm = FF_TM * FF_SUB
    assert seq % tm == 0
    hb = tm // FF_HALO
    row = lambda v: v.reshape(1, -1)
    const = lambda i: (0, 0)
    resident = lambda shape: pl.BlockSpec(shape, const, pipeline_mode=pl.Buffered(1))
    return pl.pallas_call(
        functools.partial(_ffn_kernel, seq=seq),
        grid=(t // tm,),
        in_specs=[pl.BlockSpec((tm, D_MODEL), lambda i: (i, 0)),
                  pl.BlockSpec((FF_HALO, D_MODEL), lambda i: (jnp.maximum(i * hb - 1, 0), 0)),
                  pl.BlockSpec((1, D_MODEL), const),
                  resident((D_MODEL, 2 * D_FF)),
                  pl.BlockSpec((CONV_WIDTH, 2 * D_FF), const),
                  pl.BlockSpec((1, 2 * D_FF), const),
                  resident((D_FF, D_MODEL)),
                  pl.BlockSpec((1, D_MODEL), const),
                  pl.BlockSpec((1, D_MODEL), const),
                  resident((D_MODEL, D_MODEL)),
                  resident((PLE_DIM, D_MODEL)),
                  pl.BlockSpec((tm, PLE_DIM), lambda i: (i, 0))],
        out_specs=pl.BlockSpec((tm, D_MODEL), lambda i: (i, 0)),
        out_shape=jax.ShapeDtypeStruct((t, D_MODEL), F32),
        scratch_shapes=[pltpu.VMEM((FF_SUB, FF_TM + FF_HALO, D_MODEL), BF16)],
        compiler_params=_cparams(("parallel",)),
        name=name,
    )(x, x, row(g2), w_up, conv_w, row(conv_b), w_down, row(g3), row(ple_g), ple_gate_w, ple_proj_w, p)


def _lo_half(shape):
    return _lane_iota(shape) < HEAD_DIM


def _split_heads(q):
    lo = _lo_half(q.shape)
    qf = q.astype(F32)
    return jnp.where(lo, qf, 0.0).astype(q.dtype), jnp.where(lo, 0.0, qf).astype(q.dtype)


def _tile(ref, j):
    return ref[j * TQ:(j + 1) * TQ, :]


def _scores(entries):
    s_list = []
    for q_op, k, mask in (e[:3] for e in entries):
        s = _dot_nt(k, q_op)
        s_list.append(s if mask is None else jnp.where(mask, s, NEG))
    return s_list


def _attend(s_list, values_t):
    mt = None
    for s in s_list:
        mt = s if mt is None else jnp.maximum(mt, s)
    m = jnp.max(mt, axis=0, keepdims=True)
    acc = None
    for s, v_t in zip(s_list, values_t):
        d = _dot(v_t, jnp.exp2(s - m).astype(BF16))
        acc = d if acc is None else acc + d
    return acc


def _value_t(v_tile, keep=None):
    v_t = v_tile.astype(F32).T
    return (v_t if keep is None else jnp.where(keep, v_t, 1.0)).astype(BF16)


def _store_pair_values(v_ref, vt_s, *idx):
    lo_rows = _row_iota((LANES, TQ)) < HEAD_DIM
    for j in range(v_ref.shape[0] // TQ):
        v_t = _tile(v_ref, j).astype(F32).T
        cols = slice(j * TQ, (j + 1) * TQ)
        vt_s[idx + (0, slice(None), cols)] = jnp.where(lo_rows, v_t, 1.0).astype(BF16)
        vt_s[idx + (1, slice(None), cols)] = jnp.where(lo_rows, 1.0, v_t).astype(BF16)


def _pair_output(acc0, acc1):
    out0 = acc0[:HEAD_DIM] / acc0[HEAD_DIM:HEAD_DIM + 1]
    out1 = acc1[HEAD_DIM:] / acc1[0:1]
    return jnp.concatenate([out0, out1], axis=0).T


ONES_ROWS = 16


def _tile_t(vt_s, j, *idx):
    return vt_s[idx + (slice(None), slice(j * TQ, (j + 1) * TQ))]


def _run_pipelined(stages):
    prev = None
    for score_fn, finish_fn in stages:
        state = score_fn()
        if prev is not None:
            prev[1](prev[0])
        prev = (state, finish_fn)
    if prev is not None:
        prev[1](prev[0])


def _causal_masks():
    diff = _lane_iota((TQ, TQ)) - _row_iota((TQ, TQ))
    return diff, diff >= 0


def _rank_rows(score, n_valid):
    ridx = _row_iota(score.shape)
    cnt = jnp.zeros(score.shape, F32)
    for j in range(n_valid):
        other = score[j:j + 1, :]
        cnt = cnt + jnp.where(other > score, 1.0,
                              jnp.where(other == score, jnp.where(ridx > j, 1.0, 0.0), 0.0))
    return cnt


def _bias_lanes(bias_t, first_lane):
    r, tq = bias_t.shape
    parts = []
    if first_lane:
        parts.append(jnp.zeros((first_lane, tq), F32))
    parts.append(bias_t)
    if LANES - first_lane - r:
        parts.append(jnp.zeros((LANES - first_lane - r, tq), F32))
    return jnp.concatenate(parts, axis=0).T


def _diff_kernel(q_ref, k_ref, v_ref, lam_ref, sg_ref, o_ref, vt_s, *, lambda_init):
    nq = q_ref.shape[0] // TQ
    _, causal = _causal_masks()
    for j in range(nq):
        vt_s[:LANES, j * TQ:(j + 1) * TQ] = _value_t(_tile(v_ref, j))
    vt_s[LANES:, :] = jnp.ones((ONES_ROWS, vt_s.shape[1]), BF16)
    lamv = lam_ref[...]
    lam = (jnp.exp(jnp.sum(lamv[0:1] * lamv[1:2], axis=1, keepdims=True))
           - jnp.exp(jnp.sum(lamv[2:3] * lamv[3:4], axis=1, keepdims=True)) + lambda_init)
    sg = sg_ref[...]

    def score(i):
        q0, q1 = _split_heads(_tile(q_ref, i))
        keys = [(_tile(k_ref, j), causal if j == i else None) for j in range(i + 1)]
        return _scores([(q0, k, mk) for k, mk in keys]), _scores([(q1, k, mk) for k, mk in keys])

    def finish(i, state):
        v_tiles = [_tile_t(vt_s, j) for j in range(i + 1)]
        a0 = _attend(state[0], v_tiles)
        a1 = _attend(state[1], v_tiles)
        o_t = a0[:LANES] / a0[LANES:LANES + 1] - lam * (a1[:LANES] / a1[LANES:LANES + 1])
        o_ref[i * TQ:(i + 1) * TQ, :] = (_rms(o_t.T, sg) * (1.0 - lambda_init)).astype(o_ref.dtype)

    _run_pipelined([(functools.partial(score, i), functools.partial(finish, i)) for i in range(nq)])


def _diff_attention(qkv, lam_rows, subln_g, batch, seq, lambda_init):
    t = batch * seq
    h = DIFF_HEADS
    return pl.pallas_call(
        functools.partial(_diff_kernel, lambda_init=lambda_init),
        grid=(batch, h),
        in_specs=[pl.BlockSpec((seq, LANES), lambda b, hh: (b, hh)),
                  pl.BlockSpec((seq, LANES), lambda b, hh: (b, h + hh)),
                  pl.BlockSpec((seq, LANES), lambda b, hh: (b, 2 * h + hh)),
                  pl.BlockSpec((8, LANES), lambda b, hh: (0, 0)),
                  pl.BlockSpec((1, LANES), lambda b, hh: (0, 0))],
        out_specs=pl.BlockSpec((seq, LANES), lambda b, hh: (b, hh)),
        out_shape=jax.ShapeDtypeStruct((t, h * LANES), BF16),
        scratch_shapes=[pltpu.VMEM((LANES + ONES_ROWS, seq), BF16)],
        compiler_params=_cparams(("parallel", "parallel")),
        name="diff_attention",
    )(qkv, qkv, qkv, lam_rows, subln_g.reshape(1, LANES))


def _moba_kernel(q_ref, k_ref, v_ref, o_ref, ka_s, vt_s):
    seq = k_ref.shape[0]
    _store_pair_values(v_ref, vt_s)
    nb = seq // MOBA_BLOCK
    lane = _lane_iota((TQ, LANES))
    lo = lane < HEAD_DIM
    _, causal = _causal_masks()
    km = jnp.concatenate([jnp.mean(_tile(k_ref, j).astype(F32), axis=0, keepdims=True) for j in range(nb)],
                         axis=0)
    for a in range(2):
        spare = HEAD_DIM * (1 - a)
        for j in range(nb):
            kf = _tile(k_ref, j).astype(F32)
            ind = jnp.where(lane == spare + j, 1.0, 0.0)
            ka_s[a, j * TQ:(j + 1) * TQ, :] = (jnp.where(lo, kf, ind) if a == 0
                                               else jnp.where(lo, ind, kf)).astype(BF16)
    q_aug = {}
    for i in range(nb):
        heads = _split_heads(_tile(q_ref, i))
        for a in range(2):
            q_aug[i, a] = heads[a]
            if i > 0:
                gate_t = _dot_nt(km, heads[a].astype(F32), precision=lax.Precision.HIGHEST)
                rank = _rank_rows(gate_t, i)
                drop = jnp.logical_and(_row_iota(gate_t.shape) < i, rank >= MOBA_TOPK)
                bias = _bias_lanes(jnp.where(drop, NEG, 0.0), HEAD_DIM * (1 - a))
                q_aug[i, a] = (heads[a].astype(F32) + bias).astype(BF16)
    outs = {}

    def score(i, a):
        return _scores([(q_aug[i, a], ka_s[a, j * TQ:(j + 1) * TQ, :], causal if j == i else None)
                        for j in range(i + 1)])

    def finish(i, a, s_list):
        outs[i, a] = _attend(s_list, [_tile_t(vt_s, j, a) for j in range(i + 1)])
        if a == 1:
            o_ref[i * TQ:(i + 1) * TQ, :] = _pair_output(outs[i, 0], outs[i, 1]).astype(o_ref.dtype)

    _run_pipelined([(functools.partial(score, i, a), functools.partial(finish, i, a))
                    for i in reversed(range(nb)) for a in range(2)])


def _moba_attention(qkv, batch, seq):
    assert MOBA_BLOCK == TQ and seq % MOBA_BLOCK == 0 and seq // MOBA_BLOCK <= HEAD_DIM
    t = batch * seq
    hp = MOBA_HEADS // 2
    return pl.pallas_call(
        _moba_kernel,
        grid=(batch, hp),
        in_specs=[pl.BlockSpec((seq, LANES), lambda b, hh: (b, hh)),
                  pl.BlockSpec((seq, LANES), lambda b, hh: (b, hp + hh)),
                  pl.BlockSpec((seq, LANES), lambda b, hh: (b, 2 * hp + hh))],
        out_specs=pl.BlockSpec((seq, LANES), lambda b, hh: (b, hh)),
        out_shape=jax.ShapeDtypeStruct((t, hp * LANES), BF16),
        scratch_shapes=[pltpu.VMEM((2, seq, LANES), BF16), pltpu.VMEM((2, LANES, seq), BF16)],
        compiler_params=_cparams(("parallel", "parallel")),
        name="moba_attention",
    )(qkv, qkv, qkv)


def _dil_kernel(q0_ref, q1_ref, q2_ref, k0_ref, k1_ref, k2_ref, v0_ref, v1_ref, v2_ref, o_ref, vt_s):
    seq = k0_ref.shape[0]
    for g, v_ref in enumerate((v0_ref, v1_ref, v2_ref)):
        _store_pair_values(v_ref, vt_s, g)
    nq = seq // TQ
    (w0, _), (w1, d1), (w2, d2) = DIL_PATTERNS
    assert w0 < TQ and w1 == 2 * TQ and w2 >= seq
    diff, causal = _causal_masks()
    on1 = (diff & (d1 - 1)) == 0
    on2 = (diff & (d2 - 1)) == 0
    m0_diag = jnp.logical_and(causal, diff <= w0)
    m0_prev = diff + TQ <= w0
    m1_diag = jnp.logical_and(causal, on1)
    m1_prev2 = jnp.logical_and(on1, diff <= 0)
    m2_diag = jnp.logical_and(causal, on2)
    lo = _lo_half((TQ, LANES))
    k_refs = (k0_ref, k1_ref, k2_ref)
    q_refs = (q0_ref, q1_ref, q2_ref)
    outs = {}

    def tiles(i):
        t = [(0, i, m0_diag), (1, i, m1_diag), (2, i, m2_diag)]
        if i >= 1:
            t += [(0, i - 1, m0_prev), (1, i - 1, on1)]
        if i >= 2:
            t.append((1, i - 2, m1_prev2))
        return t + [(2, j, on2) for j in range(i)]

    def score(i, a):
        hs = [_split_heads(_tile(r, i))[a] for r in q_refs]
        return _scores([(hs[g], _tile(k_refs[g], j), mask) for g, j, mask in tiles(i)])

    def finish(i, a, s_list):
        outs[i, a] = _attend(s_list, [_tile_t(vt_s, j, g, a) for g, j, _ in tiles(i)])
        if a == 1:
            o_ref[i * TQ:(i + 1) * TQ, :] = _pair_output(outs[i, 0], outs[i, 1]).astype(o_ref.dtype)

    _run_pipelined([(functools.partial(score, i, a), functools.partial(finish, i, a))
                    for i in reversed(range(nq)) for a in range(2)])


def _dil_attention(qkv, batch, seq):
    t = batch * seq
    hp = DIL_HEADS // 2
    per_g = 3 * hp

    def spec(g, which):
        return pl.BlockSpec((seq, LANES), lambda b, hh: (b, g * per_g + which * hp + hh))

    return pl.pallas_call(
        _dil_kernel,
        grid=(batch, hp),
        in_specs=[spec(0, 0), spec(1, 0), spec(2, 0), spec(0, 1), spec(1, 1), spec(2, 1),
                  spec(0, 2), spec(1, 2), spec(2, 2)],
        out_specs=pl.BlockSpec((seq, LANES), lambda b, hh: (b, hh)),
        out_shape=jax.ShapeDtypeStruct((t, hp * LANES), BF16),
        scratch_shapes=[pltpu.VMEM((len(DIL_PATTERNS), 2, LANES, seq), BF16)],
        compiler_params=_cparams(("parallel", "parallel")),
        name="dilated_attention",
    )(*([qkv] * 9))


N_CMP_PAD = 128


def _cmp_kernel(zk_ref, zv_ref, w1k_ref, w1v_ref, pk_ref, pv_ref, w1fk_ref, w1fv_ref, w2k_ref, w2v_ref,
                ko_ref, vo_ref):
    def one(z_ref, w1_ref, pos_ref, w1f_ref, w2_ref, o_ref):
        z = z_ref[...]
        a = _dot(z, w1_ref[0])
        b = _dot(z, w1_ref[1])
        b_next = jnp.concatenate([b[1:], jnp.zeros_like(b[:1])], axis=0)
        const = _dot(pos_ref[...], w1f_ref[...])[0:1]
        hid = a + b_next + jnp.concatenate([const] * NSA_GROUPS, axis=1)
        o_ref[...] = _dot(jax.nn.gelu(hid, approximate=True).astype(BF16), w2_ref[...]).astype(o_ref.dtype)

    one(zk_ref, w1k_ref, pk_ref, w1fk_ref, w2k_ref, ko_ref)
    one(zv_ref, w1v_ref, pv_ref, w1fv_ref, w2v_ref, vo_ref)


def _nsa_compress(kc, vc, ck_pos, ck_w1, ck_w2, cv_pos, cv_w1, cv_w2, batch, seq):
    g, dk, f = NSA_GROUPS, HEAD_DIM, NSA_CMP_HIDDEN
    half = NSA_CMP_LEN // 2
    assert NSA_CMP_STRIDE == half and seq // half == N_CMP_PAD
    zk = kc.reshape(batch * N_CMP_PAD, half * g * dk)
    zv = vc.reshape(batch * N_CMP_PAD, half * g * dk)

    def expand_w1(w1):
        w = w1.reshape(2, half, dk, f)
        eye = jnp.eye(g, dtype=w1.dtype)
        big = jnp.einsum('acdf,gh->acgdhf', w, eye)
        return big.reshape(2, half * g * dk, g * f).astype(BF16)

    def expand_w2(w2):
        dup = jnp.concatenate([w2, w2], axis=1)
        eye = jnp.eye(g, dtype=w2.dtype)
        return jnp.einsum('fd,gh->gfhd', dup, eye).reshape(g * f, g * LANES).astype(BF16)

    def pos_rows(pos):
        flat = pos.reshape(1, NSA_CMP_LEN * dk)
        return jnp.concatenate([flat, jnp.zeros((7, NSA_CMP_LEN * dk), pos.dtype)], axis=0).astype(BF16)

    kd = half * g * dk
    const2 = lambda b: (0, 0)
    const3 = lambda b: (0, 0, 0)
    return pl.pallas_call(
        _cmp_kernel,
        grid=(batch,),
        in_specs=[pl.BlockSpec((N_CMP_PAD, kd), lambda b: (b, 0)),
                  pl.BlockSpec((N_CMP_PAD, kd), lambda b: (b, 0)),
                  pl.BlockSpec((2, kd, g * f), const3),
                  pl.BlockSpec((2, kd, g * f), const3),
                  pl.BlockSpec((8, NSA_CMP_LEN * dk), const2),
                  pl.BlockSpec((8, NSA_CMP_LEN * dk), const2),
                  pl.BlockSpec((NSA_CMP_LEN * dk, f), const2),
                  pl.BlockSpec((NSA_CMP_LEN * dk, f), const2),
                  pl.BlockSpec((g * f, g * LANES), const2),
                  pl.BlockSpec((g * f, g * LANES), const2)],
        out_specs=[pl.BlockSpec((N_CMP_PAD, g * LANES), lambda b: (b, 0))] * 2,
        out_shape=[jax.ShapeDtypeStruct((batch * N_CMP_PAD, g * LANES), BF16)] * 2,
        compiler_params=_cparams(("parallel",)),
        name="nsa_compress",
    )(zk, zv, expand_w1(ck_w1), expand_w1(cv_w1), pos_rows(ck_pos), pos_rows(cv_pos),
      ck_w1.reshape(NSA_CMP_LEN * dk, f).astype(BF16), cv_w1.reshape(NSA_CMP_LEN * dk, f).astype(BF16),
      expand_w2(ck_w2), expand_w2(cv_w2))


SLC_SHIFT = NSA_SLC_BLOCK.bit_length() - 1


def _nsa_heads(q_pair):
    lo = _lo_half(q_pair.shape)
    qf = q_pair.astype(F32)
    return jnp.where(lo, qf, 0.0), jnp.where(lo, pltpu.roll(qf, HEAD_DIM, 1), 0.0)


def _lane_column(x, c):
    return jnp.sum(jnp.where(_lane_iota(x.shape) == c, x, 0.0), axis=1, keepdims=True)


def _nsa_select_kernel(q_ref, kc_ref, vc_ref, gate_ref, ov_ref, oc_ref, bias_ref):
    seq = q_ref.shape[0]
    nq = seq // TQ
    n_slc = seq // NSA_SLC_BLOCK
    assert 1 << SLC_SHIFT == NSA_SLC_BLOCK and HEAD_DIM + n_slc <= LANES
    lane = _lane_iota((TQ, LANES))
    row = _row_iota((TQ, LANES))
    lo = lane < HEAD_DIM
    kc = kc_ref[...]
    vc = vc_ref[...]
    jb = _row_iota((n_slc, TQ))
    cmask, scores = [], []
    for i in range(nq):
        cmask.append(jnp.logical_and(lane * NSA_CMP_STRIDE + (NSA_CMP_LEN - 1) <= i * TQ + row,
                                     lane < N_CMP_PAD - 1))
        heads_bf = []
        for pr in range(NSA_REP // 2):
            heads_bf.extend(h.astype(BF16) for h in
                            _nsa_heads(q_ref[i * TQ:(i + 1) * TQ, pr * LANES:(pr + 1) * LANES]))
        scores.append([jnp.where(cmask[i], _dot_nt(heads_bf[r], kc), NEG) for r in range(NSA_REP)])
    p_sums, o_cmps = [], []
    for i in range(nq):
        p_sum = jnp.zeros((TQ, N_CMP_PAD), F32)
        o_cmp = []
        for s in scores[i]:
            e = jnp.where(cmask[i], jnp.exp2(s - jnp.max(s, axis=1, keepdims=True)), 0.0)
            p = e / jnp.maximum(jnp.sum(e, axis=1, keepdims=True), 1e-30)
            p_sum = p_sum + p
            o_cmp.append(_dot(p.astype(BF16), vc))
        p_sums.append(p_sum)
        o_cmps.append(o_cmp)
    for i in range(nq):
        o_cmp = o_cmps[i]
        imp_t = _dot_nt(ov_ref[...], p_sums[i], precision=lax.Precision.HIGHEST)
        back = ((i * TQ + _lane_iota((n_slc, TQ))) >> SLC_SHIFT) - jb
        ok = back >= 0
        forced = jnp.logical_or(jb == 0, jnp.logical_and(ok, back < NSA_N_LOCAL))
        score = jnp.where(ok, imp_t + jnp.where(forced, NSA_FORCE_BONUS, 0.0), NEG)
        rank = _rank_rows(score, n_slc)
        keep = jnp.logical_and(ok, rank < min(NSA_TOPN, n_slc))
        bias_ref[i * TQ:(i + 1) * TQ, :] = _bias_lanes(jnp.where(keep, 0.0, NEG), HEAD_DIM).astype(bias_ref.dtype)

        gates = jax.nn.sigmoid(gate_ref[i * TQ:(i + 1) * TQ, :])
        for pr in range(NSA_REP // 2):
            r0, r1 = 2 * pr, 2 * pr + 1
            oc_ref[i * TQ:(i + 1) * TQ, pr * LANES:(pr + 1) * LANES] = jnp.where(
                lo, o_cmp[r0] * _lane_column(gates, 3 * r0), o_cmp[r1] * _lane_column(gates, 3 * r1)
            ).astype(oc_ref.dtype)


def _nsa_attn_kernel(q_ref, ks_ref, vs_ref, kw_ref, vw_ref, bias_ref, gate_ref, oc_ref, o_ref, ksa_s, vt_s):
    seq = ks_ref.shape[0]
    nq = seq // TQ
    lo_rows = _row_iota((LANES, TQ)) < HEAD_DIM
    for b, v_ref in enumerate((vs_ref, vw_ref)):
        for j in range(nq):
            vt_s[b, :, j * TQ:(j + 1) * TQ] = _value_t(_tile(v_ref, j), lo_rows)
    per_tile = TQ // NSA_SLC_BLOCK
    assert NSA_WINDOW == 2 * TQ
    pr = pl.program_id(2)
    lane = _lane_iota((TQ, LANES))
    row = _row_iota((TQ, LANES))
    lo = lane < HEAD_DIM
    diff, causal = _causal_masks()
    for j in range(nq):
        ind = jnp.where(lane - HEAD_DIM == j * per_tile + (row >> SLC_SHIFT), 1.0, 0.0)
        ksa_s[j * TQ:(j + 1) * TQ, :] = jnp.where(lo, _tile(ks_ref, j).astype(F32), ind).astype(BF16)
    outs = {}

    def win_tiles(i):
        t = [(i, causal)]
        if i >= 1:
            t.append((i - 1, None))
        if i >= 2:
            t.append((i - 2, diff < 0))
        return t

    def score(i, h):
        head = _nsa_heads(_tile(q_ref, i))[h]
        q_aug = (head + _tile(bias_ref, i).astype(F32)).astype(BF16)
        q_bf = head.astype(BF16)
        return (_scores([(q_aug, _tile(ksa_s, j), causal if j == i else None) for j in range(i + 1)]),
                _scores([(q_bf, _tile(kw_ref, j), mask) for j, mask in win_tiles(i)]))

    def finish(i, h, state):
        a_slc = _attend(state[0], [_tile_t(vt_s, j, 0) for j in range(i + 1)])
        a_win = _attend(state[1], [_tile_t(vt_s, j, 1) for j, _ in win_tiles(i)])
        gates_t = jax.nn.sigmoid(_tile(gate_ref, i)).T
        c0 = 3 * (2 * pr + h)

        def gate(c):
            return jnp.sum(jnp.where(_row_iota(gates_t.shape) == c, gates_t, 0.0), axis=0, keepdims=True)

        outs[i, h] = (a_slc[:HEAD_DIM] * (gate(c0 + 1) / a_slc[HEAD_DIM:HEAD_DIM + 1])
                      + a_win[:HEAD_DIM] * (gate(c0 + 2) / a_win[HEAD_DIM:HEAD_DIM + 1]))
        if h == 1:
            pair = jnp.concatenate([outs[i, 0], outs[i, 1]], axis=0).T
            o_ref[i * TQ:(i + 1) * TQ, :] = (_tile(oc_ref, i).astype(F32) + pair).astype(o_ref.dtype)

    _run_pipelined([(functools.partial(score, i, h), functools.partial(finish, i, h))
                    for i in reversed(range(nq)) for h in range(2)])


def _nsa_overlap_t(seq):
    n_slc = seq // NSA_SLC_BLOCK
    n = jnp.arange(N_CMP_PAD)
    first = n * NSA_CMP_STRIDE
    last = first + NSA_CMP_LEN - 1
    s_start = jnp.arange(n_slc) * NSA_SLC_BLOCK
    ov = (first[None, :] <= (s_start + NSA_SLC_BLOCK - 1)[:, None]) & (last[None, :] >= s_start[:, None])
    ov = ov & (n[None, :] < (seq - NSA_CMP_LEN) // NSA_CMP_STRIDE + 1)
    return ov.astype(F32)


def _nsa_attention(q, ks, vs, kw, vw, kcmp, vcmp, gates, batch, seq):
    t = batch * seq
    g = NSA_GROUPS
    qb = NSA_REP // 2
    n_slc = seq // NSA_SLC_BLOCK
    grp = lambda: pl.BlockSpec((seq, LANES), lambda b, gg: (b, gg))
    cm = lambda: pl.BlockSpec((N_CMP_PAD, LANES), lambda b, gg: (b, gg))
    o_cmp, bias = pl.pallas_call(
        _nsa_select_kernel,
        grid=(batch, g),
        in_specs=[pl.BlockSpec((seq, qb * LANES), lambda b, gg: (b, gg)), cm(), cm(), grp(),
                  pl.BlockSpec((n_slc, N_CMP_PAD), lambda b, gg: (0, 0))],
        out_specs=[pl.BlockSpec((seq, qb * LANES), lambda b, gg: (b, gg)), grp()],
        out_shape=[jax.ShapeDtypeStruct((t, NSA_HEADS * HEAD_DIM), BF16),
                   jax.ShapeDtypeStruct((t, g * LANES), BF16)],
        compiler_params=_cparams(("parallel", "parallel")),
        name="nsa_select",
    )(q, kcmp, vcmp, gates, _nsa_overlap_t(seq))
    kv = lambda: pl.BlockSpec((seq, LANES), lambda b, gg, pr: (b, gg))
    pair = lambda: pl.BlockSpec((seq, LANES), lambda b, gg, pr: (b, gg * qb + pr))
    return pl.pallas_call(
        _nsa_attn_kernel,
        grid=(batch, g, qb),
        in_specs=[pair(), kv(), kv(), kv(), kv(), kv(), kv(), pair()],
        out_specs=pair(),
        out_shape=jax.ShapeDtypeStruct((t, NSA_HEADS * HEAD_DIM), BF16),
        scratch_shapes=[pltpu.VMEM((seq, LANES), BF16), pltpu.VMEM((2, LANES, seq), BF16)],
        compiler_params=_cparams(("parallel", "parallel", "arbitrary")),
        name="nsa_attention",
    )(q, ks, vs, kw, vw, bias, gates, o_cmp)


def _plain_blocks(n_total, n_rope, n_q):
    return [(0, b, b < n_rope, Q_SCALE if b < n_q else 1.0) for b in range(n_total)]


def _mixer_diff(x, g, w_in, lam_q1, lam_k1, lam_q2, lam_k2, subln_g, cos, sin, batch, seq, lambda_init):
    h = DIFF_HEADS
    blocks = _plain_blocks(3 * h, 2 * h, h)
    (qkv,) = _proj(x, g, w_in.astype(BF16), cos, sin, blocks, [(3 * h, BF16)], "diff_in_proj")
    pad = lambda v: jnp.pad(v.astype(F32), (0, LANES - HEAD_DIM))
    lam_rows = jnp.stack([pad(lam_q1), pad(lam_k1), pad(lam_q2), pad(lam_k2)]
                         + [jnp.zeros((LANES,), F32)] * 4)
    return _diff_attention(qkv, lam_rows, subln_g, batch, seq, lambda_init)


def _mixer_moba(x, g, w_in, cos, sin, batch, seq):
    hp = MOBA_HEADS // 2
    blocks = _plain_blocks(3 * hp, 2 * hp, hp)
    (qkv,) = _proj(x, g, w_in.astype(BF16), cos, sin, blocks, [(3 * hp, BF16)], "moba_in_proj")
    return _moba_attention(qkv, batch, seq)


def _mixer_dil(x, g, w_in, cos, sin, batch, seq):
    hp = DIL_HEADS // 2
    blocks = []
    for gi in range(len(DIL_PATTERNS)):
        for which in range(3):
            for b in range(hp):
                blocks.append((0, len(blocks), which < 2, Q_SCALE if which == 0 else 1.0))
    (qkv,) = _proj(x, g, w_in.astype(BF16), cos, sin, blocks, [(len(blocks), BF16)], "dil_in_proj")
    return _dil_attention(qkv, batch, seq)


def _mixer_nsa(x, g, w_in, ck_pos, ck_w1, ck_w2, cv_pos, cv_w1, cv_w2, cos, sin, batch, seq):
    hq, gr, dk = NSA_HEADS, NSA_GROUPS, HEAD_DIM
    kvw = gr * dk
    o_q = 0
    o_kc, o_vc, o_ks, o_vs, o_kw, o_vw, o_g = (hq * dk + n * kvw for n in range(7))
    n_gate = 3 * hq

    def dup(off):
        cols = w_in[:, off:off + kvw].reshape(D_MODEL, gr, dk)
        return jnp.concatenate([cols, cols], axis=2).reshape(D_MODEL, gr * LANES)

    per_group = n_gate // gr
    gate_cols = jnp.pad(w_in[:, o_g:o_g + n_gate].reshape(D_MODEL, gr, per_group),
                        ((0, 0), (0, 0), (0, LANES - per_group))).reshape(D_MODEL, gr * LANES)
    w = jnp.concatenate([w_in[:, o_q:o_q + hq * dk], dup(o_ks), dup(o_kw), w_in[:, o_kc:o_kc + kvw],
                         dup(o_vs), dup(o_vw), w_in[:, o_vc:o_vc + kvw], gate_cols], axis=1).astype(BF16)
    nqb = hq * dk // LANES
    blocks = [(0, b, True, Q_SCALE) for b in range(nqb)]
    blocks += [(1, b, True, 1.0) for b in range(gr)]
    blocks += [(2, b, True, 1.0) for b in range(gr)]
    blocks += [(3, b, True, 1.0) for b in range(kvw // LANES)]
    blocks += [(4, b, False, 1.0) for b in range(gr)]
    blocks += [(5, b, False, 1.0) for b in range(gr)]
    blocks += [(6, b, False, 1.0) for b in range(kvw // LANES)]
    blocks += [(7, b, False, 1.0) for b in range(gr)]
    outs = [(nqb, BF16), (gr, BF16), (gr, BF16), (kvw // LANES, BF16), (gr, BF16), (gr, BF16),
            (kvw // LANES, BF16), (gr, F32)]
    q, ks, kw, kc, vs, vw, vc, gates = _proj(x, g, w, cos, sin, blocks, outs, "nsa_in_proj")
    kcmp, vcmp = _nsa_compress(kc, vc, ck_pos, ck_w1, ck_w2, cv_pos, cv_w1, cv_w2, batch, seq)
    return _nsa_attention(q, ks, vs, kw, vw, kcmp, vcmp, gates, batch, seq)


def kernel(x, p, positions, norm_gains, ple_norm_g, ple_gate_w, ple_proj_w, ffn_w_up, ffn_conv_w, ffn_conv_b, ffn_w_down, diff_w_in, diff_w_out, diff_lam_q1, diff_lam_k1, diff_lam_q2, diff_lam_k2, diff_subln_g, nsa_w_in, nsa_w_out, nsa_ck_pos, nsa_ck_w1, nsa_ck_w2, nsa_cv_pos, nsa_cv_w1, nsa_cv_w2, moba_w_in, moba_w_out, dil_w_in, dil_w_out):
    batch, seq, d = x.shape
    t = batch * seq
    cos, sin = _rope_tables(positions)
    xt = x.reshape(t, d)
    for i in range(DEPTH):
        kind, j = i % N_MIXERS, i // N_MIXERS
        g0 = norm_gains[i, 0]
        if kind == 0:
            lambda_init = 0.8 - 0.6 * math.exp(-0.3 * i)
            o = _mixer_diff(xt, g0, diff_w_in[j], diff_lam_q1[j], diff_lam_k1[j], diff_lam_q2[j],
                            diff_lam_k2[j], diff_subln_g[j], cos, sin, batch, seq, lambda_init)
            w_out = diff_w_out[j]
        elif kind == 1:
            o = _mixer_nsa(xt, g0, nsa_w_in[j], nsa_ck_pos[j], nsa_ck_w1[j], nsa_ck_w2[j],
                           nsa_cv_pos[j], nsa_cv_w1[j], nsa_cv_w2[j], cos, sin, batch, seq)
            w_out = nsa_w_out[j]
        elif kind == 2:
            o = _mixer_moba(xt, g0, moba_w_in[j], cos, sin, batch, seq)
            w_out = moba_w_out[j]
        else:
            o = _mixer_dil(xt, g0, dil_w_in[j], cos, sin, batch, seq)
            w_out = dil_w_out[j]
        xt = _out_proj(o, w_out.astype(BF16), xt, norm_gains[i, 1], "out_proj_%d" % i)
        xt = _ffn(xt, norm_gains[i, 2], ffn_w_up[i].astype(BF16), ffn_conv_w[i], ffn_conv_b[i],
                  ffn_w_down[i].astype(BF16), norm_gains[i, 3], ple_norm_g[i],
                  ple_gate_w[i].astype(BF16), ple_proj_w[i].astype(BF16), p[i].reshape(t, PLE_DIM),
                  seq, "ffn_%d" % i)
    return xt.reshape(batch, seq, d)
```

```python
import functools
import math

import jax
import jax.numpy as jnp
from jax import lax
from jax.experimental import pallas as pl
from jax.experimental.pallas import tpu as pltpu

F32 = jnp.float32
BF16 = jnp.bfloat16

D_MODEL = 1024
DEPTH = 4
N_MIXERS = 4
HEAD_DIM = 64
ROPE_THETA = 10000.0
NORM_EPS = 1e-6
PLE_DIM = 256
NEG = -1e30

DIFF_HEADS = D_MODEL // (2 * HEAD_DIM)

NSA_HEADS = D_MODEL // HEAD_DIM
NSA_GROUPS = 4
NSA_REP = NSA_HEADS // NSA_GROUPS
NSA_CMP_LEN = 32
NSA_CMP_STRIDE = 16
NSA_CMP_HIDDEN = 128
NSA_SLC_BLOCK = 64
NSA_TOPN = 16
NSA_N_LOCAL = 2
NSA_WINDOW = 512
NSA_FORCE_BONUS = 1e4

MOBA_HEADS = D_MODEL // HEAD_DIM
MOBA_BLOCK = 256
MOBA_TOPK = 3

DIL_PATTERNS = ((128, 1), (512, 4), (2048, 16))
DIL_HEADS = 8

D_FF = 2816
CONV_WIDTH = 3

LANES = 128
MXU_COLS = 256
TQ = 256
LOG2E = 1.4426950408889634
Q_SCALE = (HEAD_DIM ** -0.5) * LOG2E
VMEM_LIMIT = 56 * 1024 * 1024


def _cparams(sem):
    return pltpu.CompilerParams(dimension_semantics=sem, vmem_limit_bytes=VMEM_LIMIT)


def _rms(x, g):
    return x * lax.rsqrt(jnp.mean(x * x, axis=-1, keepdims=True) + NORM_EPS) * g


def _dot(a, b, precision=None):
    return jnp.dot(a, b, preferred_element_type=F32, precision=precision)


def _dot_nt(a, b, precision=None):
    return lax.dot_general(a, b, (((1,), (1,)), ((), ())), preferred_element_type=F32,
                           precision=precision)


def _lane_iota(shape):
    return lax.broadcasted_iota(jnp.int32, shape, 1)


def _row_iota(shape):
    return lax.broadcasted_iota(jnp.int32, shape, 0)


def _rope_kernel(pos_ref, inv_ref, cos_ref, sin_ref):
    ang = pos_ref[...].astype(F32) * inv_ref[...]
    lane = _lane_iota(ang.shape)
    cos_ref[...] = jnp.cos(ang)
    sin_ref[...] = jnp.where((lane & 63) < 32, -jnp.sin(ang), jnp.sin(ang))


def _rope_tables(positions):
    t = positions.size
    inv = ROPE_THETA ** (-jnp.arange(0, HEAD_DIM, 2, dtype=F32) / HEAD_DIM)
    inv = jnp.tile(inv, 4)[None, :]
    tm = 2048
    return pl.pallas_call(
        _rope_kernel,
        grid=(t // tm,),
        in_specs=[pl.BlockSpec((tm, 1), lambda i: (i, 0)),
                  pl.BlockSpec((1, LANES), lambda i: (0, 0))],
        out_specs=[pl.BlockSpec((tm, LANES), lambda i: (i, 0))] * 2,
        out_shape=[jax.ShapeDtypeStruct((t, LANES), F32)] * 2,
        compiler_params=_cparams(("parallel",)),
        name="rope_tables",
    )(positions.reshape(t, 1), inv)


def _proj_kernel(x_ref, g_ref, w_ref, cos_ref, sin_ref, *o_refs, blocks):
    hn = _rms(x_ref[...], g_ref[...]).astype(BF16)
    cos = cos_ref[...]
    sin = sin_ref[...]
    first = (_lane_iota(cos.shape) & 63) < 32
    n_chunks = len(blocks) // 2
    for c in range(n_chunks):
        acc = _dot(hn, w_ref[:, c * MXU_COLS:(c + 1) * MXU_COLS])
        for s in range(2):
            out_idx, out_blk, rope, scale = blocks[2 * c + s]
            if out_idx is None:
                continue
            a = acc[:, s * LANES:(s + 1) * LANES]
            if rope:
                rot = jnp.where(first, pltpu.roll(a, 96, 1), pltpu.roll(a, 32, 1))
                a = a * cos + rot * sin
            if scale != 1.0:
                a = a * scale
            o_ref = o_refs[out_idx]
            o_ref[:, out_blk * LANES:(out_blk + 1) * LANES] = a.astype(o_ref.dtype)


def _proj(x, g, w, cos, sin, blocks, outs, name):
    t = x.shape[0]
    n = w.shape[1]
    assert n == len(blocks) * LANES and len(blocks) % 2 == 0
    tm = 512
    return pl.pallas_call(
        functools.partial(_proj_kernel, blocks=tuple(blocks)),
        grid=(t // tm,),
        in_specs=[pl.BlockSpec((tm, D_MODEL), lambda i: (i, 0)),
                  pl.BlockSpec((1, D_MODEL), lambda i: (0, 0)),
                  pl.BlockSpec((D_MODEL, n), lambda i: (0, 0)),
                  pl.BlockSpec((tm, LANES), lambda i: (i, 0)),
                  pl.BlockSpec((tm, LANES), lambda i: (i, 0))],
        out_specs=[pl.BlockSpec((tm, nb * LANES), lambda i: (i, 0)) for nb, _ in outs],
        out_shape=[jax.ShapeDtypeStruct((t, nb * LANES), dt) for nb, dt in outs],
        compiler_params=_cparams(("parallel",)),
        name=name,
    )(x, g.reshape(1, D_MODEL), w, cos, sin)


def _out_kernel(o_ref, w_ref, x_ref, g_ref, y_ref):
    y = _dot(o_ref[...], w_ref[...])
    y_ref[...] = x_ref[...] + _rms(y, g_ref[...])


def _out_proj(o, w, x, g, name):
    t, k = o.shape
    tm = 512
    return pl.pallas_call(
        _out_kernel,
        grid=(t // tm,),
        in_specs=[pl.BlockSpec((tm, k), lambda i: (i, 0)),
                  pl.BlockSpec((k, D_MODEL), lambda i: (0, 0)),
                  pl.BlockSpec((tm, D_MODEL), lambda i: (i, 0)),
                  pl.BlockSpec((1, D_MODEL), lambda i: (0, 0))],
        out_specs=pl.BlockSpec((tm, D_MODEL), lambda i: (i, 0)),
        out_shape=jax.ShapeDtypeStruct((t, D_MODEL), F32),
        compiler_params=_cparams(("parallel",)),
        name=name,
    )(o, w, x, g.reshape(1, D_MODEL))


FF_CHUNK = 256
FF_TM = 512
FF_HALO = 16


def _gelu_tanh(x):
    k = math.sqrt(2.0 / math.pi)
    h = 0.5 * x
    return h + h * jnp.tanh(x * (k + (k * 0.044715) * (x * x)))


def _ffn_kernel(x_ref, xh_ref, g2_ref, wu_ref, cw_ref, cb_ref, wd_ref, g3_ref, pg_ref, pgw_ref, ppw_ref,
                p_ref, y_ref, hn_s, *, seq):
    i = pl.program_id(0)
    tm = FF_TM
    n_sub = x_ref.shape[0] // tm
    n_c = D_FF // FF_CHUNK

    def rows(s):
        return slice(s * tm, (s + 1) * tm)

    def prologue(s):
        hn_s[s, FF_HALO:, :] = _rms(x_ref[rows(s), :], g2_ref[...]).astype(BF16)
        if s == 0:
            halo = _rms(xh_ref[...], g2_ref[...])
            seq_start = (i * x_ref.shape[0]) % seq == 0
            halo = jnp.where(seq_start, 0.0, halo)
        else:
            halo = _rms(x_ref[s * tm - FF_HALO:s * tm, :], g2_ref[...])
        hn_s[s, :FF_HALO, :] = halo.astype(BF16)

    def conv(u, col):
        cols = slice(col, col + FF_CHUNK)
        out = cb_ref[:, cols] + pltpu.roll(u, 2, 0)[FF_HALO:] * cw_ref[0:1, cols]
        out = out + pltpu.roll(u, 1, 0)[FF_HALO:] * cw_ref[1:2, cols]
        return out + u[FF_HALO:] * cw_ref[2:3, cols]

    def up(s, c):
        hn = hn_s[s]
        gcol, vcol = c * FF_CHUNK, D_FF + c * FF_CHUNK
        return (_dot(hn, wu_ref[:, gcol:gcol + FF_CHUNK]), _dot(hn, wu_ref[:, vcol:vcol + FF_CHUNK]))

    def down(c, act, acc):
        d = _dot(act, wd_ref[c * FF_CHUNK:(c + 1) * FF_CHUNK, :])
        return d if acc is None else acc + d

    def mlp(s, hook):
        acc = None
        pending = None
        u_next = up(s, 0)
        for c in range(n_c):
            ug, uv = u_next
            if c + 1 < n_c:
                u_next = up(s, c + 1)
            if c == 2 and hook is not None:
                hook()
            act = (_gelu_tanh(conv(ug, c * FF_CHUNK)) * conv(uv, D_FF + c * FF_CHUNK)).astype(BF16)
            if pending is not None:
                acc = down(c - 1, pending, acc)
            pending = act
        return down(n_c - 1, pending, acc)

    for s in range(n_sub):
        prologue(s)
    hook = None
    for s in range(n_sub):
        acc = mlp(s, hook)
        x2 = x_ref[rows(s), :] + _rms(acc, g3_ref[...])
        hq = _rms(x2, pg_ref[...]).astype(BF16)
        state = {}

        def gate_matmuls(s=s, hq=hq, state=state):
            state["g"] = _dot(hq, pgw_ref[...])
            state["p"] = _dot(p_ref[rows(s), :].astype(BF16), ppw_ref[...])

        def finish(s=s, x2=x2, state=state):
            y_ref[rows(s), :] = x2 + jax.nn.sigmoid(state["g"]) * state["p"]

        if s + 1 < n_sub:
            def hook(gate_matmuls=gate_matmuls, finish=finish):
                gate_matmuls()
                finish()
        else:
            gate_matmuls()
            finish()


FF_SUB = 1


def _ffn(x, g2, w_up, conv_w, conv_b, w_down, g3, ple_g, ple_gate_w, ple_proj_w, p, seq, name):
    t = x.shape[0]
    tm = FF_TM * FF_SUB
    assert seq % tm == 0
    hb = tm // FF_HALO
    row = lambda v: v.reshape(1, -1)
    const = lambda i: (0, 0)
    resident = lambda shape: pl.BlockSpec(shape, const, pipeline_mode=pl.Buffered(1))
    return pl.pallas_call(
        functools.partial(_ffn_kernel, seq=seq),
        grid=(t // tm,),
        in_specs=[pl.BlockSpec((tm, D_MODEL), lambda i: (i, 0)),
                  pl.BlockSpec((FF_HALO, D_MODEL), lambda i: (jnp.maximum(i * hb - 1, 0), 0)),
                  pl.BlockSpec((1, D_MODEL), const),
                  resident((D_MODEL, 2 * D_FF)),
                  pl.BlockSpec((CONV_WIDTH, 2 * D_FF), const),
                  pl.BlockSpec((1, 2 * D_FF), const),
                  resident((D_FF, D_MODEL)),
                  pl.BlockSpec((1, D_MODEL), const),
                  pl.BlockSpec((1, D_MODEL), const),
                  resident((D_MODEL, D_MODEL)),
                  resident((PLE_DIM, D_MODEL)),
                  pl.BlockSpec((tm, PLE_DIM), lambda i: (i, 0))],
        out_specs=pl.BlockSpec((tm, D_MODEL), lambda i: (i, 0)),
        out_shape=jax.ShapeDtypeStruct((t, D_MODEL), F32),
        scratch_shapes=[pltpu.VMEM((FF_SUB, FF_TM + FF_HALO, D_MODEL), BF16)],
        compiler_params=_cparams(("parallel",)),
        name=name,
    )(x, x, row(g2), w_up, conv_w, row(conv_b), w_down, row(g3), row(ple_g), ple_gate_w, ple_proj_w, p)


def _lo_half(shape):
    return _lane_iota(shape) < HEAD_DIM


def _split_heads(q):
    lo = _lo_half(q.shape)
    qf = q.astype(F32)
    return jnp.where(lo, qf, 0.0).astype(q.dtype), jnp.where(lo, 0.0, qf).astype(q.dtype)


def _tile(ref, j):
    return ref[j * TQ:(j + 1) * TQ, :]


def _scores(q_op, k_ref, first, masks):
    s_list = []
    for t, mask in enumerate(masks):
        s = _dot_nt(_tile(k_ref, first + t), q_op)
        s_list.append(s if mask is None else jnp.where(mask, s, NEG))
    return s_list


def _attend(s_list, values_t):
    mt = None
    for s in s_list:
        mt = s if mt is None else jnp.maximum(mt, s)
    m = jnp.max(mt, axis=0, keepdims=True)
    acc = None
    for s, v_t in zip(s_list, values_t):
        d = _dot(v_t, jnp.exp2(s - m).astype(BF16))
        acc = d if acc is None else acc + d
    return acc


def _value_t(v_tile, keep=None):
    v_t = v_tile.astype(F32).T
    return (v_t if keep is None else jnp.where(keep, v_t, 1.0)).astype(BF16)


def _store_pair_values(v_ref, vt_s, j, *idx):
    lo_rows = _row_iota((LANES, TQ)) < HEAD_DIM
    v_t = _tile(v_ref, j).astype(F32).T
    cols = slice(j * TQ, (j + 1) * TQ)
    vt_s[idx + (0, slice(None), cols)] = jnp.where(lo_rows, v_t, 1.0).astype(BF16)
    vt_s[idx + (1, slice(None), cols)] = jnp.where(lo_rows, 1.0, v_t).astype(BF16)


def _pair_output(acc0, acc1):
    out0 = acc0[:HEAD_DIM] / acc0[HEAD_DIM:HEAD_DIM + 1]
    out1 = acc1[HEAD_DIM:] / acc1[0:1]
    return jnp.concatenate([out0, out1], axis=0).T


ONES_ROWS = 16


def _tile_t(vt_s, j, *idx):
    return vt_s[idx + (slice(None), slice(j * TQ, (j + 1) * TQ))]


def _run_pipelined(stages, lead=None):
    n = len(stages)
    if lead is None:
        for prep_fn, _, _ in stages:
            if prep_fn is not None:
                prep_fn()
        lead = 0
    states = {}
    for step in range(n + lead + 1):
        if lead and step < n and stages[step][0] is not None:
            stages[step][0]()
        if lead <= step < n + lead:
            states[step - lead] = stages[step - lead][1]()
        if step >= lead + 1:
            stages[step - lead - 1][2](states.pop(step - lead - 1))


PREP_LEAD = 3


def _causal_masks():
    diff = _lane_iota((TQ, TQ)) - _row_iota((TQ, TQ))
    return diff, diff >= 0


def _rank_rows(score, n_valid):
    ridx = _row_iota(score.shape)
    cnt = jnp.zeros(score.shape, F32)
    for j in range(n_valid):
        other = score[j:j + 1, :]
        cnt = cnt + jnp.where(other > score, 1.0,
                              jnp.where(other == score, jnp.where(ridx > j, 1.0, 0.0), 0.0))
    return cnt


def _bias_lanes(bias_t, first_lane):
    r, tq = bias_t.shape
    parts = []
    if first_lane:
        parts.append(jnp.zeros((first_lane, tq), F32))
    parts.append(bias_t)
    if LANES - first_lane - r:
        parts.append(jnp.zeros((LANES - first_lane - r, tq), F32))
    return jnp.concatenate(parts, axis=0).T


def _diff_kernel(q_ref, k_ref, v_ref, lam_ref, sg_ref, o_ref, vt_s, *, lambda_init):
    nq = q_ref.shape[0] // TQ
    _, causal = _causal_masks()
    lamv = lam_ref[...]
    lam = (jnp.exp(jnp.sum(lamv[0:1] * lamv[1:2], axis=1, keepdims=True))
           - jnp.exp(jnp.sum(lamv[2:3] * lamv[3:4], axis=1, keepdims=True)) + lambda_init)
    sg = sg_ref[...]

    def prep(i):
        vt_s[:LANES, i * TQ:(i + 1) * TQ] = _value_t(_tile(v_ref, i))
        vt_s[LANES:, i * TQ:(i + 1) * TQ] = jnp.ones((ONES_ROWS, TQ), BF16)

    def score(i):
        q0, q1 = _split_heads(_tile(q_ref, i))
        masks = [None] * i + [causal]
        return _scores(q0, k_ref, 0, masks), _scores(q1, k_ref, 0, masks)

    def finish(i, state):
        v_tiles = [_tile_t(vt_s, j) for j in range(i + 1)]
        a0 = _attend(state[0], v_tiles)
        a1 = _attend(state[1], v_tiles)
        o_t = a0[:LANES] / a0[LANES:LANES + 1] - lam * (a1[:LANES] / a1[LANES:LANES + 1])
        o_ref[i * TQ:(i + 1) * TQ, :] = (_rms(o_t.T, sg) * (1.0 - lambda_init)).astype(o_ref.dtype)

    _run_pipelined([(functools.partial(prep, i), functools.partial(score, i), functools.partial(finish, i))
                    for i in range(nq)], PREP_LEAD)


def _diff_attention(qkv, lam_rows, subln_g, batch, seq, lambda_init):
    t = batch * seq
    h = DIFF_HEADS
    return pl.pallas_call(
        functools.partial(_diff_kernel, lambda_init=lambda_init),
        grid=(batch, h),
        in_specs=[pl.BlockSpec((seq, LANES), lambda b, hh: (b, hh)),
                  pl.BlockSpec((seq, LANES), lambda b, hh: (b, h + hh)),
                  pl.BlockSpec((seq, LANES), lambda b, hh: (b, 2 * h + hh)),
                  pl.BlockSpec((8, LANES), lambda b, hh: (0, 0)),
                  pl.BlockSpec((1, LANES), lambda b, hh: (0, 0))],
        out_specs=pl.BlockSpec((seq, LANES), lambda b, hh: (b, hh)),
        out_shape=jax.ShapeDtypeStruct((t, h * LANES), BF16),
        scratch_shapes=[pltpu.VMEM((LANES + ONES_ROWS, seq), BF16)],
        compiler_params=_cparams(("parallel", "parallel")),
        name="diff_attention",
    )(qkv, qkv, qkv, lam_rows, subln_g.reshape(1, LANES))


def _moba_kernel(q_ref, k_ref, v_ref, o_ref, ka_s, vt_s):
    seq = k_ref.shape[0]
    nb = seq // MOBA_BLOCK
    lane = _lane_iota((TQ, LANES))
    lo = lane < HEAD_DIM
    _, causal = _causal_masks()
    km = jnp.concatenate([jnp.mean(_tile(k_ref, j).astype(F32), axis=0, keepdims=True) for j in range(nb)],
                         axis=0)
    q_aug = {}
    outs = {}

    def prep(i, a):
        if a == 0:
            _store_pair_values(v_ref, vt_s, i)
            kf = _tile(k_ref, i).astype(F32)
            ka_s[0, i * TQ:(i + 1) * TQ, :] = jnp.where(lo, kf, jnp.where(lane == HEAD_DIM + i, 1.0, 0.0)
                                                        ).astype(BF16)
            ka_s[1, i * TQ:(i + 1) * TQ, :] = jnp.where(lo, jnp.where(lane == i, 1.0, 0.0), kf).astype(BF16)
        head = _split_heads(_tile(q_ref, i))[a]
        q_aug[i, a] = head
        if i > 0:
            gate_t = _dot_nt(km, head.astype(F32), precision=lax.Precision.HIGHEST)
            rank = _rank_rows(gate_t, i)
            drop = jnp.logical_and(_row_iota(gate_t.shape) < i, rank >= MOBA_TOPK)
            bias = _bias_lanes(jnp.where(drop, NEG, 0.0), HEAD_DIM * (1 - a))
            q_aug[i, a] = (head.astype(F32) + bias).astype(BF16)

    def score(i, a):
        return _scores(q_aug[i, a], ka_s.at[a], 0, [None] * i + [causal])

    def finish(i, a, s_list):
        outs[i, a] = _attend(s_list, [_tile_t(vt_s, j, a) for j in range(i + 1)])
        if a == 1:
            o_ref[i * TQ:(i + 1) * TQ, :] = _pair_output(outs[i, 0], outs[i, 1]).astype(o_ref.dtype)

    _run_pipelined([(functools.partial(prep, i, a), functools.partial(score, i, a),
                     functools.partial(finish, i, a)) for i in range(nb) for a in range(2)], PREP_LEAD)


def _moba_attention(qkv, batch, seq):
    assert MOBA_BLOCK == TQ and seq % MOBA_BLOCK == 0 and seq // MOBA_BLOCK <= HEAD_DIM
    t = batch * seq
    hp = MOBA_HEADS // 2
    return pl.pallas_call(
        _moba_kernel,
        grid=(batch, hp),
        in_specs=[pl.BlockSpec((seq, LANES), lambda b, hh: (b, hh)),
                  pl.BlockSpec((seq, LANES), lambda b, hh: (b, hp + hh)),
                  pl.BlockSpec((seq, LANES), lambda b, hh: (b, 2 * hp + hh))],
        out_specs=pl.BlockSpec((seq, LANES), lambda b, hh: (b, hh)),
        out_shape=jax.ShapeDtypeStruct((t, hp * LANES), BF16),
        scratch_shapes=[pltpu.VMEM((2, seq, LANES), BF16), pltpu.VMEM((2, LANES, seq), BF16)],
        compiler_params=_cparams(("parallel", "parallel")),
        name="moba_attention",
    )(qkv, qkv, qkv)


def _dil_kernel(q0_ref, q1_ref, q2_ref, k0_ref, k1_ref, k2_ref, v0_ref, v1_ref, v2_ref, o_ref, vt_s):
    seq = k0_ref.shape[0]
    nq = seq // TQ
    (w0, _), (w1, d1), (w2, d2) = DIL_PATTERNS
    assert w0 < TQ and w1 == 2 * TQ and w2 >= seq
    diff, causal = _causal_masks()
    on1 = (diff & (d1 - 1)) == 0
    on2 = (diff & (d2 - 1)) == 0
    m0_diag = jnp.logical_and(causal, diff <= w0)
    m0_prev = diff + TQ <= w0
    m1_diag = jnp.logical_and(causal, on1)
    m1_prev2 = jnp.logical_and(on1, diff <= 0)
    m2_diag = jnp.logical_and(causal, on2)
    lo = _lo_half((TQ, LANES))
    k_refs = (k0_ref, k1_ref, k2_ref)
    q_refs = (q0_ref, q1_ref, q2_ref)
    outs = {}

    def runs(i):
        g0 = ([m0_prev] if i >= 1 else []) + [m0_diag]
        g1 = ([m1_prev2] if i >= 2 else []) + ([on1] if i >= 1 else []) + [m1_diag]
        g2 = [on2] * i + [m2_diag]
        return [(g, i + 1 - len(masks), masks) for g, masks in enumerate((g0, g1, g2))]

    def score(i, a):
        s_list = []
        for g, first, masks in runs(i):
            s_list += _scores(_split_heads(_tile(q_refs[g], i))[a], k_refs[g], first, masks)
        return s_list

    def finish(i, a, s_list):
        v_tiles = [_tile_t(vt_s, first + t, g, a) for g, first, masks in runs(i) for t in range(len(masks))]
        outs[i, a] = _attend(s_list, v_tiles)
        if a == 1:
            o_ref[i * TQ:(i + 1) * TQ, :] = _pair_output(outs[i, 0], outs[i, 1]).astype(o_ref.dtype)

    def prep(i, a):
        if a == 0:
            for g, v_ref in enumerate((v0_ref, v1_ref, v2_ref)):
                _store_pair_values(v_ref, vt_s, i, g)

    _run_pipelined([(functools.partial(prep, i, a), functools.partial(score, i, a),
                     functools.partial(finish, i, a)) for i in reversed(range(nq)) for a in range(2)])


def _dil_attention(qkv, batch, seq):
    t = batch * seq
    hp = DIL_HEADS // 2
    per_g = 3 * hp

    def spec(g, which):
        return pl.BlockSpec((seq, LANES), lambda b, hh: (b, g * per_g + which * hp + hh))

    return pl.pallas_call(
        _dil_kernel,
        grid=(batch, hp),
        in_specs=[spec(0, 0), spec(1, 0), spec(2, 0), spec(0, 1), spec(1, 1), spec(2, 1),
                  spec(0, 2), spec(1, 2), spec(2, 2)],
        out_specs=pl.BlockSpec((seq, LANES), lambda b, hh: (b, hh)),
        out_shape=jax.ShapeDtypeStruct((t, hp * LANES), BF16),
        scratch_shapes=[pltpu.VMEM((len(DIL_PATTERNS), 2, LANES, seq), BF16)],
        compiler_params=_cparams(("parallel", "parallel")),
        name="dilated_attention",
    )(*([qkv] * 9))


N_CMP_PAD = 128


def _cmp_kernel(zk_ref, zv_ref, w1k_ref, w1v_ref, pk_ref, pv_ref, w1fk_ref, w1fv_ref, w2k_ref, w2v_ref,
                ko_ref, vo_ref):
    def one(z_ref, w1_ref, pos_ref, w1f_ref, w2_ref, o_ref):
        z = z_ref[...]
        a = _dot(z, w1_ref[0])
        b = _dot(z, w1_ref[1])
        b_next = jnp.concatenate([b[1:], jnp.zeros_like(b[:1])], axis=0)
        const = _dot(pos_ref[...], w1f_ref[...])[0:1]
        hid = a + b_next + jnp.concatenate([const] * NSA_GROUPS, axis=1)
        o_ref[...] = _dot(jax.nn.gelu(hid, approximate=True).astype(BF16), w2_ref[...]).astype(o_ref.dtype)

    one(zk_ref, w1k_ref, pk_ref, w1fk_ref, w2k_ref, ko_ref)
    one(zv_ref, w1v_ref, pv_ref, w1fv_ref, w2v_ref, vo_ref)


def _nsa_compress(kc, vc, ck_pos, ck_w1, ck_w2, cv_pos, cv_w1, cv_w2, batch, seq):
    g, dk, f = NSA_GROUPS, HEAD_DIM, NSA_CMP_HIDDEN
    half = NSA_CMP_LEN // 2
    assert NSA_CMP_STRIDE == half and seq // half == N_CMP_PAD
    zk = kc.reshape(batch * N_CMP_PAD, half * g * dk)
    zv = vc.reshape(batch * N_CMP_PAD, half * g * dk)

    def expand_w1(w1):
        w = w1.reshape(2, half, dk, f)
        eye = jnp.eye(g, dtype=w1.dtype)
        big = jnp.einsum('acdf,gh->acgdhf', w, eye)
        return big.reshape(2, half * g * dk, g * f).astype(BF16)

    def expand_w2(w2):
        dup = jnp.concatenate([w2, w2], axis=1)
        eye = jnp.eye(g, dtype=w2.dtype)
        return jnp.einsum('fd,gh->gfhd', dup, eye).reshape(g * f, g * LANES).astype(BF16)

    def pos_rows(pos):
        flat = pos.reshape(1, NSA_CMP_LEN * dk)
        return jnp.concatenate([flat, jnp.zeros((7, NSA_CMP_LEN * dk), pos.dtype)], axis=0).astype(BF16)

    kd = half * g * dk
    const2 = lambda b: (0, 0)
    const3 = lambda b: (0, 0, 0)
    return pl.pallas_call(
        _cmp_kernel,
        grid=(batch,),
        in_specs=[pl.BlockSpec((N_CMP_PAD, kd), lambda b: (b, 0)),
                  pl.BlockSpec((N_CMP_PAD, kd), lambda b: (b, 0)),
                  pl.BlockSpec((2, kd, g * f), const3),
                  pl.BlockSpec((2, kd, g * f), const3),
                  pl.BlockSpec((8, NSA_CMP_LEN * dk), const2),
                  pl.BlockSpec((8, NSA_CMP_LEN * dk), const2),
                  pl.BlockSpec((NSA_CMP_LEN * dk, f), const2),
                  pl.BlockSpec((NSA_CMP_LEN * dk, f), const2),
                  pl.BlockSpec((g * f, g * LANES), const2),
                  pl.BlockSpec((g * f, g * LANES), const2)],
        out_specs=[pl.BlockSpec((N_CMP_PAD, g * LANES), lambda b: (b, 0))] * 2,
        out_shape=[jax.ShapeDtypeStruct((batch * N_CMP_PAD, g * LANES), BF16)] * 2,
        compiler_params=_cparams(("parallel",)),
        name="nsa_compress",
    )(zk, zv, expand_w1(ck_w1), expand_w1(cv_w1), pos_rows(ck_pos), pos_rows(cv_pos),
      ck_w1.reshape(NSA_CMP_LEN * dk, f).astype(BF16), cv_w1.reshape(NSA_CMP_LEN * dk, f).astype(BF16),
      expand_w2(ck_w2), expand_w2(cv_w2))


SLC_SHIFT = NSA_SLC_BLOCK.bit_length() - 1


def _nsa_heads(q_pair):
    lo = _lo_half(q_pair.shape)
    qf = q_pair.astype(F32)
    return jnp.where(lo, qf, 0.0), jnp.where(lo, pltpu.roll(qf, HEAD_DIM, 1), 0.0)


def _lane_column(x, c):
    return jnp.sum(jnp.where(_lane_iota(x.shape) == c, x, 0.0), axis=1, keepdims=True)


def _nsa_select_kernel(q_ref, kc_ref, vc_ref, gate_ref, ov_ref, oc_ref, bias_ref):
    seq = q_ref.shape[0]
    nq = seq // TQ
    n_slc = seq // NSA_SLC_BLOCK
    assert 1 << SLC_SHIFT == NSA_SLC_BLOCK and HEAD_DIM + n_slc <= LANES
    n_idx = _row_iota((N_CMP_PAD, TQ))
    t_idx = _lane_iota((N_CMP_PAD, TQ))
    kc = kc_ref[...]
    vc_t = _value_t(vc_ref[...])
    jb = _row_iota((n_slc, TQ))
    cmask, scores = [], []
    for i in range(nq):
        cmask.append(jnp.logical_and(n_idx * NSA_CMP_STRIDE + (NSA_CMP_LEN - 1) <= i * TQ + t_idx,
                                     n_idx < N_CMP_PAD - 1))
        heads_bf = []
        for pr in range(NSA_REP // 2):
            heads_bf.extend(h.astype(BF16) for h in
                            _nsa_heads(q_ref[i * TQ:(i + 1) * TQ, pr * LANES:(pr + 1) * LANES]))
        scores.append([jnp.where(cmask[i], _dot_nt(kc, heads_bf[r]), NEG) for r in range(NSA_REP)])
    p_sums, o_cmps = [], []
    for i in range(nq):
        p_sum = jnp.zeros((N_CMP_PAD, TQ), F32)
        o_cmp = []
        for s in scores[i]:
            e = jnp.where(cmask[i], jnp.exp2(s - jnp.max(s, axis=0, keepdims=True)), 0.0)
            p = e / jnp.maximum(jnp.sum(e, axis=0, keepdims=True), 1e-30)
            p_sum = p_sum + p
            o_cmp.append(_dot(vc_t, p.astype(BF16))[:HEAD_DIM])
        p_sums.append(p_sum)
        o_cmps.append(o_cmp)
    for i in range(nq):
        o_cmp = o_cmps[i]
        imp_t = _dot(ov_ref[...], p_sums[i], precision=lax.Precision.HIGHEST)
        back = ((i * TQ + _lane_iota((n_slc, TQ))) >> SLC_SHIFT) - jb
        ok = back >= 0
        forced = jnp.logical_or(jb == 0, jnp.logical_and(ok, back < NSA_N_LOCAL))
        score = jnp.where(ok, imp_t + jnp.where(forced, NSA_FORCE_BONUS, 0.0), NEG)
        rank = _rank_rows(score, n_slc)
        keep = jnp.logical_and(ok, rank < min(NSA_TOPN, n_slc))
        bias_ref[i * TQ:(i + 1) * TQ, :] = _bias_lanes(jnp.where(keep, 0.0, NEG), HEAD_DIM).astype(bias_ref.dtype)

        gates_t = jax.nn.sigmoid(gate_ref[i * TQ:(i + 1) * TQ, :]).T
        for pr in range(NSA_REP // 2):
            pair = jnp.concatenate([o_cmp[r] * gates_t[3 * r:3 * r + 1] for r in (2 * pr, 2 * pr + 1)], axis=0)
            oc_ref[i * TQ:(i + 1) * TQ, pr * LANES:(pr + 1) * LANES] = pair.T.astype(oc_ref.dtype)


def _nsa_attn_kernel(q_ref, ks_ref, vs_ref, kw_ref, vw_ref, bias_ref, gate_ref, oc_ref, o_ref, ksa_s, vt_s):
    seq = ks_ref.shape[0]
    nq = seq // TQ
    lo_rows = _row_iota((LANES, TQ)) < HEAD_DIM
    per_tile = TQ // NSA_SLC_BLOCK
    assert NSA_WINDOW == 2 * TQ
    pr = pl.program_id(2)
    lane = _lane_iota((TQ, LANES))
    row = _row_iota((TQ, LANES))
    lo = lane < HEAD_DIM
    diff, causal = _causal_masks()
    outs = {}

    def prep(i, h):
        if h == 0:
            for b, v_ref in enumerate((vs_ref, vw_ref)):
                vt_s[b, :, i * TQ:(i + 1) * TQ] = _value_t(_tile(v_ref, i), lo_rows)
            ind = jnp.where(lane - HEAD_DIM == i * per_tile + (row >> SLC_SHIFT), 1.0, 0.0)
            ksa_s[i * TQ:(i + 1) * TQ, :] = jnp.where(lo, _tile(ks_ref, i).astype(F32), ind).astype(BF16)

    def win_masks(i):
        return ([diff < 0] if i >= 2 else []) + ([None] if i >= 1 else []) + [causal]

    def score(i, h):
        head = _nsa_heads(_tile(q_ref, i))[h]
        q_aug = (head + _tile(bias_ref, i).astype(F32)).astype(BF16)
        wm = win_masks(i)
        return (_scores(q_aug, ksa_s, 0, [None] * i + [causal]),
                _scores(head.astype(BF16), kw_ref, i + 1 - len(wm), wm))

    def finish(i, h, state):
        n_win = len(win_masks(i))
        a_slc = _attend(state[0], [_tile_t(vt_s, j, 0) for j in range(i + 1)])
        a_win = _attend(state[1], [_tile_t(vt_s, j, 1) for j in range(i + 1 - n_win, i + 1)])
        gates_t = jax.nn.sigmoid(_tile(gate_ref, i)).T
        c0 = 3 * (2 * pr + h)

        def gate(c):
            return jnp.sum(jnp.where(_row_iota(gates_t.shape) == c, gates_t, 0.0), axis=0, keepdims=True)

        outs[i, h] = (a_slc[:HEAD_DIM] * (gate(c0 + 1) / a_slc[HEAD_DIM:HEAD_DIM + 1])
                      + a_win[:HEAD_DIM] * (gate(c0 + 2) / a_win[HEAD_DIM:HEAD_DIM + 1]))
        if h == 1:
            pair = jnp.concatenate([outs[i, 0], outs[i, 1]], axis=0).T
            o_ref[i * TQ:(i + 1) * TQ, :] = (_tile(oc_ref, i).astype(F32) + pair).astype(o_ref.dtype)

    _run_pipelined([(functools.partial(prep, i, h), functools.partial(score, i, h),
                     functools.partial(finish, i, h)) for i in range(nq) for h in range(2)], PREP_LEAD)


def _nsa_overlap_t(seq):
    n_slc = seq // NSA_SLC_BLOCK
    n = jnp.arange(N_CMP_PAD)
    first = n * NSA_CMP_STRIDE
    last = first + NSA_CMP_LEN - 1
    s_start = jnp.arange(n_slc) * NSA_SLC_BLOCK
    ov = (first[None, :] <= (s_start + NSA_SLC_BLOCK - 1)[:, None]) & (last[None, :] >= s_start[:, None])
    ov = ov & (n[None, :] < (seq - NSA_CMP_LEN) // NSA_CMP_STRIDE + 1)
    return ov.astype(F32)


def _nsa_attention(q, ks, vs, kw, vw, kcmp, vcmp, gates, batch, seq):
    t = batch * seq
    g = NSA_GROUPS
    qb = NSA_REP // 2
    n_slc = seq // NSA_SLC_BLOCK
    grp = lambda: pl.BlockSpec((seq, LANES), lambda b, gg: (b, gg))
    cm = lambda: pl.BlockSpec((N_CMP_PAD, LANES), lambda b, gg: (b, gg))
    o_cmp, bias = pl.pallas_call(
        _nsa_select_kernel,
        grid=(batch, g),
        in_specs=[pl.BlockSpec((seq, qb * LANES), lambda b, gg: (b, gg)), cm(), cm(), grp(),
                  pl.BlockSpec((n_slc, N_CMP_PAD), lambda b, gg: (0, 0))],
        out_specs=[pl.BlockSpec((seq, qb * LANES), lambda b, gg: (b, gg)), grp()],
        out_shape=[jax.ShapeDtypeStruct((t, NSA_HEADS * HEAD_DIM), BF16),
                   jax.ShapeDtypeStruct((t, g * LANES), BF16)],
        compiler_params=_cparams(("parallel", "parallel")),
        name="nsa_select",
    )(q, kcmp, vcmp, gates, _nsa_overlap_t(seq))
    kv = lambda: pl.BlockSpec((seq, LANES), lambda b, gg, pr: (b, gg))
    pair = lambda: pl.BlockSpec((seq, LANES), lambda b, gg, pr: (b, gg * qb + pr))
    return pl.pallas_call(
        _nsa_attn_kernel,
        grid=(batch, g, qb),
        in_specs=[pair(), kv(), kv(), kv(), kv(), kv(), kv(), pair()],
        out_specs=pair(),
        out_shape=jax.ShapeDtypeStruct((t, NSA_HEADS * HEAD_DIM), BF16),
        scratch_shapes=[pltpu.VMEM((seq, LANES), BF16), pltpu.VMEM((2, LANES, seq), BF16)],
        compiler_params=_cparams(("parallel", "parallel", "arbitrary")),
        name="nsa_attention",
    )(q, ks, vs, kw, vw, bias, gates, o_cmp)


def _plain_blocks(n_total, n_rope, n_q):
    return [(0, b, b < n_rope, Q_SCALE if b < n_q else 1.0) for b in range(n_total)]


def _mixer_diff(x, g, w_in, lam_q1, lam_k1, lam_q2, lam_k2, subln_g, cos, sin, batch, seq, lambda_init):
    h = DIFF_HEADS
    blocks = _plain_blocks(3 * h, 2 * h, h)
    (qkv,) = _proj(x, g, w_in.astype(BF16), cos, sin, blocks, [(3 * h, BF16)], "diff_in_proj")
    pad = lambda v: jnp.pad(v.astype(F32), (0, LANES - HEAD_DIM))
    lam_rows = jnp.stack([pad(lam_q1), pad(lam_k1), pad(lam_q2), pad(lam_k2)]
                         + [jnp.zeros((LANES,), F32)] * 4)
    return _diff_attention(qkv, lam_rows, subln_g, batch, seq, lambda_init)


def _mixer_moba(x, g, w_in, cos, sin, batch, seq):
    hp = MOBA_HEADS // 2
    blocks = _plain_blocks(3 * hp, 2 * hp, hp)
    (qkv,) = _proj(x, g, w_in.astype(BF16), cos, sin, blocks, [(3 * hp, BF16)], "moba_in_proj")
    return _moba_attention(qkv, batch, seq)


def _mixer_dil(x, g, w_in, cos, sin, batch, seq):
    hp = DIL_HEADS // 2
    blocks = []
    for gi in range(len(DIL_PATTERNS)):
        for which in range(3):
            for b in range(hp):
                blocks.append((0, len(blocks), which < 2, Q_SCALE if which == 0 else 1.0))
    (qkv,) = _proj(x, g, w_in.astype(BF16), cos, sin, blocks, [(len(blocks), BF16)], "dil_in_proj")
    return _dil_attention(qkv, batch, seq)


def _mixer_nsa(x, g, w_in, ck_pos, ck_w1, ck_w2, cv_pos, cv_w1, cv_w2, cos, sin, batch, seq):
    hq, gr, dk = NSA_HEADS, NSA_GROUPS, HEAD_DIM
    kvw = gr * dk
    o_q = 0
    o_kc, o_vc, o_ks, o_vs, o_kw, o_vw, o_g = (hq * dk + n * kvw for n in range(7))
    n_gate = 3 * hq

    def dup(off):
        cols = w_in[:, off:off + kvw].reshape(D_MODEL, gr, dk)
        return jnp.concatenate([cols, cols], axis=2).reshape(D_MODEL, gr * LANES)

    per_group = n_gate // gr
    gate_cols = jnp.pad(w_in[:, o_g:o_g + n_gate].reshape(D_MODEL, gr, per_group),
                        ((0, 0), (0, 0), (0, LANES - per_group))).reshape(D_MODEL, gr * LANES)
    w = jnp.concatenate([w_in[:, o_q:o_q + hq * dk], dup(o_ks), dup(o_kw), w_in[:, o_kc:o_kc + kvw],
                         dup(o_vs), dup(o_vw), w_in[:, o_vc:o_vc + kvw], gate_cols], axis=1).astype(BF16)
    nqb = hq * dk // LANES
    blocks = [(0, b, True, Q_SCALE) for b in range(nqb)]
    blocks += [(1, b, True, 1.0) for b in range(gr)]
    blocks += [(2, b, True, 1.0) for b in range(gr)]
    blocks += [(3, b, True, 1.0) for b in range(kvw // LANES)]
    blocks += [(4, b, False, 1.0) for b in range(gr)]
    blocks += [(5, b, False, 1.0) for b in range(gr)]
    blocks += [(6, b, False, 1.0) for b in range(kvw // LANES)]
    blocks += [(7, b, False, 1.0) for b in range(gr)]
    outs = [(nqb, BF16), (gr, BF16), (gr, BF16), (kvw // LANES, BF16), (gr, BF16), (gr, BF16),
            (kvw // LANES, BF16), (gr, F32)]
    q, ks, kw, kc, vs, vw, vc, gates = _proj(x, g, w, cos, sin, blocks, outs, "nsa_in_proj")
    kcmp, vcmp = _nsa_compress(kc, vc, ck_pos, ck_w1, ck_w2, cv_pos, cv_w1, cv_w2, batch, seq)
    return _nsa_attention(q, ks, vs, kw, vw, kcmp, vcmp, gates, batch, seq)


def kernel(x, p, positions, norm_gains, ple_norm_g, ple_gate_w, ple_proj_w, ffn_w_up, ffn_conv_w, ffn_conv_b, ffn_w_down, diff_w_in, diff_w_out, diff_lam_q1, diff_lam_k1, diff_lam_q2, diff_lam_k2, diff_subln_g, nsa_w_in, nsa_w_out, nsa_ck_pos, nsa_ck_w1, nsa_ck_w2, nsa_cv_pos, nsa_cv_w1, nsa_cv_w2, moba_w_in, moba_w_out, dil_w_in, dil_w_out):
    batch, seq, d = x.shape
    t = batch * seq
    cos, sin = _rope_tables(positions)
    xt = x.reshape(t, d)
    for i in range(DEPTH):
        kind, j = i % N_MIXERS, i // N_MIXERS
        g0 = norm_gains[i, 0]
        if kind == 0:
            lambda_init = 0.8 - 0.6 * math.exp(-0.3 * i)
            o = _mixer_diff(xt, g0, diff_w_in[j], diff_lam_q1[j], diff_lam_k1[j], diff_lam_q2[j],
                            diff_lam_k2[j], diff_subln_g[j], cos, sin, batch, seq, lambda_init)
            w_out = diff_w_out[j]
        elif kind == 1:
            o = _mixer_nsa(xt, g0, nsa_w_in[j], nsa_ck_pos[j], nsa_ck_w1[j], nsa_ck_w2[j],
                           nsa_cv_pos[j], nsa_cv_w1[j], nsa_cv_w2[j], cos, sin, batch, seq)
            w_out = nsa_w_out[j]
        elif kind == 2:
            o = _mixer_moba(xt, g0, moba_w_in[j], cos, sin, batch, seq)
            w_out = moba_w_out[j]
        else:
            o = _mixer_dil(xt, g0, dil_w_in[j], cos, sin, batch, seq)
            w_out = dil_w_out[j]
        xt = _out_proj(o, w_out.astype(BF16), xt, norm_gains[i, 1], "out_proj_%d" % i)
        xt = _ffn(xt, norm_gains[i, 2], ffn_w_up[i].astype(BF16), ffn_conv_w[i], ffn_conv_b[i],
                  ffn_w_down[i].astype(BF16), norm_gains[i, 3], ple_norm_g[i],
                  ple_gate_w[i].astype(BF16), ple_proj_w[i].astype(BF16), p[i].reshape(t, PLE_DIM),
                  seq, "ffn_%d" % i)
    return xt.reshape(batch, seq, d)
```

```python
import functools
import math

import jax
import jax.numpy as jnp
from jax import lax
from jax.experimental import pallas as pl
from jax.experimental.pallas import tpu as pltpu

F32 = jnp.float32
BF16 = jnp.bfloat16

D_MODEL = 1024
DEPTH = 4
N_MIXERS = 4
HEAD_DIM = 64
ROPE_THETA = 10000.0
NORM_EPS = 1e-6
PLE_DIM = 256
NEG = -1e30

DIFF_HEADS = D_MODEL // (2 * HEAD_DIM)

NSA_HEADS = D_MODEL // HEAD_DIM
NSA_GROUPS = 4
NSA_REP = NSA_HEADS // NSA_GROUPS
NSA_CMP_LEN = 32
NSA_CMP_STRIDE = 16
NSA_CMP_HIDDEN = 128
NSA_SLC_BLOCK = 64
NSA_TOPN = 16
NSA_N_LOCAL = 2
NSA_WINDOW = 512
NSA_FORCE_BONUS = 1e4

MOBA_HEADS = D_MODEL // HEAD_DIM
MOBA_BLOCK = 256
MOBA_TOPK = 3

DIL_PATTERNS = ((128, 1), (512, 4), (2048, 16))
DIL_HEADS = 8

D_FF = 2816
CONV_WIDTH = 3

LANES = 128
MXU_COLS = 256
TQ = 256
LOG2E = 1.4426950408889634
Q_SCALE = (HEAD_DIM ** -0.5) * LOG2E
VMEM_LIMIT = 56 * 1024 * 1024


def _cparams(sem):
    return pltpu.CompilerParams(dimension_semantics=sem, vmem_limit_bytes=VMEM_LIMIT)


def _rms(x, g):
    return x * lax.rsqrt(jnp.mean(x * x, axis=-1, keepdims=True) + NORM_EPS) * g


def _dot(a, b, precision=None):
    return jnp.dot(a, b, preferred_element_type=F32, precision=precision)


def _dot_nt(a, b, precision=None):
    return lax.dot_general(a, b, (((1,), (1,)), ((), ())), preferred_element_type=F32,
                           precision=precision)


def _lane_iota(shape):
    return lax.broadcasted_iota(jnp.int32, shape, 1)


def _row_iota(shape):
    return lax.broadcasted_iota(jnp.int32, shape, 0)


def _rope_kernel(pos_ref, inv_ref, cos_ref, sin_ref):
    ang = pos_ref[...].astype(F32) * inv_ref[...]
    lane = _lane_iota(ang.shape)
    cos_ref[...] = jnp.cos(ang)
    sin_ref[...] = jnp.where((lane & 63) < 32, -jnp.sin(ang), jnp.sin(ang))


def _rope_tables(positions):
    t = positions.size
    inv = ROPE_THETA ** (-jnp.arange(0, HEAD_DIM, 2, dtype=F32) / HEAD_DIM)
    inv = jnp.tile(inv, 4)[None, :]
    tm = 2048
    return pl.pallas_call(
        _rope_kernel,
        grid=(t // tm,),
        in_specs=[pl.BlockSpec((tm, 1), lambda i: (i, 0)),
                  pl.BlockSpec((1, LANES), lambda i: (0, 0))],
        out_specs=[pl.BlockSpec((tm, LANES), lambda i: (i, 0))] * 2,
        out_shape=[jax.ShapeDtypeStruct((t, LANES), F32)] * 2,
        compiler_params=_cparams(("parallel",)),
        name="rope_tables",
    )(positions.reshape(t, 1), inv)


def _proj_kernel(x_ref, g_ref, w_ref, cos_ref, sin_ref, *o_refs, blocks):
    hn = _rms(x_ref[...], g_ref[...]).astype(BF16)
    cos = cos_ref[...]
    sin = sin_ref[...]
    first = (_lane_iota(cos.shape) & 63) < 32
    n_chunks = len(blocks) // 2
    for c in range(n_chunks):
        acc = _dot(hn, w_ref[:, c * MXU_COLS:(c + 1) * MXU_COLS])
        for s in range(2):
            out_idx, out_blk, rope, scale = blocks[2 * c + s]
            if out_idx is None:
                continue
            a = acc[:, s * LANES:(s + 1) * LANES]
            if rope:
                rot = jnp.where(first, pltpu.roll(a, 96, 1), pltpu.roll(a, 32, 1))
                a = a * cos + rot * sin
            if scale != 1.0:
                a = a * scale
            o_ref = o_refs[out_idx]
            o_ref[:, out_blk * LANES:(out_blk + 1) * LANES] = a.astype(o_ref.dtype)


def _proj(x, g, w, cos, sin, blocks, outs, name):
    t = x.shape[0]
    n = w.shape[1]
    assert n == len(blocks) * LANES and len(blocks) % 2 == 0
    tm = 512
    return pl.pallas_call(
        functools.partial(_proj_kernel, blocks=tuple(blocks)),
        grid=(t // tm,),
        in_specs=[pl.BlockSpec((tm, D_MODEL), lambda i: (i, 0)),
                  pl.BlockSpec((1, D_MODEL), lambda i: (0, 0)),
                  pl.BlockSpec((D_MODEL, n), lambda i: (0, 0)),
                  pl.BlockSpec((tm, LANES), lambda i: (i, 0)),
                  pl.BlockSpec((tm, LANES), lambda i: (i, 0))],
        out_specs=[pl.BlockSpec((tm, nb * LANES), lambda i: (i, 0)) for nb, _ in outs],
        out_shape=[jax.ShapeDtypeStruct((t, nb * LANES), dt) for nb, dt in outs],
        compiler_params=_cparams(("parallel",)),
        name=name,
    )(x, g.reshape(1, D_MODEL), w, cos, sin)


def _out_kernel(o_ref, w_ref, x_ref, g_ref, y_ref):
    y = _dot(o_ref[...], w_ref[...])
    y_ref[...] = x_ref[...] + _rms(y, g_ref[...])


def _out_proj(o, w, x, g, name):
    t, k = o.shape
    tm = 512
    return pl.pallas_call(
        _out_kernel,
        grid=(t // tm,),
        in_specs=[pl.BlockSpec((tm, k), lambda i: (i, 0)),
                  pl.BlockSpec((k, D_MODEL), lambda i: (0, 0)),
                  pl.BlockSpec((tm, D_MODEL), lambda i: (i, 0)),
                  pl.BlockSpec((1, D_MODEL), lambda i: (0, 0))],
        out_specs=pl.BlockSpec((tm, D_MODEL), lambda i: (i, 0)),
        out_shape=jax.ShapeDtypeStruct((t, D_MODEL), F32),
        compiler_params=_cparams(("parallel",)),
        name=name,
    )(o, w, x, g.reshape(1, D_MODEL))


FF_CHUNK = 256
FF_TM = 512
FF_HALO = 16


def _gelu_tanh(x):
    k = math.sqrt(2.0 / math.pi)
    h = 0.5 * x
    return h + h * jnp.tanh(x * (k + (k * 0.044715) * (x * x)))


def _ffn_kernel(x_ref, xh_ref, g2_ref, wu_ref, cw_ref, cb_ref, wd_ref, g3_ref, pg_ref, pgw_ref, ppw_ref,
                p_ref, y_ref, hn_s, *, seq):
    i = pl.program_id(0)
    tm = x_ref.shape[0]
    n_c = D_FF // FF_CHUNK
    x = x_ref[...]
    hn_s[FF_HALO:, :] = _rms(x, g2_ref[...]).astype(BF16)
    halo = _rms(xh_ref[...], g2_ref[...])
    seq_start = (i * tm) % seq == 0
    hn_s[:FF_HALO, :] = jnp.where(seq_start, 0.0, halo).astype(BF16)

    def conv(u, col):
        cols = slice(col, col + FF_CHUNK)
        out = cb_ref[:, cols] + pltpu.roll(u, 2, 0)[FF_HALO:] * cw_ref[0:1, cols]
        out = out + pltpu.roll(u, 1, 0)[FF_HALO:] * cw_ref[1:2, cols]
        return out + u[FF_HALO:] * cw_ref[2:3, cols]

    def up(c, r):
        hn = hn_s[r * FF_ROWS:(r + 1) * FF_ROWS + FF_HALO, :]
        gcol, vcol = c * FF_CHUNK, D_FF + c * FF_CHUNK
        return (_dot(hn, wu_ref[:, gcol:gcol + FF_CHUNK]), _dot(hn, wu_ref[:, vcol:vcol + FF_CHUNK]))

    def down(c, act, acc):
        d = _dot(act, wd_ref[c * FF_CHUNK:(c + 1) * FF_CHUNK, :])
        return d if acc is None else acc + d

    units = [(c, r) for c in range(n_c) for r in range(tm // FF_ROWS)]
    accs = [None] * (tm // FF_ROWS)
    ups = {k: up(*units[k]) for k in range(FF_UP_LEAD)}
    pending = None
    for k, (c, r) in enumerate(units):
        ug, uv = ups.pop(k)
        if k + FF_UP_LEAD < len(units):
            ups[k + FF_UP_LEAD] = up(*units[k + FF_UP_LEAD])
        act = (_gelu_tanh(conv(ug, c * FF_CHUNK)) * conv(uv, D_FF + c * FF_CHUNK)).astype(BF16)
        if pending is not None:
            accs[pending[1]] = down(pending[0], pending[2], accs[pending[1]])
        pending = (c, r, act)
    accs[pending[1]] = down(pending[0], pending[2], accs[pending[1]])

    x2 = x + _rms(jnp.concatenate(accs, axis=0), g3_ref[...])
    gate = jax.nn.sigmoid(_dot(_rms(x2, pg_ref[...]).astype(BF16), pgw_ref[...]))
    y_ref[...] = x2 + gate * _dot(p_ref[...].astype(BF16), ppw_ref[...])


FF_ROWS = 256
FF_UP_LEAD = 2


def _ffn(x, g2, w_up, conv_w, conv_b, w_down, g3, ple_g, ple_gate_w, ple_proj_w, p, seq, name):
    t = x.shape[0]
    tm = FF_TM
    assert seq % tm == 0 and tm % FF_ROWS == 0
    hb = tm // FF_HALO
    row = lambda v: v.reshape(1, -1)
    const = lambda i: (0, 0)
    resident = lambda shape: pl.BlockSpec(shape, const, pipeline_mode=pl.Buffered(1))
    return pl.pallas_call(
        functools.partial(_ffn_kernel, seq=seq),
        grid=(t // tm,),
        in_specs=[pl.BlockSpec((tm, D_MODEL), lambda i: (i, 0)),
                  pl.BlockSpec((FF_HALO, D_MODEL), lambda i: (jnp.maximum(i * hb - 1, 0), 0)),
                  pl.BlockSpec((1, D_MODEL), const),
                  resident((D_MODEL, 2 * D_FF)),
                  pl.BlockSpec((CONV_WIDTH, 2 * D_FF), const),
                  pl.BlockSpec((1, 2 * D_FF), const),
                  resident((D_FF, D_MODEL)),
                  pl.BlockSpec((1, D_MODEL), const),
                  pl.BlockSpec((1, D_MODEL), const),
                  resident((D_MODEL, D_MODEL)),
                  resident((PLE_DIM, D_MODEL)),
                  pl.BlockSpec((tm, PLE_DIM), lambda i: (i, 0))],
        out_specs=pl.BlockSpec((tm, D_MODEL), lambda i: (i, 0)),
        out_shape=jax.ShapeDtypeStruct((t, D_MODEL), F32),
        scratch_shapes=[pltpu.VMEM((tm + FF_HALO, D_MODEL), BF16)],
        compiler_params=_cparams(("parallel",)),
        name=name,
    )(x, x, row(g2), w_up, conv_w, row(conv_b), w_down, row(g3), row(ple_g), ple_gate_w, ple_proj_w, p)


def _lo_half(shape):
    return _lane_iota(shape) < HEAD_DIM


def _split_heads(q):
    lo = _lo_half(q.shape)
    qf = q.astype(F32)
    return jnp.where(lo, qf, 0.0).astype(q.dtype), jnp.where(lo, 0.0, qf).astype(q.dtype)


def _tile(ref, j):
    return ref[j * TQ:(j + 1) * TQ, :]


def _scores(q_op, k_ref, first, masks, step=1):
    s_list = []
    for t, mask in enumerate(masks):
        s = _dot_nt(_tile(k_ref, first + t * step), q_op)
        s_list.append(s if mask is None else jnp.where(mask, s, NEG))
    return s_list


def _attend(s_list, values_t):
    mt = None
    for s in s_list:
        mt = s if mt is None else jnp.maximum(mt, s)
    m = jnp.max(mt, axis=0, keepdims=True)
    acc = None
    for s, v_t in zip(s_list, values_t):
        d = _dot(v_t, jnp.exp2(s - m).astype(BF16))
        acc = d if acc is None else acc + d
    return acc


def _value_t(v_tile, keep=None):
    v_t = v_tile.astype(F32).T
    return (v_t if keep is None else jnp.where(keep, v_t, 1.0)).astype(BF16)


def _store_pair_values(v_ref, vt_s, j, *idx):
    lo_rows = _row_iota((LANES, TQ)) < HEAD_DIM
    v_t = _tile(v_ref, j).astype(F32).T
    cols = slice(j * TQ, (j + 1) * TQ)
    vt_s[idx + (0, slice(None), cols)] = jnp.where(lo_rows, v_t, 1.0).astype(BF16)
    vt_s[idx + (1, slice(None), cols)] = jnp.where(lo_rows, 1.0, v_t).astype(BF16)


def _pair_output(acc0, acc1):
    out0 = acc0[:HEAD_DIM] / acc0[HEAD_DIM:HEAD_DIM + 1]
    out1 = acc1[HEAD_DIM:] / acc1[0:1]
    return jnp.concatenate([out0, out1], axis=0).T


ONES_ROWS = 16


def _tile_t(vt_s, j, *idx):
    return vt_s[idx + (slice(None), slice(j * TQ, (j + 1) * TQ))]


def _run_pipelined(stages, lead=None):
    n = len(stages)
    if lead is None:
        for prep_fn, _, _ in stages:
            if prep_fn is not None:
                prep_fn()
        lead = 0
    states = {}
    for step in range(n + lead + 1):
        if lead and step < n and stages[step][0] is not None:
            stages[step][0]()
        if lead <= step < n + lead:
            states[step - lead] = stages[step - lead][1]()
        if step >= lead + 1:
            stages[step - lead - 1][2](states.pop(step - lead - 1))


PREP_LEAD = 3


def _causal_masks():
    diff = _lane_iota((TQ, TQ)) - _row_iota((TQ, TQ))
    return diff, diff >= 0


def _rank_rows(score, n_valid):
    ridx = _row_iota(score.shape)
    cnt = jnp.zeros(score.shape, F32)
    for j in range(n_valid):
        other = score[j:j + 1, :]
        cnt = cnt + jnp.where(other > score, 1.0,
                              jnp.where(other == score, jnp.where(ridx > j, 1.0, 0.0), 0.0))
    return cnt


def _bias_lanes(bias_t, first_lane):
    r, tq = bias_t.shape
    parts = []
    if first_lane:
        parts.append(jnp.zeros((first_lane, tq), F32))
    parts.append(bias_t)
    if LANES - first_lane - r:
        parts.append(jnp.zeros((LANES - first_lane - r, tq), F32))
    return jnp.concatenate(parts, axis=0).T


def _diff_kernel(q_ref, k_ref, v_ref, lam_ref, sg_ref, o_ref, vt_s, *, lambda_init):
    nq = q_ref.shape[0] // TQ
    _, causal = _causal_masks()
    lamv = lam_ref[...]
    lam = (jnp.exp(jnp.sum(lamv[0:1] * lamv[1:2], axis=1, keepdims=True))
           - jnp.exp(jnp.sum(lamv[2:3] * lamv[3:4], axis=1, keepdims=True)) + lambda_init)
    sg = sg_ref[...]

    def prep(i):
        vt_s[:LANES, i * TQ:(i + 1) * TQ] = _value_t(_tile(v_ref, i))
        vt_s[LANES:, i * TQ:(i + 1) * TQ] = jnp.ones((ONES_ROWS, TQ), BF16)

    def score(i):
        q0, q1 = _split_heads(_tile(q_ref, i))
        masks = [None] * i + [causal]
        return _scores(q0, k_ref, 0, masks), _scores(q1, k_ref, 0, masks)

    def finish(i, state):
        v_tiles = [_tile_t(vt_s, j) for j in range(i + 1)]
        a0 = _attend(state[0], v_tiles)
        a1 = _attend(state[1], v_tiles)
        o_t = a0[:LANES] / a0[LANES:LANES + 1] - lam * (a1[:LANES] / a1[LANES:LANES + 1])
        o_ref[i * TQ:(i + 1) * TQ, :] = (_rms(o_t.T, sg) * (1.0 - lambda_init)).astype(o_ref.dtype)

    _run_pipelined([(functools.partial(prep, i), functools.partial(score, i), functools.partial(finish, i))
                    for i in range(nq)], PREP_LEAD)


def _diff_attention(qkv, lam_rows, subln_g, batch, seq, lambda_init):
    t = batch * seq
    h = DIFF_HEADS
    return pl.pallas_call(
        functools.partial(_diff_kernel, lambda_init=lambda_init),
        grid=(batch, h),
        in_specs=[pl.BlockSpec((seq, LANES), lambda b, hh: (b, hh)),
                  pl.BlockSpec((seq, LANES), lambda b, hh: (b, h + hh)),
                  pl.BlockSpec((seq, LANES), lambda b, hh: (b, 2 * h + hh)),
                  pl.BlockSpec((8, LANES), lambda b, hh: (0, 0)),
                  pl.BlockSpec((1, LANES), lambda b, hh: (0, 0))],
        out_specs=pl.BlockSpec((seq, LANES), lambda b, hh: (b, hh)),
        out_shape=jax.ShapeDtypeStruct((t, h * LANES), BF16),
        scratch_shapes=[pltpu.VMEM((LANES + ONES_ROWS, seq), BF16)],
        compiler_params=_cparams(("parallel", "parallel")),
        name="diff_attention",
    )(qkv, qkv, qkv, lam_rows, subln_g.reshape(1, LANES))


def _moba_kernel(q_ref, k_ref, v_ref, o_ref, ka_s, vt_s):
    seq = k_ref.shape[0]
    nb = seq // MOBA_BLOCK
    lane = _lane_iota((TQ, LANES))
    lo = lane < HEAD_DIM
    _, causal = _causal_masks()
    km = jnp.concatenate([jnp.mean(_tile(k_ref, j).astype(F32), axis=0, keepdims=True) for j in range(nb)],
                         axis=0)
    q_aug = {}
    outs = {}

    def prep(i, a):
        if a == 0:
            _store_pair_values(v_ref, vt_s, i)
            kf = _tile(k_ref, i).astype(F32)
            ka_s[0, i * TQ:(i + 1) * TQ, :] = jnp.where(lo, kf, jnp.where(lane == HEAD_DIM + i, 1.0, 0.0)
                                                        ).astype(BF16)
            ka_s[1, i * TQ:(i + 1) * TQ, :] = jnp.where(lo, jnp.where(lane == i, 1.0, 0.0), kf).astype(BF16)
        head = _split_heads(_tile(q_ref, i))[a]
        q_aug[i, a] = head
        if i > 0:
            gate_t = _dot_nt(km, head.astype(F32), precision=lax.Precision.HIGHEST)
            rank = _rank_rows(gate_t, i)
            drop = jnp.logical_and(_row_iota(gate_t.shape) < i, rank >= MOBA_TOPK)
            bias = _bias_lanes(jnp.where(drop, NEG, 0.0), HEAD_DIM * (1 - a))
            q_aug[i, a] = (head.astype(F32) + bias).astype(BF16)

    def score(i, a):
        return _scores(q_aug[i, a], ka_s.at[a], 0, [None] * i + [causal])

    def finish(i, a, s_list):
        outs[i, a] = _attend(s_list, [_tile_t(vt_s, j, a) for j in range(i + 1)])
        if a == 1:
            o_ref[i * TQ:(i + 1) * TQ, :] = _pair_output(outs[i, 0], outs[i, 1]).astype(o_ref.dtype)

    _run_pipelined([(functools.partial(prep, i, a), functools.partial(score, i, a),
                     functools.partial(finish, i, a)) for i in range(nb) for a in range(2)], PREP_LEAD)


def _moba_attention(qkv, batch, seq):
    assert MOBA_BLOCK == TQ and seq % MOBA_BLOCK == 0 and seq // MOBA_BLOCK <= HEAD_DIM
    t = batch * seq
    hp = MOBA_HEADS // 2
    return pl.pallas_call(
        _moba_kernel,
        grid=(batch, hp),
        in_specs=[pl.BlockSpec((seq, LANES), lambda b, hh: (b, hh)),
                  pl.BlockSpec((seq, LANES), lambda b, hh: (b, hp + hh)),
                  pl.BlockSpec((seq, LANES), lambda b, hh: (b, 2 * hp + hh))],
        out_specs=pl.BlockSpec((seq, LANES), lambda b, hh: (b, hh)),
        out_shape=jax.ShapeDtypeStruct((t, hp * LANES), BF16),
        scratch_shapes=[pltpu.VMEM((2, seq, LANES), BF16), pltpu.VMEM((2, LANES, seq), BF16)],
        compiler_params=_cparams(("parallel", "parallel")),
        name="moba_attention",
    )(qkv, qkv, qkv)


def _dil_kernel(q0_ref, q1_ref, q2_ref, k0_ref, k1_ref, k2_ref, v0_ref, v1_ref, v2_ref, o_ref, vt_s):
    seq = k0_ref.shape[0]
    nq = seq // TQ
    (w0, _), (w1, d1), (w2, d2) = DIL_PATTERNS
    assert w0 < TQ and w1 == 2 * TQ and w2 >= seq
    diff, causal = _causal_masks()
    on1 = (diff & (d1 - 1)) == 0
    on2 = (diff & (d2 - 1)) == 0
    m0_diag = jnp.logical_and(causal, diff <= w0)
    m0_prev = diff + TQ <= w0
    m1_diag = jnp.logical_and(causal, on1)
    m1_prev2 = jnp.logical_and(on1, diff <= 0)
    m2_diag = jnp.logical_and(causal, on2)
    lo = _lo_half((TQ, LANES))
    k_refs = (k0_ref, k1_ref, k2_ref)
    q_refs = (q0_ref, q1_ref, q2_ref)
    outs = {}

    def runs(i):
        g0 = ([m0_prev] if i >= 1 else []) + [m0_diag]
        g1 = ([m1_prev2] if i >= 2 else []) + ([on1] if i >= 1 else []) + [m1_diag]
        g2 = [on2] * i + [m2_diag]
        return [(g, i + 1 - len(masks), masks) for g, masks in enumerate((g0, g1, g2))]

    def score(i, a):
        s_list = []
        for g, first, masks in runs(i):
            s_list += _scores(_split_heads(_tile(q_refs[g], i))[a], k_refs[g], first, masks)
        return s_list

    def finish(i, a, s_list):
        v_tiles = [_tile_t(vt_s, first + t, g, a) for g, first, masks in runs(i) for t in range(len(masks))]
        outs[i, a] = _attend(s_list, v_tiles)
        if a == 1:
            o_ref[i * TQ:(i + 1) * TQ, :] = _pair_output(outs[i, 0], outs[i, 1]).astype(o_ref.dtype)

    def prep(i, a):
        if a == 0:
            for g, v_ref in enumerate((v0_ref, v1_ref, v2_ref)):
                _store_pair_values(v_ref, vt_s, i, g)

    _run_pipelined([(functools.partial(prep, i, a), functools.partial(score, i, a),
                     functools.partial(finish, i, a)) for i in reversed(range(nq)) for a in range(2)])


def _dil_attention(qkv, batch, seq):
    t = batch * seq
    hp = DIL_HEADS // 2
    per_g = 3 * hp

    def spec(g, which):
        return pl.BlockSpec((seq, LANES), lambda b, hh: (b, g * per_g + which * hp + hh))

    return pl.pallas_call(
        _dil_kernel,
        grid=(batch, hp),
        in_specs=[spec(0, 0), spec(1, 0), spec(2, 0), spec(0, 1), spec(1, 1), spec(2, 1),
                  spec(0, 2), spec(1, 2), spec(2, 2)],
        out_specs=pl.BlockSpec((seq, LANES), lambda b, hh: (b, hh)),
        out_shape=jax.ShapeDtypeStruct((t, hp * LANES), BF16),
        scratch_shapes=[pltpu.VMEM((len(DIL_PATTERNS), 2, LANES, seq), BF16)],
        compiler_params=_cparams(("parallel", "parallel")),
        name="dilated_attention",
    )(*([qkv] * 9))


N_CMP_PAD = 128


def _cmp_kernel(zk_ref, zv_ref, w1k_ref, w1v_ref, pk_ref, pv_ref, w1fk_ref, w1fv_ref, w2k_ref, w2v_ref,
                ko_ref, vo_ref):
    def one(z_ref, w1_ref, pos_ref, w1f_ref, w2_ref, o_ref):
        z = z_ref[...]
        a = _dot(z, w1_ref[0])
        b = _dot(z, w1_ref[1])
        b_next = jnp.concatenate([b[1:], jnp.zeros_like(b[:1])], axis=0)
        const = _dot(pos_ref[...], w1f_ref[...])[0:1]
        hid = a + b_next + jnp.concatenate([const] * NSA_GROUPS, axis=1)
        o_ref[...] = _dot(jax.nn.gelu(hid, approximate=True).astype(BF16), w2_ref[...]).astype(o_ref.dtype)

    one(zk_ref, w1k_ref, pk_ref, w1fk_ref, w2k_ref, ko_ref)
    one(zv_ref, w1v_ref, pv_ref, w1fv_ref, w2v_ref, vo_ref)


def _nsa_compress(kc, vc, ck_pos, ck_w1, ck_w2, cv_pos, cv_w1, cv_w2, batch, seq):
    g, dk, f = NSA_GROUPS, HEAD_DIM, NSA_CMP_HIDDEN
    half = NSA_CMP_LEN // 2
    assert NSA_CMP_STRIDE == half and seq // half == N_CMP_PAD
    zk = kc.reshape(batch * N_CMP_PAD, half * g * dk)
    zv = vc.reshape(batch * N_CMP_PAD, half * g * dk)

    def expand_w1(w1):
        w = w1.reshape(2, half, dk, f)
        eye = jnp.eye(g, dtype=w1.dtype)
        big = jnp.einsum('acdf,gh->acgdhf', w, eye)
        return big.reshape(2, half * g * dk, g * f).astype(BF16)

    def expand_w2(w2):
        dup = jnp.concatenate([w2, w2], axis=1)
        eye = jnp.eye(g, dtype=w2.dtype)
        return jnp.einsum('fd,gh->gfhd', dup, eye).reshape(g * f, g * LANES).astype(BF16)

    def pos_rows(pos):
        flat = pos.reshape(1, NSA_CMP_LEN * dk)
        return jnp.concatenate([flat, jnp.zeros((7, NSA_CMP_LEN * dk), pos.dtype)], axis=0).astype(BF16)

    kd = half * g * dk
    const2 = lambda b: (0, 0)
    const3 = lambda b: (0, 0, 0)
    return pl.pallas_call(
        _cmp_kernel,
        grid=(batch,),
        in_specs=[pl.BlockSpec((N_CMP_PAD, kd), lambda b: (b, 0)),
                  pl.BlockSpec((N_CMP_PAD, kd), lambda b: (b, 0)),
                  pl.BlockSpec((2, kd, g * f), const3),
                  pl.BlockSpec((2, kd, g * f), const3),
                  pl.BlockSpec((8, NSA_CMP_LEN * dk), const2),
                  pl.BlockSpec((8, NSA_CMP_LEN * dk), const2),
                  pl.BlockSpec((NSA_CMP_LEN * dk, f), const2),
                  pl.BlockSpec((NSA_CMP_LEN * dk, f), const2),
                  pl.BlockSpec((g * f, g * LANES), const2),
                  pl.BlockSpec((g * f, g * LANES), const2)],
        out_specs=[pl.BlockSpec((N_CMP_PAD, g * LANES), lambda b: (b, 0))] * 2,
        out_shape=[jax.ShapeDtypeStruct((batch * N_CMP_PAD, g * LANES), BF16)] * 2,
        compiler_params=_cparams(("parallel",)),
        name="nsa_compress",
    )(zk, zv, expand_w1(ck_w1), expand_w1(cv_w1), pos_rows(ck_pos), pos_rows(cv_pos),
      ck_w1.reshape(NSA_CMP_LEN * dk, f).astype(BF16), cv_w1.reshape(NSA_CMP_LEN * dk, f).astype(BF16),
      expand_w2(ck_w2), expand_w2(cv_w2))


SLC_SHIFT = NSA_SLC_BLOCK.bit_length() - 1


def _nsa_heads(q_pair):
    lo = _lo_half(q_pair.shape)
    qf = q_pair.astype(F32)
    return jnp.where(lo, qf, 0.0), jnp.where(lo, pltpu.roll(qf, HEAD_DIM, 1), 0.0)


def _lane_column(x, c):
    return jnp.sum(jnp.where(_lane_iota(x.shape) == c, x, 0.0), axis=1, keepdims=True)


def _nsa_select_kernel(q_ref, kc_ref, vc_ref, gate_ref, ov_ref, oc_ref, bias_ref):
    seq = q_ref.shape[0]
    nq = seq // TQ
    n_slc = seq // NSA_SLC_BLOCK
    assert 1 << SLC_SHIFT == NSA_SLC_BLOCK and HEAD_DIM + n_slc <= LANES
    n_idx = _row_iota((N_CMP_PAD, TQ))
    t_idx = _lane_iota((N_CMP_PAD, TQ))
    kc = kc_ref[...]
    vc_t = _value_t(vc_ref[...])
    jb = _row_iota((n_slc, TQ))
    cmask, scores = [], []
    for i in range(nq):
        cmask.append(jnp.logical_and(n_idx * NSA_CMP_STRIDE + (NSA_CMP_LEN - 1) <= i * TQ + t_idx,
                                     n_idx < N_CMP_PAD - 1))
        heads_bf = []
        for pr in range(NSA_REP // 2):
            heads_bf.extend(h.astype(BF16) for h in
                            _nsa_heads(q_ref[i * TQ:(i + 1) * TQ, pr * LANES:(pr + 1) * LANES]))
        scores.append([jnp.where(cmask[i], _dot_nt(kc, heads_bf[r]), NEG) for r in range(NSA_REP)])
    p_sums, o_cmps = [], []
    for i in range(nq):
        p_sum = jnp.zeros((N_CMP_PAD, TQ), F32)
        o_cmp = []
        for s in scores[i]:
            e = jnp.where(cmask[i], jnp.exp2(s - jnp.max(s, axis=0, keepdims=True)), 0.0)
            p = e / jnp.maximum(jnp.sum(e, axis=0, keepdims=True), 1e-30)
            p_sum = p_sum + p
            o_cmp.append(_dot(vc_t, p.astype(BF16))[:HEAD_DIM])
        p_sums.append(p_sum)
        o_cmps.append(o_cmp)
    for i in range(nq):
        o_cmp = o_cmps[i]
        imp_t = _dot(ov_ref[...], p_sums[i], precision=lax.Precision.HIGHEST)
        back = ((i * TQ + _lane_iota((n_slc, TQ))) >> SLC_SHIFT) - jb
        ok = back >= 0
        forced = jnp.logical_or(jb == 0, jnp.logical_and(ok, back < NSA_N_LOCAL))
        score = jnp.where(ok, imp_t + jnp.where(forced, NSA_FORCE_BONUS, 0.0), NEG)
        rank = _rank_rows(score, n_slc)
        keep = jnp.logical_and(ok, rank < min(NSA_TOPN, n_slc))
        bias_ref[i * TQ:(i + 1) * TQ, :] = _bias_lanes(jnp.where(keep, 0.0, NEG), HEAD_DIM).astype(bias_ref.dtype)

        gates_t = jax.nn.sigmoid(gate_ref[i * TQ:(i + 1) * TQ, :]).T
        for pr in range(NSA_REP // 2):
            pair = jnp.concatenate([o_cmp[r] * gates_t[3 * r:3 * r + 1] for r in (2 * pr, 2 * pr + 1)], axis=0)
            oc_ref[i * TQ:(i + 1) * TQ, pr * LANES:(pr + 1) * LANES] = pair.T.astype(oc_ref.dtype)


def _nsa_attn_kernel(q_ref, ks_ref, vs_ref, kw_ref, vw_ref, bias_ref, gate_ref, oc_ref, o_ref, ksa_s, vt_s):
    seq = ks_ref.shape[0]
    nq = seq // TQ
    lo_rows = _row_iota((LANES, TQ)) < HEAD_DIM
    per_tile = TQ // NSA_SLC_BLOCK
    assert NSA_WINDOW == 2 * TQ
    pr = pl.program_id(2)
    lane = _lane_iota((TQ, LANES))
    row = _row_iota((TQ, LANES))
    lo = lane < HEAD_DIM
    diff, causal = _causal_masks()
    outs = {}

    def prep(i, h):
        if h == 0:
            for b, v_ref in enumerate((vs_ref, vw_ref)):
                vt_s[b, :, i * TQ:(i + 1) * TQ] = _value_t(_tile(v_ref, i), lo_rows)
            ind = jnp.where(lane - HEAD_DIM == i * per_tile + (row >> SLC_SHIFT), 1.0, 0.0)
            ksa_s[i * TQ:(i + 1) * TQ, :] = jnp.where(lo, _tile(ks_ref, i).astype(F32), ind).astype(BF16)

    def win_masks(i):
        return [causal] + ([None] if i >= 1 else []) + ([diff < 0] if i >= 2 else [])

    def score(i, h):
        head = _nsa_heads(_tile(q_ref, i))[h]
        q_aug = (head + _tile(bias_ref, i).astype(F32)).astype(BF16)
        return (_scores(q_aug, ksa_s, 0, [None] * i + [causal]),
                _scores(head.astype(BF16), kw_ref, i, win_masks(i), step=-1))

    def finish(i, h, state):
        n_win = len(win_masks(i))
        a_slc = _attend(state[0], [_tile_t(vt_s, j, 0) for j in range(i + 1)])
        a_win = _attend(state[1], [_tile_t(vt_s, i - t, 1) for t in range(n_win)])
        gates_t = jax.nn.sigmoid(_tile(gate_ref, i)).T
        c0 = 3 * (2 * pr + h)

        def gate(c):
            return jnp.sum(jnp.where(_row_iota(gates_t.shape) == c, gates_t, 0.0), axis=0, keepdims=True)

        outs[i, h] = (a_slc[:HEAD_DIM] * (gate(c0 + 1) / a_slc[HEAD_DIM:HEAD_DIM + 1])
                      + a_win[:HEAD_DIM] * (gate(c0 + 2) / a_win[HEAD_DIM:HEAD_DIM + 1]))
        if h == 1:
            pair = jnp.concatenate([outs[i, 0], outs[i, 1]], axis=0).T
            o_ref[i * TQ:(i + 1) * TQ, :] = (_tile(oc_ref, i).astype(F32) + pair).astype(o_ref.dtype)

    _run_pipelined([(functools.partial(prep, i, h), functools.partial(score, i, h),
                     functools.partial(finish, i, h)) for i in range(nq) for h in range(2)], PREP_LEAD)


def _nsa_overlap_t(seq):
    n_slc = seq // NSA_SLC_BLOCK
    n = jnp.arange(N_CMP_PAD)
    first = n * NSA_CMP_STRIDE
    last = first + NSA_CMP_LEN - 1
    s_start = jnp.arange(n_slc) * NSA_SLC_BLOCK
    ov = (first[None, :] <= (s_start + NSA_SLC_BLOCK - 1)[:, None]) & (last[None, :] >= s_start[:, None])
    ov = ov & (n[None, :] < (seq - NSA_CMP_LEN) // NSA_CMP_STRIDE + 1)
    return ov.astype(F32)


def _nsa_attention(q, ks, vs, kw, vw, kcmp, vcmp, gates, batch, seq):
    t = batch * seq
    g = NSA_GROUPS
    qb = NSA_REP // 2
    n_slc = seq // NSA_SLC_BLOCK
    grp = lambda: pl.BlockSpec((seq, LANES), lambda b, gg: (b, gg))
    cm = lambda: pl.BlockSpec((N_CMP_PAD, LANES), lambda b, gg: (b, gg))
    o_cmp, bias = pl.pallas_call(
        _nsa_select_kernel,
        grid=(batch, g),
        in_specs=[pl.BlockSpec((seq, qb * LANES), lambda b, gg: (b, gg)), cm(), cm(), grp(),
                  pl.BlockSpec((n_slc, N_CMP_PAD), lambda b, gg: (0, 0))],
        out_specs=[pl.BlockSpec((seq, qb * LANES), lambda b, gg: (b, gg)), grp()],
        out_shape=[jax.ShapeDtypeStruct((t, NSA_HEADS * HEAD_DIM), BF16),
                   jax.ShapeDtypeStruct((t, g * LANES), BF16)],
        compiler_params=_cparams(("parallel", "parallel")),
        name="nsa_select",
    )(q, kcmp, vcmp, gates, _nsa_overlap_t(seq))
    kv = lambda: pl.BlockSpec((seq, LANES), lambda b, gg, pr: (b, gg))
    pair = lambda: pl.BlockSpec((seq, LANES), lambda b, gg, pr: (b, gg * qb + pr))
    return pl.pallas_call(
        _nsa_attn_kernel,
        grid=(batch, g, qb),
        in_specs=[pair(), kv(), kv(), kv(), kv(), kv(), kv(), pair()],
        out_specs=pair(),
        out_shape=jax.ShapeDtypeStruct((t, NSA_HEADS * HEAD_DIM), BF16),
        scratch_shapes=[pltpu.VMEM((seq, LANES), BF16), pltpu.VMEM((2, LANES, seq), BF16)],
        compiler_params=_cparams(("parallel", "parallel", "arbitrary")),
        name="nsa_attention",
    )(q, ks, vs, kw, vw, bias, gates, o_cmp)


def _plain_blocks(n_total, n_rope, n_q):
    return [(0, b, b < n_rope, Q_SCALE if b < n_q else 1.0) for b in range(n_total)]


def _mixer_diff(x, g, w_in, lam_q1, lam_k1, lam_q2, lam_k2, subln_g, cos, sin, batch, seq, lambda_init):
    h = DIFF_HEADS
    blocks = _plain_blocks(3 * h, 2 * h, h)
    (qkv,) = _proj(x, g, w_in.astype(BF16), cos, sin, blocks, [(3 * h, BF16)], "diff_in_proj")
    pad = lambda v: jnp.pad(v.astype(F32), (0, LANES - HEAD_DIM))
    lam_rows = jnp.stack([pad(lam_q1), pad(lam_k1), pad(lam_q2), pad(lam_k2)]
                         + [jnp.zeros((LANES,), F32)] * 4)
    return _diff_attention(qkv, lam_rows, subln_g, batch, seq, lambda_init)


def _mixer_moba(x, g, w_in, cos, sin, batch, seq):
    hp = MOBA_HEADS // 2
    blocks = _plain_blocks(3 * hp, 2 * hp, hp)
    (qkv,) = _proj(x, g, w_in.astype(BF16), cos, sin, blocks, [(3 * hp, BF16)], "moba_in_proj")
    return _moba_attention(qkv, batch, seq)


def _mixer_dil(x, g, w_in, cos, sin, batch, seq):
    hp = DIL_HEADS // 2
    blocks = []
    for gi in range(len(DIL_PATTERNS)):
        for which in range(3):
            for b in range(hp):
                blocks.append((0, len(blocks), which < 2, Q_SCALE if which == 0 else 1.0))
    (qkv,) = _proj(x, g, w_in.astype(BF16), cos, sin, blocks, [(len(blocks), BF16)], "dil_in_proj")
    return _dil_attention(qkv, batch, seq)


def _mixer_nsa(x, g, w_in, ck_pos, ck_w1, ck_w2, cv_pos, cv_w1, cv_w2, cos, sin, batch, seq):
    hq, gr, dk = NSA_HEADS, NSA_GROUPS, HEAD_DIM
    kvw = gr * dk
    o_q = 0
    o_kc, o_vc, o_ks, o_vs, o_kw, o_vw, o_g = (hq * dk + n * kvw for n in range(7))
    n_gate = 3 * hq

    def dup(off):
        cols = w_in[:, off:off + kvw].reshape(D_MODEL, gr, dk)
        return jnp.concatenate([cols, cols], axis=2).reshape(D_MODEL, gr * LANES)

    per_group = n_gate // gr
    gate_cols = jnp.pad(w_in[:, o_g:o_g + n_gate].reshape(D_MODEL, gr, per_group),
                        ((0, 0), (0, 0), (0, LANES - per_group))).reshape(D_MODEL, gr * LANES)
    w = jnp.concatenate([w_in[:, o_q:o_q + hq * dk], dup(o_ks), dup(o_kw), w_in[:, o_kc:o_kc + kvw],
                         dup(o_vs), dup(o_vw), w_in[:, o_vc:o_vc + kvw], gate_cols], axis=1).astype(BF16)
    nqb = hq * dk // LANES
    blocks = [(0, b, True, Q_SCALE) for b in range(nqb)]
    blocks += [(1, b, True, 1.0) for b in range(gr)]
    blocks += [(2, b, True, 1.0) for b in range(gr)]
    blocks += [(3, b, True, 1.0) for b in range(kvw // LANES)]
    blocks += [(4, b, False, 1.0) for b in range(gr)]
    blocks += [(5, b, False, 1.0) for b in range(gr)]
    blocks += [(6, b, False, 1.0) for b in range(kvw // LANES)]
    blocks += [(7, b, False, 1.0) for b in range(gr)]
    outs = [(nqb, BF16), (gr, BF16), (gr, BF16), (kvw // LANES, BF16), (gr, BF16), (gr, BF16),
            (kvw // LANES, BF16), (gr, F32)]
    q, ks, kw, kc, vs, vw, vc, gates = _proj(x, g, w, cos, sin, blocks, outs, "nsa_in_proj")
    kcmp, vcmp = _nsa_compress(kc, vc, ck_pos, ck_w1, ck_w2, cv_pos, cv_w1, cv_w2, batch, seq)
    return _nsa_attention(q, ks, vs, kw, vw, kcmp, vcmp, gates, batch, seq)


def kernel(x, p, positions, norm_gains, ple_norm_g, ple_gate_w, ple_proj_w, ffn_w_up, ffn_conv_w, ffn_conv_b, ffn_w_down, diff_w_in, diff_w_out, diff_lam_q1, diff_lam_k1, diff_lam_q2, diff_lam_k2, diff_subln_g, nsa_w_in, nsa_w_out, nsa_ck_pos, nsa_ck_w1, nsa_ck_w2, nsa_cv_pos, nsa_cv_w1, nsa_cv_w2, moba_w_in, moba_w_out, dil_w_in, dil_w_out):
    batch, seq, d = x.shape
    t = batch * seq
    cos, sin = _rope_tables(positions)
    xt = x.reshape(t, d)
    for i in range(DEPTH):
        kind, j = i % N_MIXERS, i // N_MIXERS
        g0 = norm_gains[i, 0]
        if kind == 0:
            lambda_init = 0.8 - 0.6 * math.exp(-0.3 * i)
            o = _mixer_diff(xt, g0, diff_w_in[j], diff_lam_q1[j], diff_lam_k1[j], diff_lam_q2[j],
                            diff_lam_k2[j], diff_subln_g[j], cos, sin, batch, seq, lambda_init)
            w_out = diff_w_out[j]
        elif kind == 1:
            o = _mixer_nsa(xt, g0, nsa_w_in[j], nsa_ck_pos[j], nsa_ck_w1[j], nsa_ck_w2[j],
                           nsa_cv_pos[j], nsa_cv_w1[j], nsa_cv_w2[j], cos, sin, batch, seq)
            w_out = nsa_w_out[j]
        elif kind == 2:
            o = _mixer_moba(xt, g0, moba_w_in[j], cos, sin, batch, seq)
            w_out = moba_w_out[j]
        else:
            o = _mixer_dil(xt, g0, dil_w_in[j], cos, sin, batch, seq)
            w_out = dil_w_out[j]
        xt = _out_proj(o, w_out.astype(BF16), xt, norm_gains[i, 1], "out_proj_%d" % i)
        xt = _ffn(xt, norm_gains[i, 2], ffn_w_up[i].astype(BF16), ffn_conv_w[i], ffn_conv_b[i],
                  ffn_w_down[i].astype(BF16), norm_gains[i, 3], ple_norm_g[i],
                  ple_gate_w[i].astype(BF16), ple_proj_w[i].astype(BF16), p[i].reshape(t, PLE_DIM),
                  seq, "ffn_%d" % i)
    return xt.reshape(batch, seq, d)
```

```python
import functools
import math

import jax
import jax.numpy as jnp
from jax import lax
from jax.experimental import pallas as pl
from jax.experimental.pallas import tpu as pltpu

F32 = jnp.float32
BF16 = jnp.bfloat16

D_MODEL = 1024
DEPTH = 4
N_MIXERS = 4
HEAD_DIM = 64
ROPE_THETA = 10000.0
NORM_EPS = 1e-6
PLE_DIM = 256
NEG = -1e30

DIFF_HEADS = D_MODEL // (2 * HEAD_DIM)

NSA_HEADS = D_MODEL // HEAD_DIM
NSA_GROUPS = 4
NSA_REP = NSA_HEADS // NSA_GROUPS
NSA_CMP_LEN = 32
NSA_CMP_STRIDE = 16
NSA_CMP_HIDDEN = 128
NSA_SLC_BLOCK = 64
NSA_TOPN = 16
NSA_N_LOCAL = 2
NSA_WINDOW = 512
NSA_FORCE_BONUS = 1e4

MOBA_HEADS = D_MODEL // HEAD_DIM
MOBA_BLOCK = 256
MOBA_TOPK = 3

DIL_PATTERNS = ((128, 1), (512, 4), (2048, 16))
DIL_HEADS = 8

D_FF = 2816
CONV_WIDTH = 3

LANES = 128
MXU_COLS = 256
TQ = 256
LOG2E = 1.4426950408889634
Q_SCALE = (HEAD_DIM ** -0.5) * LOG2E
VMEM_LIMIT = 56 * 1024 * 1024


def _cparams(sem):
    return pltpu.CompilerParams(dimension_semantics=sem, vmem_limit_bytes=VMEM_LIMIT)


def _rms(x, g):
    return x * lax.rsqrt(jnp.mean(x * x, axis=-1, keepdims=True) + NORM_EPS) * g


def _dot(a, b, precision=None):
    return jnp.dot(a, b, preferred_element_type=F32, precision=precision)


def _dot_nt(a, b, precision=None):
    return lax.dot_general(a, b, (((1,), (1,)), ((), ())), preferred_element_type=F32,
                           precision=precision)


def _lane_iota(shape):
    return lax.broadcasted_iota(jnp.int32, shape, 1)


def _row_iota(shape):
    return lax.broadcasted_iota(jnp.int32, shape, 0)


def _rope_kernel(pos_ref, inv_ref, cos_ref, sin_ref):
    ang = pos_ref[...].astype(F32) * inv_ref[...]
    lane = _lane_iota(ang.shape)
    cos_ref[...] = jnp.cos(ang)
    sin_ref[...] = jnp.where((lane & 63) < 32, -jnp.sin(ang), jnp.sin(ang))


def _rope_tables(positions):
    t = positions.size
    inv = ROPE_THETA ** (-jnp.arange(0, HEAD_DIM, 2, dtype=F32) / HEAD_DIM)
    inv = jnp.tile(inv, 4)[None, :]
    tm = 2048
    return pl.pallas_call(
        _rope_kernel,
        grid=(t // tm,),
        in_specs=[pl.BlockSpec((tm, 1), lambda i: (i, 0)),
                  pl.BlockSpec((1, LANES), lambda i: (0, 0))],
        out_specs=[pl.BlockSpec((tm, LANES), lambda i: (i, 0))] * 2,
        out_shape=[jax.ShapeDtypeStruct((t, LANES), F32)] * 2,
        compiler_params=_cparams(("parallel",)),
        name="rope_tables",
    )(positions.reshape(t, 1), inv)


def _proj_kernel(x_ref, g_ref, w_ref, cos_ref, sin_ref, *o_refs, blocks):
    hn = _rms(x_ref[...], g_ref[...]).astype(BF16)
    cos = cos_ref[...]
    sin = sin_ref[...]
    first = (_lane_iota(cos.shape) & 63) < 32
    n_chunks = len(blocks) // 2
    for c in range(n_chunks):
        acc = _dot(hn, w_ref[:, c * MXU_COLS:(c + 1) * MXU_COLS])
        for s in range(2):
            out_idx, out_blk, rope, scale = blocks[2 * c + s]
            if out_idx is None:
                continue
            a = acc[:, s * LANES:(s + 1) * LANES]
            if rope:
                rot = jnp.where(first, pltpu.roll(a, 96, 1), pltpu.roll(a, 32, 1))
                a = a * cos + rot * sin
            if scale != 1.0:
                a = a * scale
            o_ref = o_refs[out_idx]
            o_ref[:, out_blk * LANES:(out_blk + 1) * LANES] = a.astype(o_ref.dtype)


def _proj(x, g, w, cos, sin, blocks, outs, name):
    t = x.shape[0]
    n = w.shape[1]
    assert n == len(blocks) * LANES and len(blocks) % 2 == 0
    tm = 512
    return pl.pallas_call(
        functools.partial(_proj_kernel, blocks=tuple(blocks)),
        grid=(t // tm,),
        in_specs=[pl.BlockSpec((tm, D_MODEL), lambda i: (i, 0)),
                  pl.BlockSpec((1, D_MODEL), lambda i: (0, 0)),
                  pl.BlockSpec((D_MODEL, n), lambda i: (0, 0)),
                  pl.BlockSpec((tm, LANES), lambda i: (i, 0)),
                  pl.BlockSpec((tm, LANES), lambda i: (i, 0))],
        out_specs=[pl.BlockSpec((tm, nb * LANES), lambda i: (i, 0)) for nb, _ in outs],
        out_shape=[jax.ShapeDtypeStruct((t, nb * LANES), dt) for nb, dt in outs],
        compiler_params=_cparams(("parallel",)),
        name=name,
    )(x, g.reshape(1, D_MODEL), w, cos, sin)


def _out_kernel(o_ref, w_ref, x_ref, g_ref, y_ref):
    y = _dot(o_ref[...], w_ref[...])
    y_ref[...] = x_ref[...] + _rms(y, g_ref[...])


def _out_proj(o, w, x, g, name):
    t, k = o.shape
    tm = 1024
    return pl.pallas_call(
        _out_kernel,
        grid=(t // tm,),
        in_specs=[pl.BlockSpec((tm, k), lambda i: (i, 0)),
                  pl.BlockSpec((k, D_MODEL), lambda i: (0, 0)),
                  pl.BlockSpec((tm, D_MODEL), lambda i: (i, 0)),
                  pl.BlockSpec((1, D_MODEL), lambda i: (0, 0))],
        out_specs=pl.BlockSpec((tm, D_MODEL), lambda i: (i, 0)),
        out_shape=jax.ShapeDtypeStruct((t, D_MODEL), F32),
        compiler_params=_cparams(("parallel",)),
        name=name,
    )(o, w, x, g.reshape(1, D_MODEL))


FF_CHUNK = 256
FF_TM = 512
FF_HALO = 16


def _gelu_tanh(x):
    k = math.sqrt(2.0 / math.pi)
    h = 0.5 * x
    return h + h * jnp.tanh(x * (k + (k * 0.044715) * (x * x)))


def _ffn_kernel(x_ref, xh_ref, g2_ref, wu_ref, cw_ref, cb_ref, wd_ref, g3_ref, pg_ref, pgw_ref, ppw_ref,
                p_ref, y_ref, hn_s, *, seq):
    i = pl.program_id(0)
    tm = x_ref.shape[0]
    n_c = D_FF // FF_CHUNK
    x = x_ref[...]
    hn_s[FF_HALO:, :] = _rms(x, g2_ref[...]).astype(BF16)
    halo = _rms(xh_ref[...], g2_ref[...])
    seq_start = (i * tm) % seq == 0
    hn_s[:FF_HALO, :] = jnp.where(seq_start, 0.0, halo).astype(BF16)

    def conv(u, col):
        cols = slice(col, col + FF_CHUNK)
        out = cb_ref[:, cols] + pltpu.roll(u, 2, 0)[FF_HALO:] * cw_ref[0:1, cols]
        out = out + pltpu.roll(u, 1, 0)[FF_HALO:] * cw_ref[1:2, cols]
        return out + u[FF_HALO:] * cw_ref[2:3, cols]

    def up(c, r):
        hn = hn_s[r * FF_ROWS:(r + 1) * FF_ROWS + FF_HALO, :]
        gcol, vcol = c * FF_CHUNK, D_FF + c * FF_CHUNK
        return (_dot(hn, wu_ref[:, gcol:gcol + FF_CHUNK]), _dot(hn, wu_ref[:, vcol:vcol + FF_CHUNK]))

    def down(c, act, acc):
        d = _dot(act, wd_ref[c * FF_CHUNK:(c + 1) * FF_CHUNK, :])
        return d if acc is None else acc + d

    units = [(c, r) for c in range(n_c) for r in range(tm // FF_ROWS)]
    accs = [None] * (tm // FF_ROWS)
    ups = {k: up(*units[k]) for k in range(FF_UP_LEAD)}
    pending = None
    for k, (c, r) in enumerate(units):
        ug, uv = ups.pop(k)
        if k + FF_UP_LEAD < len(units):
            ups[k + FF_UP_LEAD] = up(*units[k + FF_UP_LEAD])
        act = (_gelu_tanh(conv(ug, c * FF_CHUNK)) * conv(uv, D_FF + c * FF_CHUNK)).astype(BF16)
        if pending is not None:
            accs[pending[1]] = down(pending[0], pending[2], accs[pending[1]])
        pending = (c, r, act)
    accs[pending[1]] = down(pending[0], pending[2], accs[pending[1]])

    x2 = x + _rms(jnp.concatenate(accs, axis=0), g3_ref[...])
    gate = jax.nn.sigmoid(_dot(_rms(x2, pg_ref[...]).astype(BF16), pgw_ref[...]))
    y_ref[...] = x2 + gate * _dot(p_ref[...].astype(BF16), ppw_ref[...])


FF_ROWS = 256
FF_UP_LEAD = 2


def _ffn(x, g2, w_up, conv_w, conv_b, w_down, g3, ple_g, ple_gate_w, ple_proj_w, p, seq, name):
    t = x.shape[0]
    tm = FF_TM
    assert seq % tm == 0 and tm % FF_ROWS == 0
    hb = tm // FF_HALO
    row = lambda v: v.reshape(1, -1)
    const = lambda i: (0, 0)
    resident = lambda shape: pl.BlockSpec(shape, const, pipeline_mode=pl.Buffered(1))
    return pl.pallas_call(
        functools.partial(_ffn_kernel, seq=seq),
        grid=(t // tm,),
        in_specs=[pl.BlockSpec((tm, D_MODEL), lambda i: (i, 0)),
                  pl.BlockSpec((FF_HALO, D_MODEL), lambda i: (jnp.maximum(i * hb - 1, 0), 0)),
                  pl.BlockSpec((1, D_MODEL), const),
                  resident((D_MODEL, 2 * D_FF)),
                  pl.BlockSpec((CONV_WIDTH, 2 * D_FF), const),
                  pl.BlockSpec((1, 2 * D_FF), const),
                  resident((D_FF, D_MODEL)),
                  pl.BlockSpec((1, D_MODEL), const),
                  pl.BlockSpec((1, D_MODEL), const),
                  resident((D_MODEL, D_MODEL)),
                  resident((PLE_DIM, D_MODEL)),
                  pl.BlockSpec((tm, PLE_DIM), lambda i: (i, 0))],
        out_specs=pl.BlockSpec((tm, D_MODEL), lambda i: (i, 0)),
        out_shape=jax.ShapeDtypeStruct((t, D_MODEL), F32),
        scratch_shapes=[pltpu.VMEM((tm + FF_HALO, D_MODEL), BF16)],
        compiler_params=_cparams(("parallel",)),
        name=name,
    )(x, x, row(g2), w_up, conv_w, row(conv_b), w_down, row(g3), row(ple_g), ple_gate_w, ple_proj_w, p)


def _lo_half(shape):
    return _lane_iota(shape) < HEAD_DIM


def _split_heads(q):
    lo = _lo_half(q.shape)
    qf = q.astype(F32)
    return jnp.where(lo, qf, 0.0).astype(q.dtype), jnp.where(lo, 0.0, qf).astype(q.dtype)


def _tile(ref, j):
    return ref[j * TQ:(j + 1) * TQ, :]


def _scores(q_op, k_ref, first, masks, step=1):
    s_list = []
    for t, mask in enumerate(masks):
        s = _dot_nt(_tile(k_ref, first + t * step), q_op)
        s_list.append(s if mask is None else jnp.where(mask, s, NEG))
    return s_list


def _attend(s_list, values_t):
    mt = None
    for s in s_list:
        mt = s if mt is None else jnp.maximum(mt, s)
    m = jnp.max(mt, axis=0, keepdims=True)
    acc = None
    for s, v_t in zip(s_list, values_t):
        d = _dot(v_t, jnp.exp2(s - m).astype(BF16))
        acc = d if acc is None else acc + d
    return acc


def _value_t(v_tile, keep=None):
    v_t = v_tile.astype(F32).T
    return (v_t if keep is None else jnp.where(keep, v_t, 1.0)).astype(BF16)


def _store_pair_values(v_ref, vt_s, j, *idx):
    lo_rows = _row_iota((LANES, TQ)) < HEAD_DIM
    v_t = _tile(v_ref, j).astype(F32).T
    cols = slice(j * TQ, (j + 1) * TQ)
    vt_s[idx + (0, slice(None), cols)] = jnp.where(lo_rows, v_t, 1.0).astype(BF16)
    vt_s[idx + (1, slice(None), cols)] = jnp.where(lo_rows, 1.0, v_t).astype(BF16)


def _pair_output(acc0, acc1):
    out0 = acc0[:HEAD_DIM] / acc0[HEAD_DIM:HEAD_DIM + 1]
    out1 = acc1[HEAD_DIM:] / acc1[0:1]
    return jnp.concatenate([out0, out1], axis=0).T


ONES_ROWS = 16


def _tile_t(vt_s, j, *idx):
    return vt_s[idx + (slice(None), slice(j * TQ, (j + 1) * TQ))]


def _run_pipelined(stages, lead=None):
    n = len(stages)
    if lead is None:
        for prep_fn, _, _ in stages:
            if prep_fn is not None:
                prep_fn()
        lead = 0
    states = {}
    for step in range(n + lead + 1):
        if lead and step < n and stages[step][0] is not None:
            stages[step][0]()
        if lead <= step < n + lead:
            states[step - lead] = stages[step - lead][1]()
        if step >= lead + 1:
            stages[step - lead - 1][2](states.pop(step - lead - 1))


PREP_LEAD = 5


def _causal_masks():
    diff = _lane_iota((TQ, TQ)) - _row_iota((TQ, TQ))
    return diff, diff >= 0


def _rank_rows(score, n_valid):
    ridx = _row_iota(score.shape)
    cnt = jnp.zeros(score.shape, F32)
    for j in range(n_valid):
        other = score[j:j + 1, :]
        cnt = cnt + jnp.where(other > score, 1.0,
                              jnp.where(other == score, jnp.where(ridx > j, 1.0, 0.0), 0.0))
    return cnt


def _bias_lanes(bias_t, first_lane):
    r, tq = bias_t.shape
    parts = []
    if first_lane:
        parts.append(jnp.zeros((first_lane, tq), F32))
    parts.append(bias_t)
    if LANES - first_lane - r:
        parts.append(jnp.zeros((LANES - first_lane - r, tq), F32))
    return jnp.concatenate(parts, axis=0).T


def _diff_kernel(q_ref, k_ref, v_ref, lam_ref, sg_ref, o_ref, vt_s, *, lambda_init):
    nq = q_ref.shape[0] // TQ
    _, causal = _causal_masks()
    lamv = lam_ref[...]
    lam = (jnp.exp(jnp.sum(lamv[0:1] * lamv[1:2], axis=1, keepdims=True))
           - jnp.exp(jnp.sum(lamv[2:3] * lamv[3:4], axis=1, keepdims=True)) + lambda_init)
    sg = sg_ref[...]

    def prep(i):
        vt_s[:LANES, i * TQ:(i + 1) * TQ] = _value_t(_tile(v_ref, i))
        vt_s[LANES:, i * TQ:(i + 1) * TQ] = jnp.ones((ONES_ROWS, TQ), BF16)

    def score(i):
        q0, q1 = _split_heads(_tile(q_ref, i))
        masks = [None] * i + [causal]
        return _scores(q0, k_ref, 0, masks), _scores(q1, k_ref, 0, masks)

    def finish(i, state):
        v_tiles = [_tile_t(vt_s, j) for j in range(i + 1)]
        a0 = _attend(state[0], v_tiles)
        a1 = _attend(state[1], v_tiles)
        o_t = a0[:LANES] / a0[LANES:LANES + 1] - lam * (a1[:LANES] / a1[LANES:LANES + 1])
        o_ref[i * TQ:(i + 1) * TQ, :] = (_rms(o_t.T, sg) * (1.0 - lambda_init)).astype(o_ref.dtype)

    _run_pipelined([(functools.partial(prep, i), functools.partial(score, i), functools.partial(finish, i))
                    for i in range(nq)], PREP_LEAD)


def _diff_attention(qkv, lam_rows, subln_g, batch, seq, lambda_init):
    t = batch * seq
    h = DIFF_HEADS
    return pl.pallas_call(
        functools.partial(_diff_kernel, lambda_init=lambda_init),
        grid=(batch, h),
        in_specs=[pl.BlockSpec((seq, LANES), lambda b, hh: (b, hh)),
                  pl.BlockSpec((seq, LANES), lambda b, hh: (b, h + hh)),
                  pl.BlockSpec((seq, LANES), lambda b, hh: (b, 2 * h + hh)),
                  pl.BlockSpec((8, LANES), lambda b, hh: (0, 0)),
                  pl.BlockSpec((1, LANES), lambda b, hh: (0, 0))],
        out_specs=pl.BlockSpec((seq, LANES), lambda b, hh: (b, hh)),
        out_shape=jax.ShapeDtypeStruct((t, h * LANES), BF16),
        scratch_shapes=[pltpu.VMEM((LANES + ONES_ROWS, seq), BF16)],
        compiler_params=_cparams(("parallel", "parallel")),
        name="diff_attention",
    )(qkv, qkv, qkv, lam_rows, subln_g.reshape(1, LANES))


def _moba_kernel(q_ref, k_ref, v_ref, o_ref, ka_s, vt_s):
    seq = k_ref.shape[0]
    nb = seq // MOBA_BLOCK
    lane = _lane_iota((TQ, LANES))
    lo = lane < HEAD_DIM
    _, causal = _causal_masks()
    km = jnp.concatenate([jnp.mean(_tile(k_ref, j).astype(F32), axis=0, keepdims=True) for j in range(nb)],
                         axis=0)
    q_aug = {}
    outs = {}

    def prep(i, a):
        if a == 0:
            _store_pair_values(v_ref, vt_s, i)
            kf = _tile(k_ref, i).astype(F32)
            ka_s[0, i * TQ:(i + 1) * TQ, :] = jnp.where(lo, kf, jnp.where(lane == HEAD_DIM + i, 1.0, 0.0)
                                                        ).astype(BF16)
            ka_s[1, i * TQ:(i + 1) * TQ, :] = jnp.where(lo, jnp.where(lane == i, 1.0, 0.0), kf).astype(BF16)
        head = _split_heads(_tile(q_ref, i))[a]
        q_aug[i, a] = head
        if i > 0:
            gate_t = _dot_nt(km, head.astype(F32), precision=lax.Precision.HIGHEST)
            rank = _rank_rows(gate_t, i)
            drop = jnp.logical_and(_row_iota(gate_t.shape) < i, rank >= MOBA_TOPK)
            bias = _bias_lanes(jnp.where(drop, NEG, 0.0), HEAD_DIM * (1 - a))
            q_aug[i, a] = (head.astype(F32) + bias).astype(BF16)

    def score(i, a):
        return _scores(q_aug[i, a], ka_s.at[a], 0, [None] * i + [causal])

    def finish(i, a, s_list):
        outs[i, a] = _attend(s_list, [_tile_t(vt_s, j, a) for j in range(i + 1)])
        if a == 1:
            o_ref[i * TQ:(i + 1) * TQ, :] = _pair_output(outs[i, 0], outs[i, 1]).astype(o_ref.dtype)

    _run_pipelined([(functools.partial(prep, i, a), functools.partial(score, i, a),
                     functools.partial(finish, i, a)) for i in range(nb) for a in range(2)], PREP_LEAD)


def _moba_attention(qkv, batch, seq):
    assert MOBA_BLOCK == TQ and seq % MOBA_BLOCK == 0 and seq // MOBA_BLOCK <= HEAD_DIM
    t = batch * seq
    hp = MOBA_HEADS // 2
    return pl.pallas_call(
        _moba_kernel,
        grid=(batch, hp),
        in_specs=[pl.BlockSpec((seq, LANES), lambda b, hh: (b, hh)),
                  pl.BlockSpec((seq, LANES), lambda b, hh: (b, hp + hh)),
                  pl.BlockSpec((seq, LANES), lambda b, hh: (b, 2 * hp + hh))],
        out_specs=pl.BlockSpec((seq, LANES), lambda b, hh: (b, hh)),
        out_shape=jax.ShapeDtypeStruct((t, hp * LANES), BF16),
        scratch_shapes=[pltpu.VMEM((2, seq, LANES), BF16), pltpu.VMEM((2, LANES, seq), BF16)],
        compiler_params=_cparams(("parallel", "parallel")),
        name="moba_attention",
    )(qkv, qkv, qkv)


def _dil_kernel(q0_ref, q1_ref, q2_ref, k0_ref, k1_ref, k2_ref, v0_ref, v1_ref, v2_ref, o_ref, vt_s):
    seq = k0_ref.shape[0]
    nq = seq // TQ
    (w0, _), (w1, d1), (w2, d2) = DIL_PATTERNS
    assert w0 < TQ and w1 == 2 * TQ and w2 >= seq
    diff, causal = _causal_masks()
    on1 = (diff & (d1 - 1)) == 0
    on2 = (diff & (d2 - 1)) == 0
    m0_diag = jnp.logical_and(causal, diff <= w0)
    m0_prev = diff + TQ <= w0
    m1_diag = jnp.logical_and(causal, on1)
    m1_prev2 = jnp.logical_and(on1, diff <= 0)
    m2_diag = jnp.logical_and(causal, on2)
    lo = _lo_half((TQ, LANES))
    k_refs = (k0_ref, k1_ref, k2_ref)
    q_refs = (q0_ref, q1_ref, q2_ref)
    outs = {}

    def runs(i):
        g0 = ([m0_prev] if i >= 1 else []) + [m0_diag]
        g1 = ([m1_prev2] if i >= 2 else []) + ([on1] if i >= 1 else []) + [m1_diag]
        g2 = [on2] * i + [m2_diag]
        return [(g, i + 1 - len(masks), masks) for g, masks in enumerate((g0, g1, g2))]

    def score(i, a):
        s_list = []
        for g, first, masks in runs(i):
            s_list += _scores(_split_heads(_tile(q_refs[g], i))[a], k_refs[g], first, masks)
        return s_list

    def finish(i, a, s_list):
        v_tiles = [_tile_t(vt_s, first + t, g, a) for g, first, masks in runs(i) for t in range(len(masks))]
        outs[i, a] = _attend(s_list, v_tiles)
        if a == 1:
            o_ref[i * TQ:(i + 1) * TQ, :] = _pair_output(outs[i, 0], outs[i, 1]).astype(o_ref.dtype)

    def prep(i, a):
        if a == 0:
            for g, v_ref in enumerate((v0_ref, v1_ref, v2_ref)):
                _store_pair_values(v_ref, vt_s, i, g)

    _run_pipelined([(functools.partial(prep, i, a), functools.partial(score, i, a),
                     functools.partial(finish, i, a)) for i in reversed(range(nq)) for a in range(2)])


def _dil_attention(qkv, batch, seq):
    t = batch * seq
    hp = DIL_HEADS // 2
    per_g = 3 * hp

    def spec(g, which):
        return pl.BlockSpec((seq, LANES), lambda b, hh: (b, g * per_g + which * hp + hh))

    return pl.pallas_call(
        _dil_kernel,
        grid=(batch, hp),
        in_specs=[spec(0, 0), spec(1, 0), spec(2, 0), spec(0, 1), spec(1, 1), spec(2, 1),
                  spec(0, 2), spec(1, 2), spec(2, 2)],
        out_specs=pl.BlockSpec((seq, LANES), lambda b, hh: (b, hh)),
        out_shape=jax.ShapeDtypeStruct((t, hp * LANES), BF16),
        scratch_shapes=[pltpu.VMEM((len(DIL_PATTERNS), 2, LANES, seq), BF16)],
        compiler_params=_cparams(("parallel", "parallel")),
        name="dilated_attention",
    )(*([qkv] * 9))


N_CMP_PAD = 128


def _cmp_kernel(zk_ref, zv_ref, w1k_ref, w1v_ref, pk_ref, pv_ref, w1fk_ref, w1fv_ref, w2k_ref, w2v_ref,
                ko_ref, vo_ref):
    def one(z_ref, w1_ref, pos_ref, w1f_ref, w2_ref, o_ref):
        z = z_ref[...]
        a = _dot(z, w1_ref[0])
        b = _dot(z, w1_ref[1])
        b_next = jnp.concatenate([b[1:], jnp.zeros_like(b[:1])], axis=0)
        const = _dot(pos_ref[...], w1f_ref[...])[0:1]
        hid = a + b_next + jnp.concatenate([const] * NSA_GROUPS, axis=1)
        o_ref[...] = _dot(jax.nn.gelu(hid, approximate=True).astype(BF16), w2_ref[...]).astype(o_ref.dtype)

    one(zk_ref, w1k_ref, pk_ref, w1fk_ref, w2k_ref, ko_ref)
    one(zv_ref, w1v_ref, pv_ref, w1fv_ref, w2v_ref, vo_ref)


def _nsa_compress(kc, vc, ck_pos, ck_w1, ck_w2, cv_pos, cv_w1, cv_w2, batch, seq):
    g, dk, f = NSA_GROUPS, HEAD_DIM, NSA_CMP_HIDDEN
    half = NSA_CMP_LEN // 2
    assert NSA_CMP_STRIDE == half and seq // half == N_CMP_PAD
    zk = kc.reshape(batch * N_CMP_PAD, half * g * dk)
    zv = vc.reshape(batch * N_CMP_PAD, half * g * dk)

    def expand_w1(w1):
        w = w1.reshape(2, half, dk, f)
        eye = jnp.eye(g, dtype=w1.dtype)
        big = jnp.einsum('acdf,gh->acgdhf', w, eye)
        return big.reshape(2, half * g * dk, g * f).astype(BF16)

    def expand_w2(w2):
        dup = jnp.concatenate([w2, w2], axis=1)
        eye = jnp.eye(g, dtype=w2.dtype)
        return jnp.einsum('fd,gh->gfhd', dup, eye).reshape(g * f, g * LANES).astype(BF16)

    def pos_rows(pos):
        flat = pos.reshape(1, NSA_CMP_LEN * dk)
        return jnp.concatenate([flat, jnp.zeros((7, NSA_CMP_LEN * dk), pos.dtype)], axis=0).astype(BF16)

    kd = half * g * dk
    const2 = lambda b: (0, 0)
    const3 = lambda b: (0, 0, 0)
    return pl.pallas_call(
        _cmp_kernel,
        grid=(batch,),
        in_specs=[pl.BlockSpec((N_CMP_PAD, kd), lambda b: (b, 0)),
                  pl.BlockSpec((N_CMP_PAD, kd), lambda b: (b, 0)),
                  pl.BlockSpec((2, kd, g * f), const3),
                  pl.BlockSpec((2, kd, g * f), const3),
                  pl.BlockSpec((8, NSA_CMP_LEN * dk), const2),
                  pl.BlockSpec((8, NSA_CMP_LEN * dk), const2),
                  pl.BlockSpec((NSA_CMP_LEN * dk, f), const2),
                  pl.BlockSpec((NSA_CMP_LEN * dk, f), const2),
                  pl.BlockSpec((g * f, g * LANES), const2),
                  pl.BlockSpec((g * f, g * LANES), const2)],
        out_specs=[pl.BlockSpec((N_CMP_PAD, g * LANES), lambda b: (b, 0))] * 2,
        out_shape=[jax.ShapeDtypeStruct((batch * N_CMP_PAD, g * LANES), BF16)] * 2,
        compiler_params=_cparams(("parallel",)),
        name="nsa_compress",
    )(zk, zv, expand_w1(ck_w1), expand_w1(cv_w1), pos_rows(ck_pos), pos_rows(cv_pos),
      ck_w1.reshape(NSA_CMP_LEN * dk, f).astype(BF16), cv_w1.reshape(NSA_CMP_LEN * dk, f).astype(BF16),
      expand_w2(ck_w2), expand_w2(cv_w2))


SLC_SHIFT = NSA_SLC_BLOCK.bit_length() - 1


def _nsa_heads(q_pair):
    lo = _lo_half(q_pair.shape)
    qf = q_pair.astype(F32)
    return jnp.where(lo, qf, 0.0), jnp.where(lo, pltpu.roll(qf, HEAD_DIM, 1), 0.0)


def _lane_column(x, c):
    return jnp.sum(jnp.where(_lane_iota(x.shape) == c, x, 0.0), axis=1, keepdims=True)


def _nsa_select_kernel(q_ref, kc_ref, vc_ref, gate_ref, ov_ref, oc_ref, bias_ref):
    seq = q_ref.shape[0]
    nq = seq // TQ
    n_slc = seq // NSA_SLC_BLOCK
    assert 1 << SLC_SHIFT == NSA_SLC_BLOCK and HEAD_DIM + n_slc <= LANES
    n_idx = _row_iota((N_CMP_PAD, TQ))
    t_idx = _lane_iota((N_CMP_PAD, TQ))
    kc = kc_ref[...]
    vc_t = _value_t(vc_ref[...])
    jb = _row_iota((n_slc, TQ))
    cmask, scores = [], []
    for i in range(nq):
        cmask.append(jnp.logical_and(n_idx * NSA_CMP_STRIDE + (NSA_CMP_LEN - 1) <= i * TQ + t_idx,
                                     n_idx < N_CMP_PAD - 1))
        heads_bf = []
        for pr in range(NSA_REP // 2):
            heads_bf.extend(h.astype(BF16) for h in
                            _nsa_heads(q_ref[i * TQ:(i + 1) * TQ, pr * LANES:(pr + 1) * LANES]))
        scores.append([jnp.where(cmask[i], _dot_nt(kc, heads_bf[r]), NEG) for r in range(NSA_REP)])
    p_sums, o_cmps = [], []
    for i in range(nq):
        p_sum = jnp.zeros((N_CMP_PAD, TQ), F32)
        o_cmp = []
        for s in scores[i]:
            e = jnp.where(cmask[i], jnp.exp2(s - jnp.max(s, axis=0, keepdims=True)), 0.0)
            p = e / jnp.maximum(jnp.sum(e, axis=0, keepdims=True), 1e-30)
            p_sum = p_sum + p
            o_cmp.append(_dot(vc_t, p.astype(BF16))[:HEAD_DIM])
        p_sums.append(p_sum)
        o_cmps.append(o_cmp)
    for i in range(nq):
        o_cmp = o_cmps[i]
        imp_t = _dot(ov_ref[...], p_sums[i], precision=lax.Precision.HIGHEST)
        back = ((i * TQ + _lane_iota((n_slc, TQ))) >> SLC_SHIFT) - jb
        ok = back >= 0
        forced = jnp.logical_or(jb == 0, jnp.logical_and(ok, back < NSA_N_LOCAL))
        score = jnp.where(ok, imp_t + jnp.where(forced, NSA_FORCE_BONUS, 0.0), NEG)
        rank = _rank_rows(score, n_slc)
        keep = jnp.logical_and(ok, rank < min(NSA_TOPN, n_slc))
        bias_ref[i * TQ:(i + 1) * TQ, :] = _bias_lanes(jnp.where(keep, 0.0, NEG), HEAD_DIM).astype(bias_ref.dtype)

        gates_t = jax.nn.sigmoid(gate_ref[i * TQ:(i + 1) * TQ, :]).T
        for pr in range(NSA_REP // 2):
            pair = jnp.concatenate([o_cmp[r] * gates_t[3 * r:3 * r + 1] for r in (2 * pr, 2 * pr + 1)], axis=0)
            oc_ref[i * TQ:(i + 1) * TQ, pr * LANES:(pr + 1) * LANES] = pair.T.astype(oc_ref.dtype)


def _nsa_attn_kernel(q_ref, ks_ref, vs_ref, kw_ref, vw_ref, bias_ref, gate_ref, oc_ref, o_ref, ksa_s, vt_s):
    seq = ks_ref.shape[0]
    nq = seq // TQ
    lo_rows = _row_iota((LANES, TQ)) < HEAD_DIM
    per_tile = TQ // NSA_SLC_BLOCK
    assert NSA_WINDOW == 2 * TQ
    pr = pl.program_id(2)
    lane = _lane_iota((TQ, LANES))
    row = _row_iota((TQ, LANES))
    lo = lane < HEAD_DIM
    diff, causal = _causal_masks()
    outs = {}

    def prep(i, h):
        if h == 0:
            for b, v_ref in enumerate((vs_ref, vw_ref)):
                vt_s[b, :, i * TQ:(i + 1) * TQ] = _value_t(_tile(v_ref, i), lo_rows)
            ind = jnp.where(lane - HEAD_DIM == i * per_tile + (row >> SLC_SHIFT), 1.0, 0.0)
            ksa_s[i * TQ:(i + 1) * TQ, :] = jnp.where(lo, _tile(ks_ref, i).astype(F32), ind).astype(BF16)

    def win_masks(i):
        return [causal] + ([None] if i >= 1 else []) + ([diff < 0] if i >= 2 else [])

    def score(i, h):
        head = _nsa_heads(_tile(q_ref, i))[h]
        q_aug = (head + _tile(bias_ref, i).astype(F32)).astype(BF16)
        return (_scores(q_aug, ksa_s, 0, [None] * i + [causal]),
                _scores(head.astype(BF16), kw_ref, i, win_masks(i), step=-1))

    def finish(i, h, state):
        n_win = len(win_masks(i))
        a_slc = _attend(state[0], [_tile_t(vt_s, j, 0) for j in range(i + 1)])
        a_win = _attend(state[1], [_tile_t(vt_s, i - t, 1) for t in range(n_win)])
        gates_t = jax.nn.sigmoid(_tile(gate_ref, i)).T
        c0 = 3 * (2 * pr + h)

        def gate(c):
            return jnp.sum(jnp.where(_row_iota(gates_t.shape) == c, gates_t, 0.0), axis=0, keepdims=True)

        outs[i, h] = (a_slc[:HEAD_DIM] * (gate(c0 + 1) / a_slc[HEAD_DIM:HEAD_DIM + 1])
                      + a_win[:HEAD_DIM] * (gate(c0 + 2) / a_win[HEAD_DIM:HEAD_DIM + 1]))
        if h == 1:
            pair = jnp.concatenate([outs[i, 0], outs[i, 1]], axis=0).T
            o_ref[i * TQ:(i + 1) * TQ, :] = (_tile(oc_ref, i).astype(F32) + pair).astype(o_ref.dtype)

    _run_pipelined([(functools.partial(prep, i, h), functools.partial(score, i, h),
                     functools.partial(finish, i, h)) for i in range(nq) for h in range(2)], PREP_LEAD)


def _nsa_overlap_t(seq):
    n_slc = seq // NSA_SLC_BLOCK
    n = jnp.arange(N_CMP_PAD)
    first = n * NSA_CMP_STRIDE
    last = first + NSA_CMP_LEN - 1
    s_start = jnp.arange(n_slc) * NSA_SLC_BLOCK
    ov = (first[None, :] <= (s_start + NSA_SLC_BLOCK - 1)[:, None]) & (last[None, :] >= s_start[:, None])
    ov = ov & (n[None, :] < (seq - NSA_CMP_LEN) // NSA_CMP_STRIDE + 1)
    return ov.astype(F32)


def _nsa_attention(q, ks, vs, kw, vw, kcmp, vcmp, gates, batch, seq):
    t = batch * seq
    g = NSA_GROUPS
    qb = NSA_REP // 2
    n_slc = seq // NSA_SLC_BLOCK
    grp = lambda: pl.BlockSpec((seq, LANES), lambda b, gg: (b, gg))
    cm = lambda: pl.BlockSpec((N_CMP_PAD, LANES), lambda b, gg: (b, gg))
    o_cmp, bias = pl.pallas_call(
        _nsa_select_kernel,
        grid=(batch, g),
        in_specs=[pl.BlockSpec((seq, qb * LANES), lambda b, gg: (b, gg)), cm(), cm(), grp(),
                  pl.BlockSpec((n_slc, N_CMP_PAD), lambda b, gg: (0, 0))],
        out_specs=[pl.BlockSpec((seq, qb * LANES), lambda b, gg: (b, gg)), grp()],
        out_shape=[jax.ShapeDtypeStruct((t, NSA_HEADS * HEAD_DIM), BF16),
                   jax.ShapeDtypeStruct((t, g * LANES), BF16)],
        compiler_params=_cparams(("parallel", "parallel")),
        name="nsa_select",
    )(q, kcmp, vcmp, gates, _nsa_overlap_t(seq))
    kv = lambda: pl.BlockSpec((seq, LANES), lambda b, gg, pr: (b, gg))
    pair = lambda: pl.BlockSpec((seq, LANES), lambda b, gg, pr: (b, gg * qb + pr))
    return pl.pallas_call(
        _nsa_attn_kernel,
        grid=(batch, g, qb),
        in_specs=[pair(), kv(), kv(), kv(), kv(), kv(), kv(), pair()],
        out_specs=pair(),
        out_shape=jax.ShapeDtypeStruct((t, NSA_HEADS * HEAD_DIM), BF16),
        scratch_shapes=[pltpu.VMEM((seq, LANES), BF16), pltpu.VMEM((2, LANES, seq), BF16)],
        compiler_params=_cparams(("parallel", "parallel", "arbitrary")),
        name="nsa_attention",
    )(q, ks, vs, kw, vw, bias, gates, o_cmp)


def _plain_blocks(n_total, n_rope, n_q):
    return [(0, b, b < n_rope, Q_SCALE if b < n_q else 1.0) for b in range(n_total)]


def _mixer_diff(x, g, w_in, lam_q1, lam_k1, lam_q2, lam_k2, subln_g, cos, sin, batch, seq, lambda_init):
    h = DIFF_HEADS
    blocks = _plain_blocks(3 * h, 2 * h, h)
    (qkv,) = _proj(x, g, w_in.astype(BF16), cos, sin, blocks, [(3 * h, BF16)], "diff_in_proj")
    pad = lambda v: jnp.pad(v.astype(F32), (0, LANES - HEAD_DIM))
    lam_rows = jnp.stack([pad(lam_q1), pad(lam_k1), pad(lam_q2), pad(lam_k2)]
                         + [jnp.zeros((LANES,), F32)] * 4)
    return _diff_attention(qkv, lam_rows, subln_g, batch, seq, lambda_init)


def _mixer_moba(x, g, w_in, cos, sin, batch, seq):
    hp = MOBA_HEADS // 2
    blocks = _plain_blocks(3 * hp, 2 * hp, hp)
    (qkv,) = _proj(x, g, w_in.astype(BF16), cos, sin, blocks, [(3 * hp, BF16)], "moba_in_proj")
    return _moba_attention(qkv, batch, seq)


def _mixer_dil(x, g, w_in, cos, sin, batch, seq):
    hp = DIL_HEADS // 2
    blocks = []
    for gi in range(len(DIL_PATTERNS)):
        for which in range(3):
            for b in range(hp):
                blocks.append((0, len(blocks), which < 2, Q_SCALE if which == 0 else 1.0))
    (qkv,) = _proj(x, g, w_in.astype(BF16), cos, sin, blocks, [(len(blocks), BF16)], "dil_in_proj")
    return _dil_attention(qkv, batch, seq)


def _mixer_nsa(x, g, w_in, ck_pos, ck_w1, ck_w2, cv_pos, cv_w1, cv_w2, cos, sin, batch, seq):
    hq, gr, dk = NSA_HEADS, NSA_GROUPS, HEAD_DIM
    kvw = gr * dk
    o_q = 0
    o_kc, o_vc, o_ks, o_vs, o_kw, o_vw, o_g = (hq * dk + n * kvw for n in range(7))
    n_gate = 3 * hq

    def dup(off):
        cols = w_in[:, off:off + kvw].reshape(D_MODEL, gr, dk)
        return jnp.concatenate([cols, cols], axis=2).reshape(D_MODEL, gr * LANES)

    per_group = n_gate // gr
    gate_cols = jnp.pad(w_in[:, o_g:o_g + n_gate].reshape(D_MODEL, gr, per_group),
                        ((0, 0), (0, 0), (0, LANES - per_group))).reshape(D_MODEL, gr * LANES)
    w = jnp.concatenate([w_in[:, o_q:o_q + hq * dk], dup(o_ks), dup(o_kw), w_in[:, o_kc:o_kc + kvw],
                         dup(o_vs), dup(o_vw), w_in[:, o_vc:o_vc + kvw], gate_cols], axis=1).astype(BF16)
    nqb = hq * dk // LANES
    blocks = [(0, b, True, Q_SCALE) for b in range(nqb)]
    blocks += [(1, b, True, 1.0) for b in range(gr)]
    blocks += [(2, b, True, 1.0) for b in range(gr)]
    blocks += [(3, b, True, 1.0) for b in range(kvw // LANES)]
    blocks += [(4, b, False, 1.0) for b in range(gr)]
    blocks += [(5, b, False, 1.0) for b in range(gr)]
    blocks += [(6, b, False, 1.0) for b in range(kvw // LANES)]
    blocks += [(7, b, False, 1.0) for b in range(gr)]
    outs = [(nqb, BF16), (gr, BF16), (gr, BF16), (kvw // LANES, BF16), (gr, BF16), (gr, BF16),
            (kvw // LANES, BF16), (gr, F32)]
    q, ks, kw, kc, vs, vw, vc, gates = _proj(x, g, w, cos, sin, blocks, outs, "nsa_in_proj")
    kcmp, vcmp = _nsa_compress(kc, vc, ck_pos, ck_w1, ck_w2, cv_pos, cv_w1, cv_w2, batch, seq)
    return _nsa_attention(q, ks, vs, kw, vw, kcmp, vcmp, gates, batch, seq)


def kernel(x, p, positions, norm_gains, ple_norm_g, ple_gate_w, ple_proj_w, ffn_w_up, ffn_conv_w, ffn_conv_b, ffn_w_down, diff_w_in, diff_w_out, diff_lam_q1, diff_lam_k1, diff_lam_q2, diff_lam_k2, diff_subln_g, nsa_w_in, nsa_w_out, nsa_ck_pos, nsa_ck_w1, nsa_ck_w2, nsa_cv_pos, nsa_cv_w1, nsa_cv_w2, moba_w_in, moba_w_out, dil_w_in, dil_w_out):
    batch, seq, d = x.shape
    t = batch * seq
    cos, sin = _rope_tables(positions)
    xt = x.reshape(t, d)
    for i in range(DEPTH):
        kind, j = i % N_MIXERS, i // N_MIXERS
        g0 = norm_gains[i, 0]
        if kind == 0:
            lambda_init = 0.8 - 0.6 * math.exp(-0.3 * i)
            o = _mixer_diff(xt, g0, diff_w_in[j], diff_lam_q1[j], diff_lam_k1[j], diff_lam_q2[j],
                            diff_lam_k2[j], diff_subln_g[j], cos, sin, batch, seq, lambda_init)
            w_out = diff_w_out[j]
        elif kind == 1:
            o = _mixer_nsa(xt, g0, nsa_w_in[j], nsa_ck_pos[j], nsa_ck_w1[j], nsa_ck_w2[j],
                           nsa_cv_pos[j], nsa_cv_w1[j], nsa_cv_w2[j], cos, sin, batch, seq)
            w_out = nsa_w_out[j]
        elif kind == 2:
            o = _mixer_moba(xt, g0, moba_w_in[j], cos, sin, batch, seq)
            w_out = moba_w_out[j]
        else:
            o = _mixer_dil(xt, g0, dil_w_in[j], cos, sin, batch, seq)
            w_out = dil_w_out[j]
        xt = _out_proj(o, w_out.astype(BF16), xt, norm_gains[i, 1], "out_proj_%d" % i)
        xt = _ffn(xt, norm_gains[i, 2], ffn_w_up[i].astype(BF16), ffn_conv_w[i], ffn_conv_b[i],
                  ffn_w_down[i].astype(BF16), norm_gains[i, 3], ple_norm_g[i],
                  ple_gate_w[i].astype(BF16), ple_proj_w[i].astype(BF16), p[i].reshape(t, PLE_DIM),
                  seq, "ffn_%d" % i)
    return xt.reshape(batch, seq, d)
```

```python
import functools
import math

import jax
import jax.numpy as jnp
from jax import lax
from jax.experimental import pallas as pl
from jax.experimental.pallas import tpu as pltpu

F32 = jnp.float32
BF16 = jnp.bfloat16

D_MODEL = 1024
DEPTH = 4
N_MIXERS = 4
HEAD_DIM = 64
ROPE_THETA = 10000.0
NORM_EPS = 1e-6
PLE_DIM = 256
NEG = -1e30

DIFF_HEADS = D_MODEL // (2 * HEAD_DIM)

NSA_HEADS = D_MODEL // HEAD_DIM
NSA_GROUPS = 4
NSA_REP = NSA_HEADS // NSA_GROUPS
NSA_CMP_LEN = 32
NSA_CMP_STRIDE = 16
NSA_CMP_HIDDEN = 128
NSA_SLC_BLOCK = 64
NSA_TOPN = 16
NSA_N_LOCAL = 2
NSA_WINDOW = 512
NSA_FORCE_BONUS = 1e4

MOBA_HEADS = D_MODEL // HEAD_DIM
MOBA_BLOCK = 256
MOBA_TOPK = 3

DIL_PATTERNS = ((128, 1), (512, 4), (2048, 16))
DIL_HEADS = 8

D_FF = 2816
CONV_WIDTH = 3

LANES = 128
MXU_COLS = 256
TQ = 256
LOG2E = 1.4426950408889634
Q_SCALE = (HEAD_DIM ** -0.5) * LOG2E
VMEM_LIMIT = 56 * 1024 * 1024


def _cparams(sem):
    return pltpu.CompilerParams(dimension_semantics=sem, vmem_limit_bytes=VMEM_LIMIT)


def _rms(x, g):
    return x * lax.rsqrt(jnp.mean(x * x, axis=-1, keepdims=True) + NORM_EPS) * g


def _dot(a, b, precision=None):
    return jnp.dot(a, b, preferred_element_type=F32, precision=precision)


def _dot_nt(a, b, precision=None):
    return lax.dot_general(a, b, (((1,), (1,)), ((), ())), preferred_element_type=F32,
                           precision=precision)


def _lane_iota(shape):
    return lax.broadcasted_iota(jnp.int32, shape, 1)


def _row_iota(shape):
    return lax.broadcasted_iota(jnp.int32, shape, 0)


def _rope_kernel(pos_ref, inv_ref, cos_ref, sin_ref):
    ang = pos_ref[...].astype(F32) * inv_ref[...]
    lane = _lane_iota(ang.shape)
    cos_ref[...] = jnp.cos(ang)
    sin_ref[...] = jnp.where((lane & 63) < 32, -jnp.sin(ang), jnp.sin(ang))


def _rope_tables(positions):
    t = positions.size
    inv = ROPE_THETA ** (-jnp.arange(0, HEAD_DIM, 2, dtype=F32) / HEAD_DIM)
    inv = jnp.tile(inv, 4)[None, :]
    tm = 2048
    return pl.pallas_call(
        _rope_kernel,
        grid=(t // tm,),
        in_specs=[pl.BlockSpec((tm, 1), lambda i: (i, 0)),
                  pl.BlockSpec((1, LANES), lambda i: (0, 0))],
        out_specs=[pl.BlockSpec((tm, LANES), lambda i: (i, 0))] * 2,
        out_shape=[jax.ShapeDtypeStruct((t, LANES), F32)] * 2,
        compiler_params=_cparams(("parallel",)),
        name="rope_tables",
    )(positions.reshape(t, 1), inv)


def _proj_kernel(x_ref, g_ref, w_ref, cos_ref, sin_ref, *o_refs, blocks):
    hn = _rms(x_ref[...], g_ref[...]).astype(BF16)
    cos = cos_ref[...]
    sin = sin_ref[...]
    first = (_lane_iota(cos.shape) & 63) < 32
    n_chunks = len(blocks) // 2
    for c in range(n_chunks):
        acc = _dot(hn, w_ref[:, c * MXU_COLS:(c + 1) * MXU_COLS])
        for s in range(2):
            out_idx, out_blk, rope, scale = blocks[2 * c + s]
            if out_idx is None:
                continue
            a = acc[:, s * LANES:(s + 1) * LANES]
            if rope:
                rot = jnp.where(first, pltpu.roll(a, 96, 1), pltpu.roll(a, 32, 1))
                a = a * cos + rot * sin
            if scale != 1.0:
                a = a * scale
            o_ref = o_refs[out_idx]
            o_ref[:, out_blk * LANES:(out_blk + 1) * LANES] = a.astype(o_ref.dtype)


def _proj(x, g, w, cos, sin, blocks, outs, name):
    t = x.shape[0]
    n = w.shape[1]
    assert n == len(blocks) * LANES and len(blocks) % 2 == 0
    tm = 512
    return pl.pallas_call(
        functools.partial(_proj_kernel, blocks=tuple(blocks)),
        grid=(t // tm,),
        in_specs=[pl.BlockSpec((tm, D_MODEL), lambda i: (i, 0)),
                  pl.BlockSpec((1, D_MODEL), lambda i: (0, 0)),
                  pl.BlockSpec((D_MODEL, n), lambda i: (0, 0)),
                  pl.BlockSpec((tm, LANES), lambda i: (i, 0)),
                  pl.BlockSpec((tm, LANES), lambda i: (i, 0))],
        out_specs=[pl.BlockSpec((tm, nb * LANES), lambda i: (i, 0)) for nb, _ in outs],
        out_shape=[jax.ShapeDtypeStruct((t, nb * LANES), dt) for nb, dt in outs],
        compiler_params=_cparams(("parallel",)),
        name=name,
    )(x, g.reshape(1, D_MODEL), w, cos, sin)


def _out_kernel(o_ref, w_ref, x_ref, g_ref, y_ref):
    y = _dot(o_ref[...], w_ref[...])
    y_ref[...] = x_ref[...] + _rms(y, g_ref[...])


def _out_proj(o, w, x, g, name):
    t, k = o.shape
    tm = 1024
    return pl.pallas_call(
        _out_kernel,
        grid=(t // tm,),
        in_specs=[pl.BlockSpec((tm, k), lambda i: (i, 0)),
                  pl.BlockSpec((k, D_MODEL), lambda i: (0, 0)),
                  pl.BlockSpec((tm, D_MODEL), lambda i: (i, 0)),
                  pl.BlockSpec((1, D_MODEL), lambda i: (0, 0))],
        out_specs=pl.BlockSpec((tm, D_MODEL), lambda i: (i, 0)),
        out_shape=jax.ShapeDtypeStruct((t, D_MODEL), F32),
        compiler_params=_cparams(("parallel",)),
        name=name,
    )(o, w, x, g.reshape(1, D_MODEL))


FF_CHUNK = 256
FF_TM = 512
FF_HALO = 16


def _gelu_tanh(x):
    k = math.sqrt(2.0 / math.pi)
    h = 0.5 * x
    return h + h * jnp.tanh(x * (k + (k * 0.044715) * (x * x)))


def _ffn_kernel(x_ref, xh_ref, g2_ref, wu_ref, cw_ref, cb_ref, wd_ref, g3_ref, pg_ref, pgw_ref, ppw_ref,
                p_ref, y_ref, hn_s, *, seq):
    i = pl.program_id(0)
    tm = x_ref.shape[0]
    n_c = D_FF // FF_CHUNK
    x = x_ref[...]
    hn_s[FF_HALO:, :] = _rms(x, g2_ref[...]).astype(BF16)
    halo = _rms(xh_ref[...], g2_ref[...])
    seq_start = (i * tm) % seq == 0
    hn_s[:FF_HALO, :] = jnp.where(seq_start, 0.0, halo).astype(BF16)

    def conv(u, col):
        cols = slice(col, col + FF_CHUNK)
        out = cb_ref[:, cols] + pltpu.roll(u, 2, 0)[FF_HALO:] * cw_ref[0:1, cols]
        out = out + pltpu.roll(u, 1, 0)[FF_HALO:] * cw_ref[1:2, cols]
        return out + u[FF_HALO:] * cw_ref[2:3, cols]

    def up(c, r):
        hn = hn_s[r * FF_ROWS:(r + 1) * FF_ROWS + FF_HALO, :]
        gcol, vcol = c * FF_CHUNK, D_FF + c * FF_CHUNK
        return (_dot(hn, wu_ref[:, gcol:gcol + FF_CHUNK]), _dot(hn, wu_ref[:, vcol:vcol + FF_CHUNK]))

    def down(c, act, acc):
        d = _dot(act, wd_ref[c * FF_CHUNK:(c + 1) * FF_CHUNK, :])
        return d if acc is None else acc + d

    units = [(c, r) for c in range(n_c) for r in range(tm // FF_ROWS)]
    accs = [None] * (tm // FF_ROWS)
    ups = {k: up(*units[k]) for k in range(FF_UP_LEAD)}
    pending = None
    for k, (c, r) in enumerate(units):
        ug, uv = ups.pop(k)
        if k + FF_UP_LEAD < len(units):
            ups[k + FF_UP_LEAD] = up(*units[k + FF_UP_LEAD])
        act = (_gelu_tanh(conv(ug, c * FF_CHUNK)) * conv(uv, D_FF + c * FF_CHUNK)).astype(BF16)
        if pending is not None:
            accs[pending[1]] = down(pending[0], pending[2], accs[pending[1]])
        pending = (c, r, act)
    accs[pending[1]] = down(pending[0], pending[2], accs[pending[1]])

    x2 = x + _rms(jnp.concatenate(accs, axis=0), g3_ref[...])
    gate = jax.nn.sigmoid(_dot(_rms(x2, pg_ref[...]).astype(BF16), pgw_ref[...]))
    y_ref[...] = x2 + gate * _dot(p_ref[...].astype(BF16), ppw_ref[...])


FF_ROWS = 256
FF_UP_LEAD = 2


def _ffn(x, g2, w_up, conv_w, conv_b, w_down, g3, ple_g, ple_gate_w, ple_proj_w, p, seq, name):
    t = x.shape[0]
    tm = FF_TM
    assert seq % tm == 0 and tm % FF_ROWS == 0
    hb = tm // FF_HALO
    row = lambda v: v.reshape(1, -1)
    const = lambda i: (0, 0)
    resident = lambda shape: pl.BlockSpec(shape, const, pipeline_mode=pl.Buffered(1))
    return pl.pallas_call(
        functools.partial(_ffn_kernel, seq=seq),
        grid=(t // tm,),
        in_specs=[pl.BlockSpec((tm, D_MODEL), lambda i: (i, 0)),
                  pl.BlockSpec((FF_HALO, D_MODEL), lambda i: (jnp.maximum(i * hb - 1, 0), 0)),
                  pl.BlockSpec((1, D_MODEL), const),
                  resident((D_MODEL, 2 * D_FF)),
                  pl.BlockSpec((CONV_WIDTH, 2 * D_FF), const),
                  pl.BlockSpec((1, 2 * D_FF), const),
                  resident((D_FF, D_MODEL)),
                  pl.BlockSpec((1, D_MODEL), const),
                  pl.BlockSpec((1, D_MODEL), const),
                  resident((D_MODEL, D_MODEL)),
                  resident((PLE_DIM, D_MODEL)),
                  pl.BlockSpec((tm, PLE_DIM), lambda i: (i, 0))],
        out_specs=pl.BlockSpec((tm, D_MODEL), lambda i: (i, 0)),
        out_shape=jax.ShapeDtypeStruct((t, D_MODEL), F32),
        scratch_shapes=[pltpu.VMEM((tm + FF_HALO, D_MODEL), BF16)],
        compiler_params=_cparams(("parallel",)),
        name=name,
    )(x, x, row(g2), w_up, conv_w, row(conv_b), w_down, row(g3), row(ple_g), ple_gate_w, ple_proj_w, p)


def _lo_half(shape):
    return _lane_iota(shape) < HEAD_DIM


def _split_heads(q):
    lo = _lo_half(q.shape)
    qf = q.astype(F32)
    return jnp.where(lo, qf, 0.0).astype(q.dtype), jnp.where(lo, 0.0, qf).astype(q.dtype)


def _tile(ref, j):
    return ref[j * TQ:(j + 1) * TQ, :]


def _scores(q_op, k_ref, first, masks, step=1):
    s_list = []
    for t, mask in enumerate(masks):
        s = _dot_nt(_tile(k_ref, first + t * step), q_op)
        s_list.append(s if mask is None else jnp.where(mask, s, NEG))
    return s_list


def _attend(s_list, values_t):
    mt = None
    for s in s_list:
        mt = s if mt is None else jnp.maximum(mt, s)
    m = jnp.max(mt, axis=0, keepdims=True)
    acc = None
    for s, v_t in zip(s_list, values_t):
        d = _dot(v_t, jnp.exp2(s - m).astype(BF16))
        acc = d if acc is None else acc + d
    return acc


def _value_t(v_tile, keep=None):
    v_t = v_tile.astype(F32).T
    return (v_t if keep is None else jnp.where(keep, v_t, 1.0)).astype(BF16)


def _store_pair_values(v_ref, vt_s, j, *idx):
    lo_rows = _row_iota((LANES, TQ)) < HEAD_DIM
    v_t = _tile(v_ref, j).astype(F32).T
    cols = slice(j * TQ, (j + 1) * TQ)
    vt_s[idx + (0, slice(None), cols)] = jnp.where(lo_rows, v_t, 1.0).astype(BF16)
    vt_s[idx + (1, slice(None), cols)] = jnp.where(lo_rows, 1.0, v_t).astype(BF16)


def _pair_output(acc0, acc1):
    out0 = acc0[:HEAD_DIM] / acc0[HEAD_DIM:HEAD_DIM + 1]
    out1 = acc1[HEAD_DIM:] / acc1[0:1]
    return jnp.concatenate([out0, out1], axis=0).T


ONES_ROWS = 16


def _tile_t(vt_s, j, *idx):
    return vt_s[idx + (slice(None), slice(j * TQ, (j + 1) * TQ))]


def _run_pipelined(stages, lead=None):
    n = len(stages)
    if lead is None:
        for prep_fn, _, _ in stages:
            if prep_fn is not None:
                prep_fn()
        lead = 0
    states = {}
    for step in range(n + lead + 1):
        if lead and step < n and stages[step][0] is not None:
            stages[step][0]()
        if lead <= step < n + lead:
            states[step - lead] = stages[step - lead][1]()
        if step >= lead + 1:
            stages[step - lead - 1][2](states.pop(step - lead - 1))


PREP_LEAD = 5


def _causal_masks():
    diff = _lane_iota((TQ, TQ)) - _row_iota((TQ, TQ))
    return diff, diff >= 0


def _rank_rows(score, n_valid):
    ridx = _row_iota(score.shape)
    cnt = jnp.zeros(score.shape, F32)
    for j in range(n_valid):
        other = score[j:j + 1, :]
        cnt = cnt + jnp.where(other > score, 1.0,
                              jnp.where(other == score, jnp.where(ridx > j, 1.0, 0.0), 0.0))
    return cnt


def _bias_lanes(bias_t, first_lane):
    r, tq = bias_t.shape
    parts = []
    if first_lane:
        parts.append(jnp.zeros((first_lane, tq), F32))
    parts.append(bias_t)
    if LANES - first_lane - r:
        parts.append(jnp.zeros((LANES - first_lane - r, tq), F32))
    return jnp.concatenate(parts, axis=0).T


def _diff_kernel(q_ref, k_ref, v_ref, lam_ref, sg_ref, o_ref, vt_s, *, lambda_init):
    nq = q_ref.shape[0] // TQ
    _, causal = _causal_masks()
    lamv = lam_ref[...]
    lam = (jnp.exp(jnp.sum(lamv[0:1] * lamv[1:2], axis=1, keepdims=True))
           - jnp.exp(jnp.sum(lamv[2:3] * lamv[3:4], axis=1, keepdims=True)) + lambda_init)
    sg = sg_ref[...]

    def prep(i):
        vt_s[:LANES, i * TQ:(i + 1) * TQ] = _value_t(_tile(v_ref, i))
        vt_s[LANES:, i * TQ:(i + 1) * TQ] = jnp.ones((ONES_ROWS, TQ), BF16)

    def score(i):
        q0, q1 = _split_heads(_tile(q_ref, i))
        masks = [None] * i + [causal]
        return _scores(q0, k_ref, 0, masks), _scores(q1, k_ref, 0, masks)

    def finish(i, state):
        v_tiles = [_tile_t(vt_s, j) for j in range(i + 1)]
        a0 = _attend(state[0], v_tiles)
        a1 = _attend(state[1], v_tiles)
        o_t = a0[:LANES] / a0[LANES:LANES + 1] - lam * (a1[:LANES] / a1[LANES:LANES + 1])
        o_ref[i * TQ:(i + 1) * TQ, :] = (_rms(o_t.T, sg) * (1.0 - lambda_init)).astype(o_ref.dtype)

    _run_pipelined([(functools.partial(prep, i), functools.partial(score, i), functools.partial(finish, i))
                    for i in range(nq)], PREP_LEAD)


def _diff_attention(qkv, lam_rows, subln_g, batch, seq, lambda_init):
    t = batch * seq
    h = DIFF_HEADS
    return pl.pallas_call(
        functools.partial(_diff_kernel, lambda_init=lambda_init),
        grid=(batch, h),
        in_specs=[pl.BlockSpec((seq, LANES), lambda b, hh: (b, hh)),
                  pl.BlockSpec((seq, LANES), lambda b, hh: (b, h + hh)),
                  pl.BlockSpec((seq, LANES), lambda b, hh: (b, 2 * h + hh)),
                  pl.BlockSpec((8, LANES), lambda b, hh: (0, 0)),
                  pl.BlockSpec((1, LANES), lambda b, hh: (0, 0))],
        out_specs=pl.BlockSpec((seq, LANES), lambda b, hh: (b, hh)),
        out_shape=jax.ShapeDtypeStruct((t, h * LANES), BF16),
        scratch_shapes=[pltpu.VMEM((LANES + ONES_ROWS, seq), BF16)],
        compiler_params=_cparams(("parallel", "parallel")),
        name="diff_attention",
    )(qkv, qkv, qkv, lam_rows, subln_g.reshape(1, LANES))


def _moba_kernel(q_ref, k_ref, v_ref, o_ref, ka_s, vt_s):
    seq = k_ref.shape[0]
    nb = seq // MOBA_BLOCK
    lane = _lane_iota((TQ, LANES))
    lo = lane < HEAD_DIM
    _, causal = _causal_masks()
    km = jnp.concatenate([jnp.mean(_tile(k_ref, j).astype(F32), axis=0, keepdims=True) for j in range(nb)],
                         axis=0)
    q_aug = {}
    outs = {}

    def prep(i, a):
        if a == 0:
            _store_pair_values(v_ref, vt_s, i)
            kf = _tile(k_ref, i).astype(F32)
            ka_s[0, i * TQ:(i + 1) * TQ, :] = jnp.where(lo, kf, jnp.where(lane == HEAD_DIM + i, 1.0, 0.0)
                                                        ).astype(BF16)
            ka_s[1, i * TQ:(i + 1) * TQ, :] = jnp.where(lo, jnp.where(lane == i, 1.0, 0.0), kf).astype(BF16)
        head = _split_heads(_tile(q_ref, i))[a]
        q_aug[i, a] = head
        if i > 0:
            gate_t = _dot_nt(km, head.astype(F32), precision=lax.Precision.HIGHEST)
            rank = _rank_rows(gate_t, i)
            drop = jnp.logical_and(_row_iota(gate_t.shape) < i, rank >= MOBA_TOPK)
            bias = _bias_lanes(jnp.where(drop, NEG, 0.0), HEAD_DIM * (1 - a))
            q_aug[i, a] = (head.astype(F32) + bias).astype(BF16)

    def score(i, a):
        return _scores(q_aug[i, a], ka_s.at[a], 0, [None] * i + [causal])

    def finish(i, a, s_list):
        outs[i, a] = _attend(s_list, [_tile_t(vt_s, j, a) for j in range(i + 1)])
        if a == 1:
            o_ref[i * TQ:(i + 1) * TQ, :] = _pair_output(outs[i, 0], outs[i, 1]).astype(o_ref.dtype)

    _run_pipelined([(functools.partial(prep, i, a), functools.partial(score, i, a),
                     functools.partial(finish, i, a)) for i in range(nb) for a in range(2)], PREP_LEAD)


def _moba_attention(qkv, batch, seq):
    assert MOBA_BLOCK == TQ and seq % MOBA_BLOCK == 0 and seq // MOBA_BLOCK <= HEAD_DIM
    t = batch * seq
    hp = MOBA_HEADS // 2
    return pl.pallas_call(
        _moba_kernel,
        grid=(batch, hp),
        in_specs=[pl.BlockSpec((seq, LANES), lambda b, hh: (b, hh)),
                  pl.BlockSpec((seq, LANES), lambda b, hh: (b, hp + hh)),
                  pl.BlockSpec((seq, LANES), lambda b, hh: (b, 2 * hp + hh))],
        out_specs=pl.BlockSpec((seq, LANES), lambda b, hh: (b, hh)),
        out_shape=jax.ShapeDtypeStruct((t, hp * LANES), BF16),
        scratch_shapes=[pltpu.VMEM((2, seq, LANES), BF16), pltpu.VMEM((2, LANES, seq), BF16)],
        compiler_params=_cparams(("parallel", "parallel")),
        name="moba_attention",
    )(qkv, qkv, qkv)


def _dil_kernel(q0_ref, q1_ref, q2_ref, k0_ref, k1_ref, k2_ref, v0_ref, v1_ref, v2_ref, o_ref, vt_s):
    seq = k0_ref.shape[0]
    nq = seq // TQ
    (w0, _), (w1, d1), (w2, d2) = DIL_PATTERNS
    assert w0 < TQ and w1 == 2 * TQ and w2 >= seq
    diff, causal = _causal_masks()
    on1 = (diff & (d1 - 1)) == 0
    on2 = (diff & (d2 - 1)) == 0
    m0_diag = jnp.logical_and(causal, diff <= w0)
    m0_prev = diff + TQ <= w0
    m1_diag = jnp.logical_and(causal, on1)
    m1_prev2 = jnp.logical_and(on1, diff <= 0)
    m2_diag = jnp.logical_and(causal, on2)
    lo = _lo_half((TQ, LANES))
    k_refs = (k0_ref, k1_ref, k2_ref)
    q_refs = (q0_ref, q1_ref, q2_ref)
    outs = {}

    def runs(i):
        g0 = ([m0_prev] if i >= 1 else []) + [m0_diag]
        g1 = ([m1_prev2] if i >= 2 else []) + ([on1] if i >= 1 else []) + [m1_diag]
        g2 = [on2] * i + [m2_diag]
        return [(g, i + 1 - len(masks), masks) for g, masks in enumerate((g0, g1, g2))]

    def score(i, a):
        s_list = []
        for g, first, masks in runs(i):
            s_list += _scores(_split_heads(_tile(q_refs[g], i))[a], k_refs[g], first, masks)
        return s_list

    def finish(i, a, s_list):
        v_tiles = [_tile_t(vt_s, first + t, g, a) for g, first, masks in runs(i) for t in range(len(masks))]
        outs[i, a] = _attend(s_list, v_tiles)
        if a == 1:
            o_ref[i * TQ:(i + 1) * TQ, :] = _pair_output(outs[i, 0], outs[i, 1]).astype(o_ref.dtype)

    def prep(i, a):
        if a == 0:
            for g, v_ref in enumerate((v0_ref, v1_ref, v2_ref)):
                _store_pair_values(v_ref, vt_s, i, g)

    _run_pipelined([(functools.partial(prep, i, a), functools.partial(score, i, a),
                     functools.partial(finish, i, a)) for i in reversed(range(nq)) for a in range(2)])


def _dil_attention(qkv, batch, seq):
    t = batch * seq
    hp = DIL_HEADS // 2
    per_g = 3 * hp

    def spec(g, which):
        return pl.BlockSpec((seq, LANES), lambda b, hh: (b, g * per_g + which * hp + hh))

    return pl.pallas_call(
        _dil_kernel,
        grid=(batch, hp),
        in_specs=[spec(0, 0), spec(1, 0), spec(2, 0), spec(0, 1), spec(1, 1), spec(2, 1),
                  spec(0, 2), spec(1, 2), spec(2, 2)],
        out_specs=pl.BlockSpec((seq, LANES), lambda b, hh: (b, hh)),
        out_shape=jax.ShapeDtypeStruct((t, hp * LANES), BF16),
        scratch_shapes=[pltpu.VMEM((len(DIL_PATTERNS), 2, LANES, seq), BF16)],
        compiler_params=_cparams(("parallel", "parallel")),
        name="dilated_attention",
    )(*([qkv] * 9))


N_CMP_PAD = 128


def _cmp_kernel(zk_ref, zv_ref, w1k_ref, w1v_ref, pk_ref, pv_ref, w1fk_ref, w1fv_ref, w2k_ref, w2v_ref,
                ko_ref, vo_ref):
    def one(z_ref, w1_ref, pos_ref, w1f_ref, w2_ref, o_ref):
        z = z_ref[...]
        a = _dot(z, w1_ref[0])
        b = _dot(z, w1_ref[1])
        b_next = jnp.concatenate([b[1:], jnp.zeros_like(b[:1])], axis=0)
        const = _dot(pos_ref[...], w1f_ref[...])[0:1]
        hid = a + b_next + jnp.concatenate([const] * NSA_GROUPS, axis=1)
        o_ref[...] = _dot(jax.nn.gelu(hid, approximate=True).astype(BF16), w2_ref[...]).astype(o_ref.dtype)

    one(zk_ref, w1k_ref, pk_ref, w1fk_ref, w2k_ref, ko_ref)
    one(zv_ref, w1v_ref, pv_ref, w1fv_ref, w2v_ref, vo_ref)


def _nsa_compress(kc, vc, ck_pos, ck_w1, ck_w2, cv_pos, cv_w1, cv_w2, batch, seq):
    g, dk, f = NSA_GROUPS, HEAD_DIM, NSA_CMP_HIDDEN
    half = NSA_CMP_LEN // 2
    assert NSA_CMP_STRIDE == half and seq // half == N_CMP_PAD
    zk = kc.reshape(batch * N_CMP_PAD, half * g * dk)
    zv = vc.reshape(batch * N_CMP_PAD, half * g * dk)

    def expand_w1(w1):
        w = w1.reshape(2, half, dk, f)
        eye = jnp.eye(g, dtype=w1.dtype)
        big = jnp.einsum('acdf,gh->acgdhf', w, eye)
        return big.reshape(2, half * g * dk, g * f).astype(BF16)

    def expand_w2(w2):
        dup = jnp.concatenate([w2, w2], axis=1)
        eye = jnp.eye(g, dtype=w2.dtype)
        return jnp.einsum('fd,gh->gfhd', dup, eye).reshape(g * f, g * LANES).astype(BF16)

    def pos_rows(pos):
        flat = pos.reshape(1, NSA_CMP_LEN * dk)
        return jnp.concatenate([flat, jnp.zeros((7, NSA_CMP_LEN * dk), pos.dtype)], axis=0).astype(BF16)

    kd = half * g * dk
    const2 = lambda b: (0, 0)
    const3 = lambda b: (0, 0, 0)
    return pl.pallas_call(
        _cmp_kernel,
        grid=(batch,),
        in_specs=[pl.BlockSpec((N_CMP_PAD, kd), lambda b: (b, 0)),
                  pl.BlockSpec((N_CMP_PAD, kd), lambda b: (b, 0)),
                  pl.BlockSpec((2, kd, g * f), const3),
                  pl.BlockSpec((2, kd, g * f), const3),
                  pl.BlockSpec((8, NSA_CMP_LEN * dk), const2),
                  pl.BlockSpec((8, NSA_CMP_LEN * dk), const2),
                  pl.BlockSpec((NSA_CMP_LEN * dk, f), const2),
                  pl.BlockSpec((NSA_CMP_LEN * dk, f), const2),
                  pl.BlockSpec((g * f, g * LANES), const2),
                  pl.BlockSpec((g * f, g * LANES), const2)],
        out_specs=[pl.BlockSpec((N_CMP_PAD, g * LANES), lambda b: (b, 0))] * 2,
        out_shape=[jax.ShapeDtypeStruct((batch * N_CMP_PAD, g * LANES), BF16)] * 2,
        compiler_params=_cparams(("parallel",)),
        name="nsa_compress",
    )(zk, zv, expand_w1(ck_w1), expand_w1(cv_w1), pos_rows(ck_pos), pos_rows(cv_pos),
      ck_w1.reshape(NSA_CMP_LEN * dk, f).astype(BF16), cv_w1.reshape(NSA_CMP_LEN * dk, f).astype(BF16),
      expand_w2(ck_w2), expand_w2(cv_w2))


SLC_SHIFT = NSA_SLC_BLOCK.bit_length() - 1


def _nsa_heads(q_pair):
    lo = _lo_half(q_pair.shape)
    qf = q_pair.astype(F32)
    return jnp.where(lo, qf, 0.0), jnp.where(lo, pltpu.roll(qf, HEAD_DIM, 1), 0.0)


def _lane_column(x, c):
    return jnp.sum(jnp.where(_lane_iota(x.shape) == c, x, 0.0), axis=1, keepdims=True)


def _nsa_select_kernel(q_ref, kc_ref, vc_ref, gate_ref, ov_ref, oc_ref, bias_ref):
    seq = q_ref.shape[0]
    nq = seq // TQ
    n_slc = seq // NSA_SLC_BLOCK
    assert 1 << SLC_SHIFT == NSA_SLC_BLOCK and HEAD_DIM + n_slc <= LANES
    n_idx = _row_iota((N_CMP_PAD, TQ))
    t_idx = _lane_iota((N_CMP_PAD, TQ))
    kc = kc_ref[...]
    vc_t = _value_t(vc_ref[...])
    jb = _row_iota((n_slc, TQ))
    cmask, scores = [], []
    for i in range(nq):
        cmask.append(jnp.logical_and(n_idx * NSA_CMP_STRIDE + (NSA_CMP_LEN - 1) <= i * TQ + t_idx,
                                     n_idx < N_CMP_PAD - 1))
        heads_bf = []
        for pr in range(NSA_REP // 2):
            heads_bf.extend(h.astype(BF16) for h in
                            _nsa_heads(q_ref[i * TQ:(i + 1) * TQ, pr * LANES:(pr + 1) * LANES]))
        scores.append([jnp.where(cmask[i], _dot_nt(kc, heads_bf[r]), NEG) for r in range(NSA_REP)])
    p_sums, o_cmps = [], []
    for i in range(nq):
        p_sum = jnp.zeros((N_CMP_PAD, TQ), F32)
        o_cmp = []
        for s in scores[i]:
            e = jnp.where(cmask[i], jnp.exp2(s - jnp.max(s, axis=0, keepdims=True)), 0.0)
            p = e / jnp.maximum(jnp.sum(e, axis=0, keepdims=True), 1e-30)
            p_sum = p_sum + p
            o_cmp.append(_dot(vc_t, p.astype(BF16))[:HEAD_DIM])
        p_sums.append(p_sum)
        o_cmps.append(o_cmp)
    for i in range(nq):
        o_cmp = o_cmps[i]
        imp_t = _dot(ov_ref[...], p_sums[i], precision=lax.Precision.HIGHEST)
        back = ((i * TQ + _lane_iota((n_slc, TQ))) >> SLC_SHIFT) - jb
        ok = back >= 0
        forced = jnp.logical_or(jb == 0, jnp.logical_and(ok, back < NSA_N_LOCAL))
        score = jnp.where(ok, imp_t + jnp.where(forced, NSA_FORCE_BONUS, 0.0), NEG)
        rank = _rank_rows(score, n_slc)
        keep = jnp.logical_and(ok, rank < min(NSA_TOPN, n_slc))
        bias_ref[i * TQ:(i + 1) * TQ, :] = _bias_lanes(jnp.where(keep, 0.0, NEG), HEAD_DIM).astype(bias_ref.dtype)

        gates_t = jax.nn.sigmoid(gate_ref[i * TQ:(i + 1) * TQ, :]).T
        for pr in range(NSA_REP // 2):
            pair = jnp.concatenate([o_cmp[r] * gates_t[3 * r:3 * r + 1] for r in (2 * pr, 2 * pr + 1)], axis=0)
            oc_ref[i * TQ:(i + 1) * TQ, pr * LANES:(pr + 1) * LANES] = pair.T.astype(oc_ref.dtype)


def _nsa_attn_kernel(q_ref, ks_ref, vs_ref, kw_ref, vw_ref, bias_ref, gate_ref, oc_ref, o_ref, ksa_s, vt_s):
    seq = ks_ref.shape[0]
    nq = seq // TQ
    lo_rows = _row_iota((LANES, TQ)) < HEAD_DIM
    per_tile = TQ // NSA_SLC_BLOCK
    assert NSA_WINDOW == 2 * TQ
    pr = pl.program_id(2)
    lane = _lane_iota((TQ, LANES))
    row = _row_iota((TQ, LANES))
    lo = lane < HEAD_DIM
    diff, causal = _causal_masks()
    outs = {}

    def prep(i, h):
        if h == 0:
            for b, v_ref in enumerate((vs_ref, vw_ref)):
                vt_s[b, :, i * TQ:(i + 1) * TQ] = _value_t(_tile(v_ref, i), lo_rows)
            ind = jnp.where(lane - HEAD_DIM == i * per_tile + (row >> SLC_SHIFT), 1.0, 0.0)
            ksa_s[i * TQ:(i + 1) * TQ, :] = jnp.where(lo, _tile(ks_ref, i).astype(F32), ind).astype(BF16)

    def win_masks(i):
        return [causal] + ([None] if i >= 1 else []) + ([diff < 0] if i >= 2 else [])

    def score(i, h):
        head = _nsa_heads(_tile(q_ref, i))[h]
        q_aug = (head + _tile(bias_ref, i).astype(F32)).astype(BF16)
        return (_scores(q_aug, ksa_s, 0, [None] * i + [causal]),
                _scores(head.astype(BF16), kw_ref, i, win_masks(i), step=-1))

    def finish(i, h, state):
        n_win = len(win_masks(i))
        a_slc = _attend(state[0], [_tile_t(vt_s, j, 0) for j in range(i + 1)])
        a_win = _attend(state[1], [_tile_t(vt_s, i - t, 1) for t in range(n_win)])
        gates_t = jax.nn.sigmoid(_tile(gate_ref, i)).T
        c0 = 3 * (2 * pr + h)

        def gate(c):
            return jnp.sum(jnp.where(_row_iota(gates_t.shape) == c, gates_t, 0.0), axis=0, keepdims=True)

        outs[i, h] = (a_slc[:HEAD_DIM] * (gate(c0 + 1) / a_slc[HEAD_DIM:HEAD_DIM + 1])
                      + a_win[:HEAD_DIM] * (gate(c0 + 2) / a_win[HEAD_DIM:HEAD_DIM + 1]))
        if h == 1:
            pair = jnp.concatenate([outs[i, 0], outs[i, 1]], axis=0).T
            o_ref[i * TQ:(i + 1) * TQ, :] = (_tile(oc_ref, i).astype(F32) + pair).astype(o_ref.dtype)

    _run_pipelined([(functools.partial(prep, i, h), functools.partial(score, i, h),
                     functools.partial(finish, i, h)) for i in range(nq) for h in range(2)], PREP_LEAD)


def _nsa_overlap_t(seq):
    n_slc = seq // NSA_SLC_BLOCK
    n = jnp.arange(N_CMP_PAD)
    first = n * NSA_CMP_STRIDE
    last = first + NSA_CMP_LEN - 1
    s_start = jnp.arange(n_slc) * NSA_SLC_BLOCK
    ov = (first[None, :] <= (s_start + NSA_SLC_BLOCK - 1)[:, None]) & (last[None, :] >= s_start[:, None])
    ov = ov & (n[None, :] < (seq - NSA_CMP_LEN) // NSA_CMP_STRIDE + 1)
    return ov.astype(F32)


def _nsa_attention(q, ks, vs, kw, vw, kcmp, vcmp, gates, batch, seq):
    t = batch * seq
    g = NSA_GROUPS
    qb = NSA_REP // 2
    n_slc = seq // NSA_SLC_BLOCK
    grp = lambda: pl.BlockSpec((seq, LANES), lambda b, gg: (b, gg))
    cm = lambda: pl.BlockSpec((N_CMP_PAD, LANES), lambda b, gg: (b, gg))
    o_cmp, bias = pl.pallas_call(
        _nsa_select_kernel,
        grid=(batch, g),
        in_specs=[pl.BlockSpec((seq, qb * LANES), lambda b, gg: (b, gg)), cm(), cm(), grp(),
                  pl.BlockSpec((n_slc, N_CMP_PAD), lambda b, gg: (0, 0))],
        out_specs=[pl.BlockSpec((seq, qb * LANES), lambda b, gg: (b, gg)), grp()],
        out_shape=[jax.ShapeDtypeStruct((t, NSA_HEADS * HEAD_DIM), BF16),
                   jax.ShapeDtypeStruct((t, g * LANES), BF16)],
        compiler_params=_cparams(("parallel", "parallel")),
        name="nsa_select",
    )(q, kcmp, vcmp, gates, _nsa_overlap_t(seq))
    kv = lambda: pl.BlockSpec((seq, LANES), lambda b, gg, pr: (b, gg))
    pair = lambda: pl.BlockSpec((seq, LANES), lambda b, gg, pr: (b, gg * qb + pr))
    return pl.pallas_call(
        _nsa_attn_kernel,
        grid=(batch, g, qb),
        in_specs=[pair(), kv(), kv(), kv(), kv(), kv(), kv(), pair()],
        out_specs=pair(),
        out_shape=jax.ShapeDtypeStruct((t, NSA_HEADS * HEAD_DIM), BF16),
        scratch_shapes=[pltpu.VMEM((seq, LANES), BF16), pltpu.VMEM((2, LANES, seq), BF16)],
        compiler_params=_cparams(("parallel", "parallel", "arbitrary")),
        name="nsa_attention",
    )(q, ks, vs, kw, vw, bias, gates, o_cmp)


def _plain_blocks(n_total, n_rope, n_q):
    return [(0, b, b < n_rope, Q_SCALE if b < n_q else 1.0) for b in range(n_total)]


def _mixer_diff(x, g, w_in, lam_q1, lam_k1, lam_q2, lam_k2, subln_g, cos, sin, batch, seq, lambda_init):
    h = DIFF_HEADS
    blocks = _plain_blocks(3 * h, 2 * h, h)
    (qkv,) = _proj(x, g, w_in.astype(BF16), cos, sin, blocks, [(3 * h, BF16)], "diff_in_proj")
    pad = lambda v: jnp.pad(v.astype(F32), (0, LANES - HEAD_DIM))
    lam_rows = jnp.stack([pad(lam_q1), pad(lam_k1), pad(lam_q2), pad(lam_k2)]
                         + [jnp.zeros((LANES,), F32)] * 4)
    return _diff_attention(qkv, lam_rows, subln_g, batch, seq, lambda_init)


def _mixer_moba(x, g, w_in, cos, sin, batch, seq):
    hp = MOBA_HEADS // 2
    blocks = _plain_blocks(3 * hp, 2 * hp, hp)
    (qkv,) = _proj(x, g, w_in.astype(BF16), cos, sin, blocks, [(3 * hp, BF16)], "moba_in_proj")
    return _moba_attention(qkv, batch, seq)


def _mixer_dil(x, g, w_in, cos, sin, batch, seq):
    hp = DIL_HEADS // 2
    blocks = []
    for gi in range(len(DIL_PATTERNS)):
        for which in range(3):
            for b in range(hp):
                blocks.append((0, len(blocks), which < 2, Q_SCALE if which == 0 else 1.0))
    (qkv,) = _proj(x, g, w_in.astype(BF16), cos, sin, blocks, [(len(blocks), BF16)], "dil_in_proj")
    return _dil_attention(qkv, batch, seq)


def _mixer_nsa(x, g, w_in, ck_pos, ck_w1, ck_w2, cv_pos, cv_w1, cv_w2, cos, sin, batch, seq):
    hq, gr, dk = NSA_HEADS, NSA_GROUPS, HEAD_DIM
    kvw = gr * dk
    o_q = 0
    o_kc, o_vc, o_ks, o_vs, o_kw, o_vw, o_g = (hq * dk + n * kvw for n in range(7))
    n_gate = 3 * hq

    def dup(off):
        cols = w_in[:, off:off + kvw].reshape(D_MODEL, gr, dk)
        return jnp.concatenate([cols, cols], axis=2).reshape(D_MODEL, gr * LANES)

    per_group = n_gate // gr
    gate_cols = jnp.pad(w_in[:, o_g:o_g + n_gate].reshape(D_MODEL, gr, per_group),
                        ((0, 0), (0, 0), (0, LANES - per_group))).reshape(D_MODEL, gr * LANES)
    w = jnp.concatenate([w_in[:, o_q:o_q + hq * dk], dup(o_ks), dup(o_kw), w_in[:, o_kc:o_kc + kvw],
                         dup(o_vs), dup(o_vw), w_in[:, o_vc:o_vc + kvw], gate_cols], axis=1).astype(BF16)
    nqb = hq * dk // LANES
    blocks = [(0, b, True, Q_SCALE) for b in range(nqb)]
    blocks += [(1, b, True, 1.0) for b in range(gr)]
    blocks += [(2, b, True, 1.0) for b in range(gr)]
    blocks += [(3, b, True, 1.0) for b in range(kvw // LANES)]
    blocks += [(4, b, False, 1.0) for b in range(gr)]
    blocks += [(5, b, False, 1.0) for b in range(gr)]
    blocks += [(6, b, False, 1.0) for b in range(kvw // LANES)]
    blocks += [(7, b, False, 1.0) for b in range(gr)]
    outs = [(nqb, BF16), (gr, BF16), (gr, BF16), (kvw // LANES, BF16), (gr, BF16), (gr, BF16),
            (kvw // LANES, BF16), (gr, F32)]
    q, ks, kw, kc, vs, vw, vc, gates = _proj(x, g, w, cos, sin, blocks, outs, "nsa_in_proj")
    kcmp, vcmp = _nsa_compress(kc, vc, ck_pos, ck_w1, ck_w2, cv_pos, cv_w1, cv_w2, batch, seq)
    return _nsa_attention(q, ks, vs, kw, vw, kcmp, vcmp, gates, batch, seq)


def kernel(x, p, positions, norm_gains, ple_norm_g, ple_gate_w, ple_proj_w, ffn_w_up, ffn_conv_w, ffn_conv_b, ffn_w_down, diff_w_in, diff_w_out, diff_lam_q1, diff_lam_k1, diff_lam_q2, diff_lam_k2, diff_subln_g, nsa_w_in, nsa_w_out, nsa_ck_pos, nsa_ck_w1, nsa_ck_w2, nsa_cv_pos, nsa_cv_w1, nsa_cv_w2, moba_w_in, moba_w_out, dil_w_in, dil_w_out):
    batch, seq, d = x.shape
    t = batch * seq
    cos, sin = _rope_tables(positions)
    xt = x.reshape(t, d)
    (ffn_w_up, ffn_w_down, ple_gate_w, ple_proj_w, diff_w_in, diff_w_out, nsa_w_in, nsa_w_out, moba_w_in,
     moba_w_out, dil_w_in, dil_w_out) = (w.astype(BF16) for w in (
         ffn_w_up, ffn_w_down, ple_gate_w, ple_proj_w, diff_w_in, diff_w_out, nsa_w_in, nsa_w_out, moba_w_in,
         moba_w_out, dil_w_in, dil_w_out))
    for i in range(DEPTH):
        kind, j = i % N_MIXERS, i // N_MIXERS
        g0 = norm_gains[i, 0]
        if kind == 0:
            lambda_init = 0.8 - 0.6 * math.exp(-0.3 * i)
            o = _mixer_diff(xt, g0, diff_w_in[j], diff_lam_q1[j], diff_lam_k1[j], diff_lam_q2[j],
                            diff_lam_k2[j], diff_subln_g[j], cos, sin, batch, seq, lambda_init)
            w_out = diff_w_out[j]
        elif kind == 1:
            o = _mixer_nsa(xt, g0, nsa_w_in[j], nsa_ck_pos[j], nsa_ck_w1[j], nsa_ck_w2[j],
                           nsa_cv_pos[j], nsa_cv_w1[j], nsa_cv_w2[j], cos, sin, batch, seq)
            w_out = nsa_w_out[j]
        elif kind == 2:
            o = _mixer_moba(xt, g0, moba_w_in[j], cos, sin, batch, seq)
            w_out = moba_w_out[j]
        else:
            o = _mixer_dil(xt, g0, dil_w_in[j], cos, sin, batch, seq)
            w_out = dil_w_out[j]
        xt = _out_proj(o, w_out.astype(BF16), xt, norm_gains[i, 1], "out_proj_%d" % i)
        xt = _ffn(xt, norm_gains[i, 2], ffn_w_up[i].astype(BF16), ffn_conv_w[i], ffn_conv_b[i],
                  ffn_w_down[i].astype(BF16), norm_gains[i, 3], ple_norm_g[i],
                  ple_gate_w[i].astype(BF16), ple_proj_w[i].astype(BF16), p[i].reshape(t, PLE_DIM),
                  seq, "ffn_%d" % i)
    return xt.reshape(batch, seq, d)
```

```python
import functools
import math

import jax
import jax.numpy as jnp
from jax import lax
from jax.experimental import pallas as pl
from jax.experimental.pallas import tpu as pltpu

F32 = jnp.float32
BF16 = jnp.bfloat16

D_MODEL = 1024
DEPTH = 4
N_MIXERS = 4
HEAD_DIM = 64
ROPE_THETA = 10000.0
NORM_EPS = 1e-6
PLE_DIM = 256
NEG = -1e30

DIFF_HEADS = D_MODEL // (2 * HEAD_DIM)

NSA_HEADS = D_MODEL // HEAD_DIM
NSA_GROUPS = 4
NSA_REP = NSA_HEADS // NSA_GROUPS
NSA_CMP_LEN = 32
NSA_CMP_STRIDE = 16
NSA_CMP_HIDDEN = 128
NSA_SLC_BLOCK = 64
NSA_TOPN = 16
NSA_N_LOCAL = 2
NSA_WINDOW = 512
NSA_FORCE_BONUS = 1e4

MOBA_HEADS = D_MODEL // HEAD_DIM
MOBA_BLOCK = 256
MOBA_TOPK = 3

DIL_PATTERNS = ((128, 1), (512, 4), (2048, 16))
DIL_HEADS = 8

D_FF = 2816
CONV_WIDTH = 3

LANES = 128
MXU_COLS = 256
TQ = 256
LOG2E = 1.4426950408889634
Q_SCALE = (HEAD_DIM ** -0.5) * LOG2E
VMEM_LIMIT = 56 * 1024 * 1024


def _cparams(sem):
    return pltpu.CompilerParams(dimension_semantics=sem, vmem_limit_bytes=VMEM_LIMIT)


def _rms(x, g):
    return x * lax.rsqrt(jnp.mean(x * x, axis=-1, keepdims=True) + NORM_EPS) * g


def _dot(a, b, precision=None):
    return jnp.dot(a, b, preferred_element_type=F32, precision=precision)


def _dot_nt(a, b, precision=None):
    return lax.dot_general(a, b, (((1,), (1,)), ((), ())), preferred_element_type=F32,
                           precision=precision)


def _lane_iota(shape):
    return lax.broadcasted_iota(jnp.int32, shape, 1)


def _row_iota(shape):
    return lax.broadcasted_iota(jnp.int32, shape, 0)


def _rope_kernel(pos_ref, inv_ref, cos_ref, sin_ref):
    ang = pos_ref[...].astype(F32) * inv_ref[...]
    lane = _lane_iota(ang.shape)
    cos_ref[...] = jnp.cos(ang)
    sin_ref[...] = jnp.where((lane & 63) < 32, -jnp.sin(ang), jnp.sin(ang))


def _rope_tables(positions):
    t = positions.size
    inv = ROPE_THETA ** (-jnp.arange(0, HEAD_DIM, 2, dtype=F32) / HEAD_DIM)
    inv = jnp.tile(inv, 4)[None, :]
    tm = 2048
    return pl.pallas_call(
        _rope_kernel,
        grid=(t // tm,),
        in_specs=[pl.BlockSpec((tm, 1), lambda i: (i, 0)),
                  pl.BlockSpec((1, LANES), lambda i: (0, 0))],
        out_specs=[pl.BlockSpec((tm, LANES), lambda i: (i, 0))] * 2,
        out_shape=[jax.ShapeDtypeStruct((t, LANES), F32)] * 2,
        compiler_params=_cparams(("parallel",)),
        name="rope_tables",
    )(positions.reshape(t, 1), inv)


def _proj_kernel(x_ref, g_ref, w_ref, cos_ref, sin_ref, *o_refs, blocks):
    hn = _rms(x_ref[...], g_ref[...]).astype(BF16)
    cos = cos_ref[...]
    sin = sin_ref[...]
    first = (_lane_iota(cos.shape) & 63) < 32
    n_chunks = len(blocks) // 2
    for c in range(n_chunks):
        acc = _dot(hn, w_ref[:, c * MXU_COLS:(c + 1) * MXU_COLS])
        for s in range(2):
            out_idx, out_blk, rope, scale = blocks[2 * c + s]
            if out_idx is None:
                continue
            a = acc[:, s * LANES:(s + 1) * LANES]
            if rope:
                rot = jnp.where(first, pltpu.roll(a, 96, 1), pltpu.roll(a, 32, 1))
                a = a * cos + rot * sin
            if scale != 1.0:
                a = a * scale
            o_ref = o_refs[out_idx]
            o_ref[:, out_blk * LANES:(out_blk + 1) * LANES] = a.astype(o_ref.dtype)


def _proj(x, g, w, cos, sin, blocks, outs, name):
    t = x.shape[0]
    n = w.shape[1]
    assert n == len(blocks) * LANES and len(blocks) % 2 == 0
    tm = 512
    return pl.pallas_call(
        functools.partial(_proj_kernel, blocks=tuple(blocks)),
        grid=(t // tm,),
        in_specs=[pl.BlockSpec((tm, D_MODEL), lambda i: (i, 0)),
                  pl.BlockSpec((1, D_MODEL), lambda i: (0, 0)),
                  pl.BlockSpec((D_MODEL, n), lambda i: (0, 0)),
                  pl.BlockSpec((tm, LANES), lambda i: (i, 0)),
                  pl.BlockSpec((tm, LANES), lambda i: (i, 0))],
        out_specs=[pl.BlockSpec((tm, nb * LANES), lambda i: (i, 0)) for nb, _ in outs],
        out_shape=[jax.ShapeDtypeStruct((t, nb * LANES), dt) for nb, dt in outs],
        compiler_params=_cparams(("parallel",)),
        name=name,
    )(x, g.reshape(1, D_MODEL), w, cos, sin)


def _out_kernel(o_ref, w_ref, x_ref, g_ref, y_ref):
    y = _dot(o_ref[...], w_ref[...])
    y_ref[...] = x_ref[...] + _rms(y, g_ref[...])


def _out_proj(o, w, x, g, name):
    t, k = o.shape
    tm = 1024
    return pl.pallas_call(
        _out_kernel,
        grid=(t // tm,),
        in_specs=[pl.BlockSpec((tm, k), lambda i: (i, 0)),
                  pl.BlockSpec((k, D_MODEL), lambda i: (0, 0)),
                  pl.BlockSpec((tm, D_MODEL), lambda i: (i, 0)),
                  pl.BlockSpec((1, D_MODEL), lambda i: (0, 0))],
        out_specs=pl.BlockSpec((tm, D_MODEL), lambda i: (i, 0)),
        out_shape=jax.ShapeDtypeStruct((t, D_MODEL), F32),
        compiler_params=_cparams(("parallel",)),
        name=name,
    )(o, w, x, g.reshape(1, D_MODEL))


FF_CHUNK = 256
FF_TM = 512
FF_HALO = 16


def _gelu_tanh(x):
    k = math.sqrt(2.0 / math.pi)
    h = 0.5 * x
    return h + h * jnp.tanh(x * (k + (k * 0.044715) * (x * x)))


def _ffn_kernel(x_ref, xh_ref, g2_ref, wu_ref, cw_ref, cb_ref, wd_ref, g3_ref, pg_ref, pgw_ref, ppw_ref,
                p_ref, y_ref, hn_s, *, seq):
    i = pl.program_id(0)
    tm = x_ref.shape[0]
    n_c = D_FF // FF_CHUNK
    x = x_ref[...]
    hn_s[FF_HALO:, :] = _rms(x, g2_ref[...]).astype(BF16)
    halo = _rms(xh_ref[...], g2_ref[...])
    seq_start = (i * tm) % seq == 0
    hn_s[:FF_HALO, :] = jnp.where(seq_start, 0.0, halo).astype(BF16)

    def conv(u, col):
        cols = slice(col, col + FF_CHUNK)
        out = cb_ref[:, cols] + pltpu.roll(u, 2, 0)[FF_HALO:] * cw_ref[0:1, cols]
        out = out + pltpu.roll(u, 1, 0)[FF_HALO:] * cw_ref[1:2, cols]
        return out + u[FF_HALO:] * cw_ref[2:3, cols]

    def up(c, r):
        hn = hn_s[r * FF_ROWS:(r + 1) * FF_ROWS + FF_HALO, :]
        gcol, vcol = c * FF_CHUNK, D_FF + c * FF_CHUNK
        return (_dot(hn, wu_ref[:, gcol:gcol + FF_CHUNK]), _dot(hn, wu_ref[:, vcol:vcol + FF_CHUNK]))

    def down(c, act, acc):
        d = _dot(act, wd_ref[c * FF_CHUNK:(c + 1) * FF_CHUNK, :])
        return d if acc is None else acc + d

    units = [(c, r) for c in range(n_c) for r in range(tm // FF_ROWS)]
    accs = [None] * (tm // FF_ROWS)
    ups = {k: up(*units[k]) for k in range(FF_UP_LEAD)}
    pending = None
    for k, (c, r) in enumerate(units):
        ug, uv = ups.pop(k)
        if k + FF_UP_LEAD < len(units):
            ups[k + FF_UP_LEAD] = up(*units[k + FF_UP_LEAD])
        act = (_gelu_tanh(conv(ug, c * FF_CHUNK)) * conv(uv, D_FF + c * FF_CHUNK)).astype(BF16)
        if pending is not None:
            accs[pending[1]] = down(pending[0], pending[2], accs[pending[1]])
        pending = (c, r, act)
    accs[pending[1]] = down(pending[0], pending[2], accs[pending[1]])

    x2 = x + _rms(jnp.concatenate(accs, axis=0), g3_ref[...])
    gate = jax.nn.sigmoid(_dot(_rms(x2, pg_ref[...]).astype(BF16), pgw_ref[...]))
    y_ref[...] = x2 + gate * _dot(p_ref[...].astype(BF16), ppw_ref[...])


FF_ROWS = 256
FF_UP_LEAD = 2


def _ffn(x, g2, w_up, conv_w, conv_b, w_down, g3, ple_g, ple_gate_w, ple_proj_w, p, seq, name):
    t = x.shape[0]
    tm = FF_TM
    assert seq % tm == 0 and tm % FF_ROWS == 0
    hb = tm // FF_HALO
    row = lambda v: v.reshape(1, -1)
    const = lambda i: (0, 0)
    resident = lambda shape: pl.BlockSpec(shape, const, pipeline_mode=pl.Buffered(1))
    return pl.pallas_call(
        functools.partial(_ffn_kernel, seq=seq),
        grid=(t // tm,),
        in_specs=[pl.BlockSpec((tm, D_MODEL), lambda i: (i, 0)),
                  pl.BlockSpec((FF_HALO, D_MODEL), lambda i: (jnp.maximum(i * hb - 1, 0), 0)),
                  pl.BlockSpec((1, D_MODEL), const),
                  resident((D_MODEL, 2 * D_FF)),
                  pl.BlockSpec((CONV_WIDTH, 2 * D_FF), const),
                  pl.BlockSpec((1, 2 * D_FF), const),
                  resident((D_FF, D_MODEL)),
                  pl.BlockSpec((1, D_MODEL), const),
                  pl.BlockSpec((1, D_MODEL), const),
                  resident((D_MODEL, D_MODEL)),
                  resident((PLE_DIM, D_MODEL)),
                  pl.BlockSpec((tm, PLE_DIM), lambda i: (i, 0))],
        out_specs=pl.BlockSpec((tm, D_MODEL), lambda i: (i, 0)),
        out_shape=jax.ShapeDtypeStruct((t, D_MODEL), F32),
        scratch_shapes=[pltpu.VMEM((tm + FF_HALO, D_MODEL), BF16)],
        compiler_params=_cparams(("parallel",)),
        name=name,
    )(x, x, row(g2), w_up, conv_w, row(conv_b), w_down, row(g3), row(ple_g), ple_gate_w, ple_proj_w, p)


def _lo_half(shape):
    return _lane_iota(shape) < HEAD_DIM


def _split_heads(q):
    lo = _lo_half(q.shape)
    qf = q.astype(F32)
    return jnp.where(lo, qf, 0.0).astype(q.dtype), jnp.where(lo, 0.0, qf).astype(q.dtype)


def _tile(ref, j):
    return ref[j * TQ:(j + 1) * TQ, :]


def _scores(q_op, k_ref, first, masks, step=1):
    s_list = []
    for t, mask in enumerate(masks):
        s = _dot_nt(_tile(k_ref, first + t * step), q_op)
        s_list.append(s if mask is None else jnp.where(mask, s, NEG))
    return s_list


def _attend(s_list, values_t):
    mt = None
    for s in s_list:
        mt = s if mt is None else jnp.maximum(mt, s)
    m = jnp.max(mt, axis=0, keepdims=True)
    acc = None
    for s, v_t in zip(s_list, values_t):
        d = _dot(v_t, jnp.exp2(s - m).astype(BF16))
        acc = d if acc is None else acc + d
    return acc


def _value_t(v_tile, keep=None):
    v_t = v_tile.astype(F32).T
    return (v_t if keep is None else jnp.where(keep, v_t, 1.0)).astype(BF16)


def _store_pair_values(v_ref, vt_s, j, *idx):
    lo_rows = _row_iota((LANES, TQ)) < HEAD_DIM
    v_t = _tile(v_ref, j).astype(F32).T
    cols = slice(j * TQ, (j + 1) * TQ)
    vt_s[idx + (0, slice(None), cols)] = jnp.where(lo_rows, v_t, 1.0).astype(BF16)
    vt_s[idx + (1, slice(None), cols)] = jnp.where(lo_rows, 1.0, v_t).astype(BF16)


def _pair_output(acc0, acc1):
    out0 = acc0[:HEAD_DIM] / acc0[HEAD_DIM:HEAD_DIM + 1]
    out1 = acc1[HEAD_DIM:] / acc1[0:1]
    return jnp.concatenate([out0, out1], axis=0).T


ONES_ROWS = 16


def _tile_t(vt_s, j, *idx):
    return vt_s[idx + (slice(None), slice(j * TQ, (j + 1) * TQ))]


def _run_pipelined(stages, lead=None):
    n = len(stages)
    if lead is None:
        for prep_fn, _, _ in stages:
            if prep_fn is not None:
                prep_fn()
        lead = 0
    states = {}
    for step in range(n + lead + 1):
        if lead and step < n and stages[step][0] is not None:
            stages[step][0]()
        if lead <= step < n + lead:
            states[step - lead] = stages[step - lead][1]()
        if step >= lead + 1:
            stages[step - lead - 1][2](states.pop(step - lead - 1))


PREP_LEAD = 5


def _causal_masks():
    diff = _lane_iota((TQ, TQ)) - _row_iota((TQ, TQ))
    return diff, diff >= 0


def _rank_rows(score, n_valid):
    ridx = _row_iota(score.shape)
    cnt = jnp.zeros(score.shape, F32)
    for j in range(n_valid):
        other = score[j:j + 1, :]
        cnt = cnt + jnp.where(other > score, 1.0,
                              jnp.where(other == score, jnp.where(ridx > j, 1.0, 0.0), 0.0))
    return cnt


def _bias_lanes(bias_t, first_lane):
    r, tq = bias_t.shape
    parts = []
    if first_lane:
        parts.append(jnp.zeros((first_lane, tq), F32))
    parts.append(bias_t)
    if LANES - first_lane - r:
        parts.append(jnp.zeros((LANES - first_lane - r, tq), F32))
    return jnp.concatenate(parts, axis=0).T


def _diff_kernel(q_ref, k_ref, v_ref, lam_ref, sg_ref, o_ref, vt_s, *, lambda_init):
    nq = q_ref.shape[0] // TQ
    _, causal = _causal_masks()
    lamv = lam_ref[...]
    lam = (jnp.exp(jnp.sum(lamv[0:1] * lamv[1:2], axis=1, keepdims=True))
           - jnp.exp(jnp.sum(lamv[2:3] * lamv[3:4], axis=1, keepdims=True)) + lambda_init)
    sg = sg_ref[...]

    def prep(i):
        vt_s[:LANES, i * TQ:(i + 1) * TQ] = _value_t(_tile(v_ref, i))
        vt_s[LANES:, i * TQ:(i + 1) * TQ] = jnp.ones((ONES_ROWS, TQ), BF16)

    def score(i):
        q0, q1 = _split_heads(_tile(q_ref, i))
        masks = [None] * i + [causal]
        return _scores(q0, k_ref, 0, masks), _scores(q1, k_ref, 0, masks)

    def finish(i, state):
        v_tiles = [_tile_t(vt_s, j) for j in range(i + 1)]
        a0 = _attend(state[0], v_tiles)
        a1 = _attend(state[1], v_tiles)
        o_t = a0[:LANES] / a0[LANES:LANES + 1] - lam * (a1[:LANES] / a1[LANES:LANES + 1])
        o_ref[i * TQ:(i + 1) * TQ, :] = (_rms(o_t.T, sg) * (1.0 - lambda_init)).astype(o_ref.dtype)

    _run_pipelined([(functools.partial(prep, i), functools.partial(score, i), functools.partial(finish, i))
                    for i in range(nq)], PREP_LEAD)


def _diff_attention(qkv, lam_rows, subln_g, batch, seq, lambda_init):
    t = batch * seq
    h = DIFF_HEADS
    return pl.pallas_call(
        functools.partial(_diff_kernel, lambda_init=lambda_init),
        grid=(batch, h),
        in_specs=[pl.BlockSpec((seq, LANES), lambda b, hh: (b, hh)),
                  pl.BlockSpec((seq, LANES), lambda b, hh: (b, h + hh)),
                  pl.BlockSpec((seq, LANES), lambda b, hh: (b, 2 * h + hh)),
                  pl.BlockSpec((8, LANES), lambda b, hh: (0, 0)),
                  pl.BlockSpec((1, LANES), lambda b, hh: (0, 0))],
        out_specs=pl.BlockSpec((seq, LANES), lambda b, hh: (b, hh)),
        out_shape=jax.ShapeDtypeStruct((t, h * LANES), BF16),
        scratch_shapes=[pltpu.VMEM((LANES + ONES_ROWS, seq), BF16)],
        compiler_params=_cparams(("parallel", "parallel")),
        name="diff_attention",
    )(qkv, qkv, qkv, lam_rows, subln_g.reshape(1, LANES))


def _moba_kernel(q_ref, k_ref, v_ref, o_ref, ka_s, vt_s):
    seq = k_ref.shape[0]
    nb = seq // MOBA_BLOCK
    lane = _lane_iota((TQ, LANES))
    lo = lane < HEAD_DIM
    _, causal = _causal_masks()
    km = jnp.concatenate([jnp.mean(_tile(k_ref, j).astype(F32), axis=0, keepdims=True) for j in range(nb)],
                         axis=0)
    q_aug = {}
    outs = {}

    def prep(i, a):
        if a == 0:
            _store_pair_values(v_ref, vt_s, i)
            kf = _tile(k_ref, i).astype(F32)
            ka_s[0, i * TQ:(i + 1) * TQ, :] = jnp.where(lo, kf, jnp.where(lane == HEAD_DIM + i, 1.0, 0.0)
                                                        ).astype(BF16)
            ka_s[1, i * TQ:(i + 1) * TQ, :] = jnp.where(lo, jnp.where(lane == i, 1.0, 0.0), kf).astype(BF16)
        head = _split_heads(_tile(q_ref, i))[a]
        q_aug[i, a] = head
        if i > 0:
            gate_t = _dot_nt(km, head.astype(F32), precision=lax.Precision.HIGHEST)
            rank = _rank_rows(gate_t, i)
            drop = jnp.logical_and(_row_iota(gate_t.shape) < i, rank >= MOBA_TOPK)
            bias = _bias_lanes(jnp.where(drop, NEG, 0.0), HEAD_DIM * (1 - a))
            q_aug[i, a] = (head.astype(F32) + bias).astype(BF16)

    def score(i, a):
        return _scores(q_aug[i, a], ka_s.at[a], 0, [None] * i + [causal])

    def finish(i, a, s_list):
        outs[i, a] = _attend(s_list, [_tile_t(vt_s, j, a) for j in range(i + 1)])
        if a == 1:
            o_ref[i * TQ:(i + 1) * TQ, :] = _pair_output(outs[i, 0], outs[i, 1]).astype(o_ref.dtype)

    _run_pipelined([(functools.partial(prep, i, a), functools.partial(score, i, a),
                     functools.partial(finish, i, a)) for i in range(nb) for a in range(2)], PREP_LEAD)


def _moba_attention(qkv, batch, seq):
    assert MOBA_BLOCK == TQ and seq % MOBA_BLOCK == 0 and seq // MOBA_BLOCK <= HEAD_DIM
    t = batch * seq
    hp = MOBA_HEADS // 2
    return pl.pallas_call(
        _moba_kernel,
        grid=(batch, hp),
        in_specs=[pl.BlockSpec((seq, LANES), lambda b, hh: (b, hh)),
                  pl.BlockSpec((seq, LANES), lambda b, hh: (b, hp + hh)),
                  pl.BlockSpec((seq, LANES), lambda b, hh: (b, 2 * hp + hh))],
        out_specs=pl.BlockSpec((seq, LANES), lambda b, hh: (b, hh)),
        out_shape=jax.ShapeDtypeStruct((t, hp * LANES), BF16),
        scratch_shapes=[pltpu.VMEM((2, seq, LANES), BF16), pltpu.VMEM((2, LANES, seq), BF16)],
        compiler_params=_cparams(("parallel", "parallel")),
        name="moba_attention",
    )(qkv, qkv, qkv)


def _dil_kernel(q0_ref, q1_ref, q2_ref, k0_ref, k1_ref, k2_ref, v0_ref, v1_ref, v2_ref, o_ref, vt_s):
    seq = k0_ref.shape[0]
    nq = seq // TQ
    (w0, _), (w1, d1), (w2, d2) = DIL_PATTERNS
    assert w0 < TQ and w1 == 2 * TQ and w2 >= seq
    diff, causal = _causal_masks()
    on1 = (diff & (d1 - 1)) == 0
    on2 = (diff & (d2 - 1)) == 0
    m0_diag = jnp.logical_and(causal, diff <= w0)
    m0_prev = diff + TQ <= w0
    m1_diag = jnp.logical_and(causal, on1)
    m1_prev2 = jnp.logical_and(on1, diff <= 0)
    m2_diag = jnp.logical_and(causal, on2)
    k_refs = (k0_ref, k1_ref, k2_ref)
    q_refs = (q0_ref, q1_ref, q2_ref)
    outs = {}

    def runs(i):
        g0 = ([m0_prev] if i >= 1 else []) + [m0_diag]
        g1 = ([m1_prev2] if i >= 2 else []) + ([on1] if i >= 1 else []) + [m1_diag]
        g2 = [on2] * i + [m2_diag]
        return [(g, i + 1 - len(masks), masks) for g, masks in enumerate((g0, g1, g2))]

    def score(i, a):
        s_list = []
        for g, first, masks in runs(i):
            s_list += _scores(_split_heads(_tile(q_refs[g], i))[a], k_refs[g], first, masks)
        return s_list

    def finish(i, a, s_list):
        v_tiles = [_tile_t(vt_s, first + t, g, a) for g, first, masks in runs(i) for t in range(len(masks))]
        outs[i, a] = _attend(s_list, v_tiles)
        if a == 1:
            o_ref[i * TQ:(i + 1) * TQ, :] = _pair_output(outs[i, 0], outs[i, 1]).astype(o_ref.dtype)

    def prep(i, a):
        if a == 0:
            for g, v_ref in enumerate((v0_ref, v1_ref, v2_ref)):
                _store_pair_values(v_ref, vt_s, i, g)

    _run_pipelined([(functools.partial(prep, i, a), functools.partial(score, i, a),
                     functools.partial(finish, i, a)) for i in reversed(range(nq)) for a in range(2)])


def _dil_attention(qkv, batch, seq):
    t = batch * seq
    hp = DIL_HEADS // 2
    per_g = 3 * hp

    def spec(g, which):
        return pl.BlockSpec((seq, LANES), lambda b, hh: (b, g * per_g + which * hp + hh))

    return pl.pallas_call(
        _dil_kernel,
        grid=(batch, hp),
        in_specs=[spec(0, 0), spec(1, 0), spec(2, 0), spec(0, 1), spec(1, 1), spec(2, 1),
                  spec(0, 2), spec(1, 2), spec(2, 2)],
        out_specs=pl.BlockSpec((seq, LANES), lambda b, hh: (b, hh)),
        out_shape=jax.ShapeDtypeStruct((t, hp * LANES), BF16),
        scratch_shapes=[pltpu.VMEM((len(DIL_PATTERNS), 2, LANES, seq), BF16)],
        compiler_params=_cparams(("parallel", "parallel")),
        name="dilated_attention",
    )(*([qkv] * 9))


N_CMP_PAD = 128


def _cmp_kernel(zk_ref, zv_ref, w1k_ref, w1v_ref, pk_ref, pv_ref, w1fk_ref, w1fv_ref, w2k_ref, w2v_ref,
                ko_ref, vo_ref):
    def one(z_ref, w1_ref, pos_ref, w1f_ref, w2_ref, o_ref):
        z = z_ref[...]
        a = _dot(z, w1_ref[0])
        b = _dot(z, w1_ref[1])
        b_next = jnp.concatenate([b[1:], jnp.zeros_like(b[:1])], axis=0)
        const = _dot(pos_ref[...], w1f_ref[...])[0:1]
        hid = a + b_next + jnp.concatenate([const] * NSA_GROUPS, axis=1)
        o_ref[...] = _dot(jax.nn.gelu(hid, approximate=True).astype(BF16), w2_ref[...]).astype(o_ref.dtype)

    one(zk_ref, w1k_ref, pk_ref, w1fk_ref, w2k_ref, ko_ref)
    one(zv_ref, w1v_ref, pv_ref, w1fv_ref, w2v_ref, vo_ref)


def _nsa_compress(kc, vc, ck_pos, ck_w1, ck_w2, cv_pos, cv_w1, cv_w2, batch, seq):
    g, dk, f = NSA_GROUPS, HEAD_DIM, NSA_CMP_HIDDEN
    half = NSA_CMP_LEN // 2
    assert NSA_CMP_STRIDE == half and seq // half == N_CMP_PAD
    zk = kc.reshape(batch * N_CMP_PAD, half * g * dk)
    zv = vc.reshape(batch * N_CMP_PAD, half * g * dk)

    def expand_w1(w1):
        w = w1.reshape(2, half, dk, f)
        eye = jnp.eye(g, dtype=w1.dtype)
        big = jnp.einsum('acdf,gh->acgdhf', w, eye)
        return big.reshape(2, half * g * dk, g * f).astype(BF16)

    def expand_w2(w2):
        dup = jnp.concatenate([w2, w2], axis=1)
        eye = jnp.eye(g, dtype=w2.dtype)
        return jnp.einsum('fd,gh->gfhd', dup, eye).reshape(g * f, g * LANES).astype(BF16)

    def pos_rows(pos):
        flat = pos.reshape(1, NSA_CMP_LEN * dk)
        return jnp.concatenate([flat, jnp.zeros((7, NSA_CMP_LEN * dk), pos.dtype)], axis=0).astype(BF16)

    kd = half * g * dk
    const2 = lambda b: (0, 0)
    const3 = lambda b: (0, 0, 0)
    return pl.pallas_call(
        _cmp_kernel,
        grid=(batch,),
        in_specs=[pl.BlockSpec((N_CMP_PAD, kd), lambda b: (b, 0)),
                  pl.BlockSpec((N_CMP_PAD, kd), lambda b: (b, 0)),
                  pl.BlockSpec((2, kd, g * f), const3),
                  pl.BlockSpec((2, kd, g * f), const3),
                  pl.BlockSpec((8, NSA_CMP_LEN * dk), const2),
                  pl.BlockSpec((8, NSA_CMP_LEN * dk), const2),
                  pl.BlockSpec((NSA_CMP_LEN * dk, f), const2),
                  pl.BlockSpec((NSA_CMP_LEN * dk, f), const2),
                  pl.BlockSpec((g * f, g * LANES), const2),
                  pl.BlockSpec((g * f, g * LANES), const2)],
        out_specs=[pl.BlockSpec((N_CMP_PAD, g * LANES), lambda b: (b, 0))] * 2,
        out_shape=[jax.ShapeDtypeStruct((batch * N_CMP_PAD, g * LANES), BF16)] * 2,
        compiler_params=_cparams(("parallel",)),
        name="nsa_compress",
    )(zk, zv, expand_w1(ck_w1), expand_w1(cv_w1), pos_rows(ck_pos), pos_rows(cv_pos),
      ck_w1.reshape(NSA_CMP_LEN * dk, f).astype(BF16), cv_w1.reshape(NSA_CMP_LEN * dk, f).astype(BF16),
      expand_w2(ck_w2), expand_w2(cv_w2))


SLC_SHIFT = NSA_SLC_BLOCK.bit_length() - 1


def _nsa_heads(q_pair):
    lo = _lo_half(q_pair.shape)
    qf = q_pair.astype(F32)
    return jnp.where(lo, qf, 0.0), jnp.where(lo, pltpu.roll(qf, HEAD_DIM, 1), 0.0)


def _nsa_select_kernel(q_ref, kc_ref, vc_ref, gate_ref, ov_ref, oc_ref, bias_ref):
    seq = q_ref.shape[0]
    nq = seq // TQ
    n_slc = seq // NSA_SLC_BLOCK
    assert 1 << SLC_SHIFT == NSA_SLC_BLOCK and HEAD_DIM + n_slc <= LANES
    n_idx = _row_iota((N_CMP_PAD, TQ))
    t_idx = _lane_iota((N_CMP_PAD, TQ))
    kc = kc_ref[...]
    vc_t = _value_t(vc_ref[...])
    jb = _row_iota((n_slc, TQ))
    cmask, scores = [], []
    for i in range(nq):
        cmask.append(jnp.logical_and(n_idx * NSA_CMP_STRIDE + (NSA_CMP_LEN - 1) <= i * TQ + t_idx,
                                     n_idx < N_CMP_PAD - 1))
        heads_bf = []
        for pr in range(NSA_REP // 2):
            heads_bf.extend(h.astype(BF16) for h in
                            _nsa_heads(q_ref[i * TQ:(i + 1) * TQ, pr * LANES:(pr + 1) * LANES]))
        scores.append([jnp.where(cmask[i], _dot_nt(kc, heads_bf[r]), NEG) for r in range(NSA_REP)])
    p_sums, o_cmps = [], []
    for i in range(nq):
        p_sum = jnp.zeros((N_CMP_PAD, TQ), F32)
        o_cmp = []
        for s in scores[i]:
            e = jnp.where(cmask[i], jnp.exp2(s - jnp.max(s, axis=0, keepdims=True)), 0.0)
            p = e / jnp.maximum(jnp.sum(e, axis=0, keepdims=True), 1e-30)
            p_sum = p_sum + p
            o_cmp.append(_dot(vc_t, p.astype(BF16))[:HEAD_DIM])
        p_sums.append(p_sum)
        o_cmps.append(o_cmp)
    for i in range(nq):
        o_cmp = o_cmps[i]
        imp_t = _dot(ov_ref[...], p_sums[i], precision=lax.Precision.HIGHEST)
        back = ((i * TQ + _lane_iota((n_slc, TQ))) >> SLC_SHIFT) - jb
        ok = back >= 0
        forced = jnp.logical_or(jb == 0, jnp.logical_and(ok, back < NSA_N_LOCAL))
        score = jnp.where(ok, imp_t + jnp.where(forced, NSA_FORCE_BONUS, 0.0), NEG)
        rank = _rank_rows(score, n_slc)
        keep = jnp.logical_and(ok, rank < min(NSA_TOPN, n_slc))
        bias_ref[i * TQ:(i + 1) * TQ, :] = _bias_lanes(jnp.where(keep, 0.0, NEG), HEAD_DIM).astype(bias_ref.dtype)

        gates_t = jax.nn.sigmoid(gate_ref[i * TQ:(i + 1) * TQ, :]).T
        for pr in range(NSA_REP // 2):
            pair = jnp.concatenate([o_cmp[r] * gates_t[3 * r:3 * r + 1] for r in (2 * pr, 2 * pr + 1)], axis=0)
            oc_ref[i * TQ:(i + 1) * TQ, pr * LANES:(pr + 1) * LANES] = pair.T.astype(oc_ref.dtype)


def _nsa_attn_kernel(q_ref, ks_ref, vs_ref, kw_ref, vw_ref, bias_ref, gate_ref, oc_ref, o_ref, ksa_s, vt_s):
    seq = ks_ref.shape[0]
    nq = seq // TQ
    lo_rows = _row_iota((LANES, TQ)) < HEAD_DIM
    per_tile = TQ // NSA_SLC_BLOCK
    assert NSA_WINDOW == 2 * TQ
    pr = pl.program_id(2)
    lane = _lane_iota((TQ, LANES))
    row = _row_iota((TQ, LANES))
    lo = lane < HEAD_DIM
    diff, causal = _causal_masks()
    outs = {}

    def prep(i, h):
        if h == 0:
            for b, v_ref in enumerate((vs_ref, vw_ref)):
                vt_s[b, :, i * TQ:(i + 1) * TQ] = _value_t(_tile(v_ref, i), lo_rows)
            ind = jnp.where(lane - HEAD_DIM == i * per_tile + (row >> SLC_SHIFT), 1.0, 0.0)
            ksa_s[i * TQ:(i + 1) * TQ, :] = jnp.where(lo, _tile(ks_ref, i).astype(F32), ind).astype(BF16)

    def win_masks(i):
        return [causal] + ([None] if i >= 1 else []) + ([diff < 0] if i >= 2 else [])

    def score(i, h):
        head = _nsa_heads(_tile(q_ref, i))[h]
        q_aug = (head + _tile(bias_ref, i).astype(F32)).astype(BF16)
        return (_scores(q_aug, ksa_s, 0, [None] * i + [causal]),
                _scores(head.astype(BF16), kw_ref, i, win_masks(i), step=-1))

    def finish(i, h, state):
        n_win = len(win_masks(i))
        a_slc = _attend(state[0], [_tile_t(vt_s, j, 0) for j in range(i + 1)])
        a_win = _attend(state[1], [_tile_t(vt_s, i - t, 1) for t in range(n_win)])
        gates_t = jax.nn.sigmoid(_tile(gate_ref, i)).T
        c0 = 3 * (2 * pr + h)

        def gate(c):
            return jnp.sum(jnp.where(_row_iota(gates_t.shape) == c, gates_t, 0.0), axis=0, keepdims=True)

        outs[i, h] = (a_slc[:HEAD_DIM] * (gate(c0 + 1) / a_slc[HEAD_DIM:HEAD_DIM + 1])
                      + a_win[:HEAD_DIM] * (gate(c0 + 2) / a_win[HEAD_DIM:HEAD_DIM + 1]))
        if h == 1:
            pair = jnp.concatenate([outs[i, 0], outs[i, 1]], axis=0).T
            o_ref[i * TQ:(i + 1) * TQ, :] = (_tile(oc_ref, i).astype(F32) + pair).astype(o_ref.dtype)

    _run_pipelined([(functools.partial(prep, i, h), functools.partial(score, i, h),
                     functools.partial(finish, i, h)) for i in range(nq) for h in range(2)], PREP_LEAD)


def _nsa_overlap_t(seq):
    n_slc = seq // NSA_SLC_BLOCK
    n = jnp.arange(N_CMP_PAD)
    first = n * NSA_CMP_STRIDE
    last = first + NSA_CMP_LEN - 1
    s_start = jnp.arange(n_slc) * NSA_SLC_BLOCK
    ov = (first[None, :] <= (s_start + NSA_SLC_BLOCK - 1)[:, None]) & (last[None, :] >= s_start[:, None])
    ov = ov & (n[None, :] < (seq - NSA_CMP_LEN) // NSA_CMP_STRIDE + 1)
    return ov.astype(F32)


def _nsa_attention(q, ks, vs, kw, vw, kcmp, vcmp, gates, batch, seq):
    t = batch * seq
    g = NSA_GROUPS
    qb = NSA_REP // 2
    n_slc = seq // NSA_SLC_BLOCK
    grp = lambda: pl.BlockSpec((seq, LANES), lambda b, gg: (b, gg))
    cm = lambda: pl.BlockSpec((N_CMP_PAD, LANES), lambda b, gg: (b, gg))
    o_cmp, bias = pl.pallas_call(
        _nsa_select_kernel,
        grid=(batch, g),
        in_specs=[pl.BlockSpec((seq, qb * LANES), lambda b, gg: (b, gg)), cm(), cm(), grp(),
                  pl.BlockSpec((n_slc, N_CMP_PAD), lambda b, gg: (0, 0))],
        out_specs=[pl.BlockSpec((seq, qb * LANES), lambda b, gg: (b, gg)), grp()],
        out_shape=[jax.ShapeDtypeStruct((t, NSA_HEADS * HEAD_DIM), BF16),
                   jax.ShapeDtypeStruct((t, g * LANES), BF16)],
        compiler_params=_cparams(("parallel", "parallel")),
        name="nsa_select",
    )(q, kcmp, vcmp, gates, _nsa_overlap_t(seq))
    kv = lambda: pl.BlockSpec((seq, LANES), lambda b, gg, pr: (b, gg))
    pair = lambda: pl.BlockSpec((seq, LANES), lambda b, gg, pr: (b, gg * qb + pr))
    return pl.pallas_call(
        _nsa_attn_kernel,
        grid=(batch, g, qb),
        in_specs=[pair(), kv(), kv(), kv(), kv(), kv(), kv(), pair()],
        out_specs=pair(),
        out_shape=jax.ShapeDtypeStruct((t, NSA_HEADS * HEAD_DIM), BF16),
        scratch_shapes=[pltpu.VMEM((seq, LANES), BF16), pltpu.VMEM((2, LANES, seq), BF16)],
        compiler_params=_cparams(("parallel", "parallel", "arbitrary")),
        name="nsa_attention",
    )(q, ks, vs, kw, vw, bias, gates, o_cmp)


def _plain_blocks(n_total, n_rope, n_q):
    return [(0, b, b < n_rope, Q_SCALE if b < n_q else 1.0) for b in range(n_total)]


def _mixer_diff(x, g, w_in, lam_q1, lam_k1, lam_q2, lam_k2, subln_g, cos, sin, batch, seq, lambda_init):
    h = DIFF_HEADS
    blocks = _plain_blocks(3 * h, 2 * h, h)
    (qkv,) = _proj(x, g, w_in.astype(BF16), cos, sin, blocks, [(3 * h, BF16)], "diff_in_proj")
    pad = lambda v: jnp.pad(v.astype(F32), (0, LANES - HEAD_DIM))
    lam_rows = jnp.stack([pad(lam_q1), pad(lam_k1), pad(lam_q2), pad(lam_k2)]
                         + [jnp.zeros((LANES,), F32)] * 4)
    return _diff_attention(qkv, lam_rows, subln_g, batch, seq, lambda_init)


def _mixer_moba(x, g, w_in, cos, sin, batch, seq):
    hp = MOBA_HEADS // 2
    blocks = _plain_blocks(3 * hp, 2 * hp, hp)
    (qkv,) = _proj(x, g, w_in.astype(BF16), cos, sin, blocks, [(3 * hp, BF16)], "moba_in_proj")
    return _moba_attention(qkv, batch, seq)


def _mixer_dil(x, g, w_in, cos, sin, batch, seq):
    hp = DIL_HEADS // 2
    blocks = []
    for gi in range(len(DIL_PATTERNS)):
        for which in range(3):
            for b in range(hp):
                blocks.append((0, len(blocks), which < 2, Q_SCALE if which == 0 else 1.0))
    (qkv,) = _proj(x, g, w_in.astype(BF16), cos, sin, blocks, [(len(blocks), BF16)], "dil_in_proj")
    return _dil_attention(qkv, batch, seq)


def _mixer_nsa(x, g, w_in, ck_pos, ck_w1, ck_w2, cv_pos, cv_w1, cv_w2, cos, sin, batch, seq):
    hq, gr, dk = NSA_HEADS, NSA_GROUPS, HEAD_DIM
    kvw = gr * dk
    o_q = 0
    o_kc, o_vc, o_ks, o_vs, o_kw, o_vw, o_g = (hq * dk + n * kvw for n in range(7))
    n_gate = 3 * hq

    def dup(off):
        cols = w_in[:, off:off + kvw].reshape(D_MODEL, gr, dk)
        return jnp.concatenate([cols, cols], axis=2).reshape(D_MODEL, gr * LANES)

    per_group = n_gate // gr
    gate_cols = jnp.pad(w_in[:, o_g:o_g + n_gate].reshape(D_MODEL, gr, per_group),
                        ((0, 0), (0, 0), (0, LANES - per_group))).reshape(D_MODEL, gr * LANES)
    w = jnp.concatenate([w_in[:, o_q:o_q + hq * dk], dup(o_ks), dup(o_kw), w_in[:, o_kc:o_kc + kvw],
                         dup(o_vs), dup(o_vw), w_in[:, o_vc:o_vc + kvw], gate_cols], axis=1).astype(BF16)
    nqb = hq * dk // LANES
    blocks = [(0, b, True, Q_SCALE) for b in range(nqb)]
    blocks += [(1, b, True, 1.0) for b in range(gr)]
    blocks += [(2, b, True, 1.0) for b in range(gr)]
    blocks += [(3, b, True, 1.0) for b in range(kvw // LANES)]
    blocks += [(4, b, False, 1.0) for b in range(gr)]
    blocks += [(5, b, False, 1.0) for b in range(gr)]
    blocks += [(6, b, False, 1.0) for b in range(kvw // LANES)]
    blocks += [(7, b, False, 1.0) for b in range(gr)]
    outs = [(nqb, BF16), (gr, BF16), (gr, BF16), (kvw // LANES, BF16), (gr, BF16), (gr, BF16),
            (kvw // LANES, BF16), (gr, F32)]
    q, ks, kw, kc, vs, vw, vc, gates = _proj(x, g, w, cos, sin, blocks, outs, "nsa_in_proj")
    kcmp, vcmp = _nsa_compress(kc, vc, ck_pos, ck_w1, ck_w2, cv_pos, cv_w1, cv_w2, batch, seq)
    return _nsa_attention(q, ks, vs, kw, vw, kcmp, vcmp, gates, batch, seq)


def kernel(x, p, positions, norm_gains, ple_norm_g, ple_gate_w, ple_proj_w, ffn_w_up, ffn_conv_w, ffn_conv_b, ffn_w_down, diff_w_in, diff_w_out, diff_lam_q1, diff_lam_k1, diff_lam_q2, diff_lam_k2, diff_subln_g, nsa_w_in, nsa_w_out, nsa_ck_pos, nsa_ck_w1, nsa_ck_w2, nsa_cv_pos, nsa_cv_w1, nsa_cv_w2, moba_w_in, moba_w_out, dil_w_in, dil_w_out):
    batch, seq, d = x.shape
    t = batch * seq
    cos, sin = _rope_tables(positions)
    xt = x.reshape(t, d)
    (ffn_w_up, ffn_w_down, ple_gate_w, ple_proj_w, diff_w_in, diff_w_out, nsa_w_in, nsa_w_out, moba_w_in,
     moba_w_out, dil_w_in, dil_w_out) = (w.astype(BF16) for w in (
         ffn_w_up, ffn_w_down, ple_gate_w, ple_proj_w, diff_w_in, diff_w_out, nsa_w_in, nsa_w_out, moba_w_in,
         moba_w_out, dil_w_in, dil_w_out))
    for i in range(DEPTH):
        kind, j = i % N_MIXERS, i // N_MIXERS
        g0 = norm_gains[i, 0]
        if kind == 0:
            lambda_init = 0.8 - 0.6 * math.exp(-0.3 * i)
            o = _mixer_diff(xt, g0, diff_w_in[j], diff_lam_q1[j], diff_lam_k1[j], diff_lam_q2[j],
                            diff_lam_k2[j], diff_subln_g[j], cos, sin, batch, seq, lambda_init)
            w_out = diff_w_out[j]
        elif kind == 1:
            o = _mixer_nsa(xt, g0, nsa_w_in[j], nsa_ck_pos[j], nsa_ck_w1[j], nsa_ck_w2[j],
                           nsa_cv_pos[j], nsa_cv_w1[j], nsa_cv_w2[j], cos, sin, batch, seq)
            w_out = nsa_w_out[j]
        elif kind == 2:
            o = _mixer_moba(xt, g0, moba_w_in[j], cos, sin, batch, seq)
            w_out = moba_w_out[j]
        else:
            o = _mixer_dil(xt, g0, dil_w_in[j], cos, sin, batch, seq)
            w_out = dil_w_out[j]
        xt = _out_proj(o, w_out.astype(BF16), xt, norm_gains[i, 1], "out_proj_%d" % i)
        xt = _ffn(xt, norm_gains[i, 2], ffn_w_up[i].astype(BF16), ffn_conv_w[i], ffn_conv_b[i],
                  ffn_w_down[i].astype(BF16), norm_gains[i, 3], ple_norm_g[i],
                  ple_gate_w[i].astype(BF16), ple_proj_w[i].astype(BF16), p[i].reshape(t, PLE_DIM),
                  seq, "ffn_%d" % i)
    return xt.reshape(batch, seq, d)
```

```python
import functools
import math

import jax
import jax.numpy as jnp
from jax import lax
from jax.experimental import pallas as pl
from jax.experimental.pallas import tpu as pltpu

F32 = jnp.float32
BF16 = jnp.bfloat16

D_MODEL = 1024
DEPTH = 4
N_MIXERS = 4
HEAD_DIM = 64
ROPE_THETA = 10000.0
NORM_EPS = 1e-6
PLE_DIM = 256
NEG = -1e30

DIFF_HEADS = D_MODEL // (2 * HEAD_DIM)

NSA_HEADS = D_MODEL // HEAD_DIM
NSA_GROUPS = 4
NSA_REP = NSA_HEADS // NSA_GROUPS
NSA_CMP_LEN = 32
NSA_CMP_STRIDE = 16
NSA_CMP_HIDDEN = 128
NSA_SLC_BLOCK = 64
NSA_TOPN = 16
NSA_N_LOCAL = 2
NSA_WINDOW = 512
NSA_FORCE_BONUS = 1e4

MOBA_HEADS = D_MODEL // HEAD_DIM
MOBA_BLOCK = 256
MOBA_TOPK = 3

DIL_PATTERNS = ((128, 1), (512, 4), (2048, 16))
DIL_HEADS = 8

D_FF = 2816
CONV_WIDTH = 3

LANES = 128
MXU_COLS = 256
TQ = 256
LOG2E = 1.4426950408889634
Q_SCALE = (HEAD_DIM ** -0.5) * LOG2E
VMEM_LIMIT = 56 * 1024 * 1024


def _cparams(sem):
    return pltpu.CompilerParams(dimension_semantics=sem, vmem_limit_bytes=VMEM_LIMIT)


def _rms(x, g):
    return x * lax.rsqrt(jnp.mean(x * x, axis=-1, keepdims=True) + NORM_EPS) * g


def _dot(a, b, precision=None):
    return jnp.dot(a, b, preferred_element_type=F32, precision=precision)


def _dot_nt(a, b, precision=None):
    return lax.dot_general(a, b, (((1,), (1,)), ((), ())), preferred_element_type=F32,
                           precision=precision)


def _lane_iota(shape):
    return lax.broadcasted_iota(jnp.int32, shape, 1)


def _row_iota(shape):
    return lax.broadcasted_iota(jnp.int32, shape, 0)


def _rope_kernel(pos_ref, inv_ref, cos_ref, sin_ref):
    ang = pos_ref[...].astype(F32) * inv_ref[...]
    lane = _lane_iota(ang.shape)
    cos_ref[...] = jnp.cos(ang)
    sin_ref[...] = jnp.where((lane & 63) < 32, -jnp.sin(ang), jnp.sin(ang))


def _rope_tables(positions):
    t = positions.size
    inv = ROPE_THETA ** (-jnp.arange(0, HEAD_DIM, 2, dtype=F32) / HEAD_DIM)
    inv = jnp.tile(inv, 4)[None, :]
    tm = 2048
    return pl.pallas_call(
        _rope_kernel,
        grid=(t // tm,),
        in_specs=[pl.BlockSpec((tm, 1), lambda i: (i, 0)),
                  pl.BlockSpec((1, LANES), lambda i: (0, 0))],
        out_specs=[pl.BlockSpec((tm, LANES), lambda i: (i, 0))] * 2,
        out_shape=[jax.ShapeDtypeStruct((t, LANES), F32)] * 2,
        compiler_params=_cparams(("parallel",)),
        name="rope_tables",
    )(positions.reshape(t, 1), inv)


def _proj_kernel(x_ref, g_ref, w_ref, cos_ref, sin_ref, *o_refs, blocks):
    hn = _rms(x_ref[...], g_ref[...]).astype(BF16)
    cos = cos_ref[...]
    sin = sin_ref[...]
    first = (_lane_iota(cos.shape) & 63) < 32
    n_chunks = len(blocks) // 2
    for c in range(n_chunks):
        acc = _dot(hn, w_ref[:, c * MXU_COLS:(c + 1) * MXU_COLS])
        for s in range(2):
            out_idx, out_blk, rope, scale = blocks[2 * c + s]
            if out_idx is None:
                continue
            a = acc[:, s * LANES:(s + 1) * LANES]
            if rope:
                rot = jnp.where(first, pltpu.roll(a, 96, 1), pltpu.roll(a, 32, 1))
                a = a * cos + rot * sin
            if scale != 1.0:
                a = a * scale
            o_ref = o_refs[out_idx]
            o_ref[:, out_blk * LANES:(out_blk + 1) * LANES] = a.astype(o_ref.dtype)


def _proj(x, g, w, cos, sin, blocks, outs, name):
    t = x.shape[0]
    n = w.shape[1]
    assert n == len(blocks) * LANES and len(blocks) % 2 == 0
    tm = 512
    return pl.pallas_call(
        functools.partial(_proj_kernel, blocks=tuple(blocks)),
        grid=(t // tm,),
        in_specs=[pl.BlockSpec((tm, D_MODEL), lambda i: (i, 0)),
                  pl.BlockSpec((1, D_MODEL), lambda i: (0, 0)),
                  pl.BlockSpec((D_MODEL, n), lambda i: (0, 0)),
                  pl.BlockSpec((tm, LANES), lambda i: (i, 0)),
                  pl.BlockSpec((tm, LANES), lambda i: (i, 0))],
        out_specs=[pl.BlockSpec((tm, nb * LANES), lambda i: (i, 0)) for nb, _ in outs],
        out_shape=[jax.ShapeDtypeStruct((t, nb * LANES), dt) for nb, dt in outs],
        compiler_params=_cparams(("parallel",)),
        name=name,
    )(x, g.reshape(1, D_MODEL), w, cos, sin)


def _out_kernel(o_ref, w_ref, x_ref, g_ref, y_ref):
    y = _dot(o_ref[...], w_ref[...])
    y_ref[...] = x_ref[...] + _rms(y, g_ref[...])


def _out_proj(o, w, x, g, name):
    t, k = o.shape
    tm = 1024
    return pl.pallas_call(
        _out_kernel,
        grid=(t // tm,),
        in_specs=[pl.BlockSpec((tm, k), lambda i: (i, 0)),
                  pl.BlockSpec((k, D_MODEL), lambda i: (0, 0)),
                  pl.BlockSpec((tm, D_MODEL), lambda i: (i, 0)),
                  pl.BlockSpec((1, D_MODEL), lambda i: (0, 0))],
        out_specs=pl.BlockSpec((tm, D_MODEL), lambda i: (i, 0)),
        out_shape=jax.ShapeDtypeStruct((t, D_MODEL), F32),
        compiler_params=_cparams(("parallel",)),
        name=name,
    )(o, w, x, g.reshape(1, D_MODEL))


FF_CHUNK = 256
FF_TM = 512
FF_HALO = 16


def _gelu_tanh(x):
    k = math.sqrt(2.0 / math.pi)
    h = 0.5 * x
    return h + h * jnp.tanh(x * (k + (k * 0.044715) * (x * x)))


def _ffn_kernel(x_ref, xh_ref, g2_ref, wu_ref, cw_ref, cb_ref, wd_ref, g3_ref, pg_ref, pgw_ref, ppw_ref,
                p_ref, y_ref, hn_s, *, seq):
    i = pl.program_id(0)
    tm = x_ref.shape[0]
    n_c = D_FF // FF_CHUNK
    x = x_ref[...]
    hn_s[FF_HALO:, :] = _rms(x, g2_ref[...]).astype(BF16)
    halo = _rms(xh_ref[...], g2_ref[...])
    seq_start = (i * tm) % seq == 0
    hn_s[:FF_HALO, :] = jnp.where(seq_start, 0.0, halo).astype(BF16)

    def conv(u, col):
        cols = slice(col, col + FF_CHUNK)
        out = cb_ref[:, cols] + pltpu.roll(u, 2, 0)[FF_HALO:] * cw_ref[0:1, cols]
        out = out + pltpu.roll(u, 1, 0)[FF_HALO:] * cw_ref[1:2, cols]
        return out + u[FF_HALO:] * cw_ref[2:3, cols]

    def up(c, r):
        hn = hn_s[r * FF_ROWS:(r + 1) * FF_ROWS + FF_HALO, :]
        gcol, vcol = c * FF_CHUNK, D_FF + c * FF_CHUNK
        return (_dot(hn, wu_ref[:, gcol:gcol + FF_CHUNK]), _dot(hn, wu_ref[:, vcol:vcol + FF_CHUNK]))

    def down(c, act, acc):
        d = _dot(act, wd_ref[c * FF_CHUNK:(c + 1) * FF_CHUNK, :])
        return d if acc is None else acc + d

    units = [(c, r) for c in range(n_c) for r in range(tm // FF_ROWS)]
    accs = [None] * (tm // FF_ROWS)
    ups = {k: up(*units[k]) for k in range(FF_UP_LEAD)}
    pending = None
    for k, (c, r) in enumerate(units):
        ug, uv = ups.pop(k)
        if k + FF_UP_LEAD < len(units):
            ups[k + FF_UP_LEAD] = up(*units[k + FF_UP_LEAD])
        act = (_gelu_tanh(conv(ug, c * FF_CHUNK)) * conv(uv, D_FF + c * FF_CHUNK)).astype(BF16)
        if pending is not None:
            accs[pending[1]] = down(pending[0], pending[2], accs[pending[1]])
        pending = (c, r, act)
    accs[pending[1]] = down(pending[0], pending[2], accs[pending[1]])

    for r, acc in enumerate(accs):
        rows = slice(r * FF_ROWS, (r + 1) * FF_ROWS)
        x2 = x[rows] + _rms(acc, g3_ref[...])
        gate = jax.nn.sigmoid(_dot(_rms(x2, pg_ref[...]).astype(BF16), pgw_ref[...]))
        y_ref[rows, :] = x2 + gate * _dot(p_ref[rows, :].astype(BF16), ppw_ref[...])


FF_ROWS = 256
FF_UP_LEAD = 2


def _ffn(x, g2, w_up, conv_w, conv_b, w_down, g3, ple_g, ple_gate_w, ple_proj_w, p, seq, name):
    t = x.shape[0]
    tm = FF_TM
    assert seq % tm == 0 and tm % FF_ROWS == 0
    hb = tm // FF_HALO
    row = lambda v: v.reshape(1, -1)
    const = lambda i: (0, 0)
    resident = lambda shape: pl.BlockSpec(shape, const, pipeline_mode=pl.Buffered(1))
    return pl.pallas_call(
        functools.partial(_ffn_kernel, seq=seq),
        grid=(t // tm,),
        in_specs=[pl.BlockSpec((tm, D_MODEL), lambda i: (i, 0)),
                  pl.BlockSpec((FF_HALO, D_MODEL), lambda i: (jnp.maximum(i * hb - 1, 0), 0)),
                  pl.BlockSpec((1, D_MODEL), const),
                  resident((D_MODEL, 2 * D_FF)),
                  pl.BlockSpec((CONV_WIDTH, 2 * D_FF), const),
                  pl.BlockSpec((1, 2 * D_FF), const),
                  resident((D_FF, D_MODEL)),
                  pl.BlockSpec((1, D_MODEL), const),
                  pl.BlockSpec((1, D_MODEL), const),
                  resident((D_MODEL, D_MODEL)),
                  resident((PLE_DIM, D_MODEL)),
                  pl.BlockSpec((tm, PLE_DIM), lambda i: (i, 0))],
        out_specs=pl.BlockSpec((tm, D_MODEL), lambda i: (i, 0)),
        out_shape=jax.ShapeDtypeStruct((t, D_MODEL), F32),
        scratch_shapes=[pltpu.VMEM((tm + FF_HALO, D_MODEL), BF16)],
        compiler_params=_cparams(("parallel",)),
        name=name,
    )(x, x, row(g2), w_up, conv_w, row(conv_b), w_down, row(g3), row(ple_g), ple_gate_w, ple_proj_w, p)


def _lo_half(shape):
    return _lane_iota(shape) < HEAD_DIM


def _split_heads(q):
    lo = _lo_half(q.shape)
    qf = q.astype(F32)
    return jnp.where(lo, qf, 0.0).astype(q.dtype), jnp.where(lo, 0.0, qf).astype(q.dtype)


def _tile(ref, j):
    return ref[j * TQ:(j + 1) * TQ, :]


def _scores(q_op, k_ref, first, masks, step=1):
    s_list = []
    for t, mask in enumerate(masks):
        s = _dot_nt(_tile(k_ref, first + t * step), q_op)
        s_list.append(s if mask is None else jnp.where(mask, s, NEG))
    return s_list


def _attend(s_list, values_t):
    mt = None
    for s in s_list:
        mt = s if mt is None else jnp.maximum(mt, s)
    m = jnp.max(mt, axis=0, keepdims=True)
    acc = None
    for s, v_t in zip(s_list, values_t):
        d = _dot(v_t, jnp.exp2(s - m).astype(BF16))
        acc = d if acc is None else acc + d
    return acc


def _value_t(v_tile, keep=None):
    v_t = v_tile.astype(F32).T
    return (v_t if keep is None else jnp.where(keep, v_t, 1.0)).astype(BF16)


def _store_pair_values(v_ref, vt_s, j, *idx):
    lo_rows = _row_iota((LANES, TQ)) < HEAD_DIM
    v_t = _tile(v_ref, j).astype(F32).T
    cols = slice(j * TQ, (j + 1) * TQ)
    vt_s[idx + (0, slice(None), cols)] = jnp.where(lo_rows, v_t, 1.0).astype(BF16)
    vt_s[idx + (1, slice(None), cols)] = jnp.where(lo_rows, 1.0, v_t).astype(BF16)


def _pair_output(acc0, acc1):
    out0 = acc0[:HEAD_DIM] / acc0[HEAD_DIM:HEAD_DIM + 1]
    out1 = acc1[HEAD_DIM:] / acc1[0:1]
    return jnp.concatenate([out0, out1], axis=0).T


ONES_ROWS = 16


def _tile_t(vt_s, j, *idx):
    return vt_s[idx + (slice(None), slice(j * TQ, (j + 1) * TQ))]


def _run_pipelined(stages, lead=None):
    n = len(stages)
    if lead is None:
        for prep_fn, _, _ in stages:
            if prep_fn is not None:
                prep_fn()
        lead = 0
    states = {}
    for step in range(n + lead + 1):
        if lead and step < n and stages[step][0] is not None:
            stages[step][0]()
        if lead <= step < n + lead:
            states[step - lead] = stages[step - lead][1]()
        if step >= lead + 1:
            stages[step - lead - 1][2](states.pop(step - lead - 1))


PREP_LEAD = 5


def _causal_masks():
    diff = _lane_iota((TQ, TQ)) - _row_iota((TQ, TQ))
    return diff, diff >= 0


def _rank_rows(score, n_valid):
    ridx = _row_iota(score.shape)
    cnt = jnp.zeros(score.shape, F32)
    for j in range(n_valid):
        other = score[j:j + 1, :]
        cnt = cnt + jnp.where(other > score, 1.0,
                              jnp.where(other == score, jnp.where(ridx > j, 1.0, 0.0), 0.0))
    return cnt


def _bias_lanes(bias_t, first_lane):
    r, tq = bias_t.shape
    parts = []
    if first_lane:
        parts.append(jnp.zeros((first_lane, tq), F32))
    parts.append(bias_t)
    if LANES - first_lane - r:
        parts.append(jnp.zeros((LANES - first_lane - r, tq), F32))
    return jnp.concatenate(parts, axis=0).T


def _diff_kernel(q_ref, k_ref, v_ref, lam_ref, sg_ref, o_ref, vt_s, *, lambda_init):
    nq = q_ref.shape[0] // TQ
    _, causal = _causal_masks()
    lamv = lam_ref[...]
    lam = (jnp.exp(jnp.sum(lamv[0:1] * lamv[1:2], axis=1, keepdims=True))
           - jnp.exp(jnp.sum(lamv[2:3] * lamv[3:4], axis=1, keepdims=True)) + lambda_init)
    sg = sg_ref[...]

    def prep(i):
        vt_s[:LANES, i * TQ:(i + 1) * TQ] = _value_t(_tile(v_ref, i))
        vt_s[LANES:, i * TQ:(i + 1) * TQ] = jnp.ones((ONES_ROWS, TQ), BF16)

    def score(i):
        q0, q1 = _split_heads(_tile(q_ref, i))
        masks = [None] * i + [causal]
        return _scores(q0, k_ref, 0, masks), _scores(q1, k_ref, 0, masks)

    def finish(i, state):
        v_tiles = [_tile_t(vt_s, j) for j in range(i + 1)]
        a0 = _attend(state[0], v_tiles)
        a1 = _attend(state[1], v_tiles)
        o_t = a0[:LANES] / a0[LANES:LANES + 1] - lam * (a1[:LANES] / a1[LANES:LANES + 1])
        o_ref[i * TQ:(i + 1) * TQ, :] = (_rms(o_t.T, sg) * (1.0 - lambda_init)).astype(o_ref.dtype)

    _run_pipelined([(functools.partial(prep, i), functools.partial(score, i), functools.partial(finish, i))
                    for i in range(nq)], PREP_LEAD)


def _diff_attention(qkv, lam_rows, subln_g, batch, seq, lambda_init):
    t = batch * seq
    h = DIFF_HEADS
    return pl.pallas_call(
        functools.partial(_diff_kernel, lambda_init=lambda_init),
        grid=(batch, h),
        in_specs=[pl.BlockSpec((seq, LANES), lambda b, hh: (b, hh)),
                  pl.BlockSpec((seq, LANES), lambda b, hh: (b, h + hh)),
                  pl.BlockSpec((seq, LANES), lambda b, hh: (b, 2 * h + hh)),
                  pl.BlockSpec((8, LANES), lambda b, hh: (0, 0)),
                  pl.BlockSpec((1, LANES), lambda b, hh: (0, 0))],
        out_specs=pl.BlockSpec((seq, LANES), lambda b, hh: (b, hh)),
        out_shape=jax.ShapeDtypeStruct((t, h * LANES), BF16),
        scratch_shapes=[pltpu.VMEM((LANES + ONES_ROWS, seq), BF16)],
        compiler_params=_cparams(("parallel", "parallel")),
        name="diff_attention",
    )(qkv, qkv, qkv, lam_rows, subln_g.reshape(1, LANES))


def _moba_kernel(q_ref, k_ref, v_ref, o_ref, ka_s, vt_s):
    seq = k_ref.shape[0]
    nb = seq // MOBA_BLOCK
    lane = _lane_iota((TQ, LANES))
    lo = lane < HEAD_DIM
    _, causal = _causal_masks()
    km = jnp.concatenate([jnp.mean(_tile(k_ref, j).astype(F32), axis=0, keepdims=True) for j in range(nb)],
                         axis=0)
    q_aug = {}
    outs = {}

    def prep(i, a):
        if a == 0:
            _store_pair_values(v_ref, vt_s, i)
            kf = _tile(k_ref, i).astype(F32)
            ka_s[0, i * TQ:(i + 1) * TQ, :] = jnp.where(lo, kf, jnp.where(lane == HEAD_DIM + i, 1.0, 0.0)
                                                        ).astype(BF16)
            ka_s[1, i * TQ:(i + 1) * TQ, :] = jnp.where(lo, jnp.where(lane == i, 1.0, 0.0), kf).astype(BF16)
        head = _split_heads(_tile(q_ref, i))[a]
        q_aug[i, a] = head
        if i > 0:
            gate_t = _dot_nt(km, head.astype(F32), precision=lax.Precision.HIGHEST)
            rank = _rank_rows(gate_t, i)
            drop = jnp.logical_and(_row_iota(gate_t.shape) < i, rank >= MOBA_TOPK)
            bias = _bias_lanes(jnp.where(drop, NEG, 0.0), HEAD_DIM * (1 - a))
            q_aug[i, a] = (head.astype(F32) + bias).astype(BF16)

    def score(i, a):
        return _scores(q_aug[i, a], ka_s.at[a], 0, [None] * i + [causal])

    def finish(i, a, s_list):
        outs[i, a] = _attend(s_list, [_tile_t(vt_s, j, a) for j in range(i + 1)])
        if a == 1:
            o_ref[i * TQ:(i + 1) * TQ, :] = _pair_output(outs[i, 0], outs[i, 1]).astype(o_ref.dtype)

    _run_pipelined([(functools.partial(prep, i, a), functools.partial(score, i, a),
                     functools.partial(finish, i, a)) for i in range(nb) for a in range(2)], PREP_LEAD)


def _moba_attention(qkv, batch, seq):
    assert MOBA_BLOCK == TQ and seq % MOBA_BLOCK == 0 and seq // MOBA_BLOCK <= HEAD_DIM
    t = batch * seq
    hp = MOBA_HEADS // 2
    return pl.pallas_call(
        _moba_kernel,
        grid=(batch, hp),
        in_specs=[pl.BlockSpec((seq, LANES), lambda b, hh: (b, hh)),
                  pl.BlockSpec((seq, LANES), lambda b, hh: (b, hp + hh)),
                  pl.BlockSpec((seq, LANES), lambda b, hh: (b, 2 * hp + hh))],
        out_specs=pl.BlockSpec((seq, LANES), lambda b, hh: (b, hh)),
        out_shape=jax.ShapeDtypeStruct((t, hp * LANES), BF16),
        scratch_shapes=[pltpu.VMEM((2, seq, LANES), BF16), pltpu.VMEM((2, LANES, seq), BF16)],
        compiler_params=_cparams(("parallel", "parallel")),
        name="moba_attention",
    )(qkv, qkv, qkv)


def _dil_kernel(q0_ref, q1_ref, q2_ref, k0_ref, k1_ref, k2_ref, v0_ref, v1_ref, v2_ref, o_ref, vt_s):
    seq = k0_ref.shape[0]
    nq = seq // TQ
    (w0, _), (w1, d1), (w2, d2) = DIL_PATTERNS
    assert w0 < TQ and w1 == 2 * TQ and w2 >= seq
    diff, causal = _causal_masks()
    on1 = (diff & (d1 - 1)) == 0
    on2 = (diff & (d2 - 1)) == 0
    m0_diag = jnp.logical_and(causal, diff <= w0)
    m0_prev = diff + TQ <= w0
    m1_diag = jnp.logical_and(causal, on1)
    m1_prev2 = jnp.logical_and(on1, diff <= 0)
    m2_diag = jnp.logical_and(causal, on2)
    lo = _lo_half((TQ, LANES))
    k_refs = (k0_ref, k1_ref, k2_ref)
    q_refs = (q0_ref, q1_ref, q2_ref)
    outs = {}

    def runs(i):
        g0 = ([m0_prev] if i >= 1 else []) + [m0_diag]
        g1 = ([m1_prev2] if i >= 2 else []) + ([on1] if i >= 1 else []) + [m1_diag]
        g2 = [on2] * i + [m2_diag]
        return [(g, i + 1 - len(masks), masks) for g, masks in enumerate((g0, g1, g2))]

    def score(i, a):
        s_list = []
        for g, first, masks in runs(i):
            s_list += _scores(_split_heads(_tile(q_refs[g], i))[a], k_refs[g], first, masks)
        return s_list

    def finish(i, a, s_list):
        v_tiles = [_tile_t(vt_s, first + t, g, a) for g, first, masks in runs(i) for t in range(len(masks))]
        outs[i, a] = _attend(s_list, v_tiles)
        if a == 1:
            o_ref[i * TQ:(i + 1) * TQ, :] = _pair_output(outs[i, 0], outs[i, 1]).astype(o_ref.dtype)

    def prep(i, a):
        if a == 0:
            for g, v_ref in enumerate((v0_ref, v1_ref, v2_ref)):
                _store_pair_values(v_ref, vt_s, i, g)

    _run_pipelined([(functools.partial(prep, i, a), functools.partial(score, i, a),
                     functools.partial(finish, i, a)) for i in reversed(range(nq)) for a in range(2)])


def _dil_attention(qkv, batch, seq):
    t = batch * seq
    hp = DIL_HEADS // 2
    per_g = 3 * hp

    def spec(g, which):
        return pl.BlockSpec((seq, LANES), lambda b, hh: (b, g * per_g + which * hp + hh))

    return pl.pallas_call(
        _dil_kernel,
        grid=(batch, hp),
        in_specs=[spec(0, 0), spec(1, 0), spec(2, 0), spec(0, 1), spec(1, 1), spec(2, 1),
                  spec(0, 2), spec(1, 2), spec(2, 2)],
        out_specs=pl.BlockSpec((seq, LANES), lambda b, hh: (b, hh)),
        out_shape=jax.ShapeDtypeStruct((t, hp * LANES), BF16),
        scratch_shapes=[pltpu.VMEM((len(DIL_PATTERNS), 2, LANES, seq), BF16)],
        compiler_params=_cparams(("parallel", "parallel")),
        name="dilated_attention",
    )(*([qkv] * 9))


N_CMP_PAD = 128


def _cmp_kernel(zk_ref, zv_ref, w1k_ref, w1v_ref, pk_ref, pv_ref, w1fk_ref, w1fv_ref, w2k_ref, w2v_ref,
                ko_ref, vo_ref):
    def one(z_ref, w1_ref, pos_ref, w1f_ref, w2_ref, o_ref):
        z = z_ref[...]
        a = _dot(z, w1_ref[0])
        b = _dot(z, w1_ref[1])
        b_next = jnp.concatenate([b[1:], jnp.zeros_like(b[:1])], axis=0)
        const = _dot(pos_ref[...], w1f_ref[...])[0:1]
        hid = a + b_next + jnp.concatenate([const] * NSA_GROUPS, axis=1)
        o_ref[...] = _dot(jax.nn.gelu(hid, approximate=True).astype(BF16), w2_ref[...]).astype(o_ref.dtype)

    one(zk_ref, w1k_ref, pk_ref, w1fk_ref, w2k_ref, ko_ref)
    one(zv_ref, w1v_ref, pv_ref, w1fv_ref, w2v_ref, vo_ref)


def _nsa_compress(kc, vc, ck_pos, ck_w1, ck_w2, cv_pos, cv_w1, cv_w2, batch, seq):
    g, dk, f = NSA_GROUPS, HEAD_DIM, NSA_CMP_HIDDEN
    half = NSA_CMP_LEN // 2
    assert NSA_CMP_STRIDE == half and seq // half == N_CMP_PAD
    zk = kc.reshape(batch * N_CMP_PAD, half * g * dk)
    zv = vc.reshape(batch * N_CMP_PAD, half * g * dk)

    def expand_w1(w1):
        w = w1.reshape(2, half, dk, f)
        eye = jnp.eye(g, dtype=w1.dtype)
        big = jnp.einsum('acdf,gh->acgdhf', w, eye)
        return big.reshape(2, half * g * dk, g * f).astype(BF16)

    def expand_w2(w2):
        dup = jnp.concatenate([w2, w2], axis=1)
        eye = jnp.eye(g, dtype=w2.dtype)
        return jnp.einsum('fd,gh->gfhd', dup, eye).reshape(g * f, g * LANES).astype(BF16)

    def pos_rows(pos):
        flat = pos.reshape(1, NSA_CMP_LEN * dk)
        return jnp.concatenate([flat, jnp.zeros((7, NSA_CMP_LEN * dk), pos.dtype)], axis=0).astype(BF16)

    kd = half * g * dk
    const2 = lambda b: (0, 0)
    const3 = lambda b: (0, 0, 0)
    return pl.pallas_call(
        _cmp_kernel,
        grid=(batch,),
        in_specs=[pl.BlockSpec((N_CMP_PAD, kd), lambda b: (b, 0)),
                  pl.BlockSpec((N_CMP_PAD, kd), lambda b: (b, 0)),
                  pl.BlockSpec((2, kd, g * f), const3),
                  pl.BlockSpec((2, kd, g * f), const3),
                  pl.BlockSpec((8, NSA_CMP_LEN * dk), const2),
                  pl.BlockSpec((8, NSA_CMP_LEN * dk), const2),
                  pl.BlockSpec((NSA_CMP_LEN * dk, f), const2),
                  pl.BlockSpec((NSA_CMP_LEN * dk, f), const2),
                  pl.BlockSpec((g * f, g * LANES), const2),
                  pl.BlockSpec((g * f, g * LANES), const2)],
        out_specs=[pl.BlockSpec((N_CMP_PAD, g * LANES), lambda b: (b, 0))] * 2,
        out_shape=[jax.ShapeDtypeStruct((batch * N_CMP_PAD, g * LANES), BF16)] * 2,
        compiler_params=_cparams(("parallel",)),
        name="nsa_compress",
    )(zk, zv, expand_w1(ck_w1), expand_w1(cv_w1), pos_rows(ck_pos), pos_rows(cv_pos),
      ck_w1.reshape(NSA_CMP_LEN * dk, f).astype(BF16), cv_w1.reshape(NSA_CMP_LEN * dk, f).astype(BF16),
      expand_w2(ck_w2), expand_w2(cv_w2))


SLC_SHIFT = NSA_SLC_BLOCK.bit_length() - 1


def _nsa_heads(q_pair):
    lo = _lo_half(q_pair.shape)
    qf = q_pair.astype(F32)
    return jnp.where(lo, qf, 0.0), jnp.where(lo, pltpu.roll(qf, HEAD_DIM, 1), 0.0)


def _lane_column(x, c):
    return jnp.sum(jnp.where(_lane_iota(x.shape) == c, x, 0.0), axis=1, keepdims=True)


def _nsa_select_kernel(q_ref, kc_ref, vc_ref, gate_ref, ov_ref, oc_ref, bias_ref):
    seq = q_ref.shape[0]
    nq = seq // TQ
    n_slc = seq // NSA_SLC_BLOCK
    assert 1 << SLC_SHIFT == NSA_SLC_BLOCK and HEAD_DIM + n_slc <= LANES
    n_idx = _row_iota((N_CMP_PAD, TQ))
    t_idx = _lane_iota((N_CMP_PAD, TQ))
    kc = kc_ref[...]
    vc_t = _value_t(vc_ref[...])
    jb = _row_iota((n_slc, TQ))
    cmask, scores = [], []
    for i in range(nq):
        cmask.append(jnp.logical_and(n_idx * NSA_CMP_STRIDE + (NSA_CMP_LEN - 1) <= i * TQ + t_idx,
                                     n_idx < N_CMP_PAD - 1))
        heads_bf = []
        for pr in range(NSA_REP // 2):
            heads_bf.extend(h.astype(BF16) for h in
                            _nsa_heads(q_ref[i * TQ:(i + 1) * TQ, pr * LANES:(pr + 1) * LANES]))
        scores.append([jnp.where(cmask[i], _dot_nt(kc, heads_bf[r]), NEG) for r in range(NSA_REP)])
    p_sums, o_cmps = [], []
    for i in range(nq):
        p_sum = jnp.zeros((N_CMP_PAD, TQ), F32)
        o_cmp = []
        for s in scores[i]:
            e = jnp.where(cmask[i], jnp.exp2(s - jnp.max(s, axis=0, keepdims=True)), 0.0)
            p = e / jnp.maximum(jnp.sum(e, axis=0, keepdims=True), 1e-30)
            p_sum = p_sum + p
            o_cmp.append(_dot(vc_t, p.astype(BF16))[:HEAD_DIM])
        p_sums.append(p_sum)
        o_cmps.append(o_cmp)
    for i in range(nq):
        o_cmp = o_cmps[i]
        imp_t = _dot(ov_ref[...], p_sums[i], precision=lax.Precision.HIGHEST)
        back = ((i * TQ + _lane_iota((n_slc, TQ))) >> SLC_SHIFT) - jb
        ok = back >= 0
        forced = jnp.logical_or(jb == 0, jnp.logical_and(ok, back < NSA_N_LOCAL))
        score = jnp.where(ok, imp_t + jnp.where(forced, NSA_FORCE_BONUS, 0.0), NEG)
        rank = _rank_rows(score, n_slc)
        keep = jnp.logical_and(ok, rank < min(NSA_TOPN, n_slc))
        bias_ref[i * TQ:(i + 1) * TQ, :] = _bias_lanes(jnp.where(keep, 0.0, NEG), HEAD_DIM).astype(bias_ref.dtype)

        gates_t = jax.nn.sigmoid(gate_ref[i * TQ:(i + 1) * TQ, :]).T
        for pr in range(NSA_REP // 2):
            pair = jnp.concatenate([o_cmp[r] * gates_t[3 * r:3 * r + 1] for r in (2 * pr, 2 * pr + 1)], axis=0)
            oc_ref[i * TQ:(i + 1) * TQ, pr * LANES:(pr + 1) * LANES] = pair.T.astype(oc_ref.dtype)


def _nsa_attn_kernel(q_ref, ks_ref, vs_ref, kw_ref, vw_ref, bias_ref, gate_ref, oc_ref, o_ref, ksa_s, vt_s):
    seq = ks_ref.shape[0]
    nq = seq // TQ
    lo_rows = _row_iota((LANES, TQ)) < HEAD_DIM
    per_tile = TQ // NSA_SLC_BLOCK
    assert NSA_WINDOW == 2 * TQ
    pr = pl.program_id(2)
    lane = _lane_iota((TQ, LANES))
    row = _row_iota((TQ, LANES))
    lo = lane < HEAD_DIM
    diff, causal = _causal_masks()
    outs = {}

    def prep(i, h):
        if h == 0:
            for b, v_ref in enumerate((vs_ref, vw_ref)):
                vt_s[b, :, i * TQ:(i + 1) * TQ] = _value_t(_tile(v_ref, i), lo_rows)
            ind = jnp.where(lane - HEAD_DIM == i * per_tile + (row >> SLC_SHIFT), 1.0, 0.0)
            ksa_s[i * TQ:(i + 1) * TQ, :] = jnp.where(lo, _tile(ks_ref, i).astype(F32), ind).astype(BF16)

    def win_masks(i):
        return [causal] + ([None] if i >= 1 else []) + ([diff < 0] if i >= 2 else [])

    def score(i, h):
        head = _nsa_heads(_tile(q_ref, i))[h]
        q_aug = (head + _tile(bias_ref, i).astype(F32)).astype(BF16)
        return (_scores(q_aug, ksa_s, 0, [None] * i + [causal]),
                _scores(head.astype(BF16), kw_ref, i, win_masks(i), step=-1))

    def finish(i, h, state):
        n_win = len(win_masks(i))
        a_slc = _attend(state[0], [_tile_t(vt_s, j, 0) for j in range(i + 1)])
        a_win = _attend(state[1], [_tile_t(vt_s, i - t, 1) for t in range(n_win)])
        gates_t = jax.nn.sigmoid(_tile(gate_ref, i)).T
        c0 = 3 * (2 * pr + h)

        def gate(c):
            return jnp.sum(jnp.where(_row_iota(gates_t.shape) == c, gates_t, 0.0), axis=0, keepdims=True)

        outs[i, h] = (a_slc[:HEAD_DIM] * (gate(c0 + 1) / a_slc[HEAD_DIM:HEAD_DIM + 1])
                      + a_win[:HEAD_DIM] * (gate(c0 + 2) / a_win[HEAD_DIM:HEAD_DIM + 1]))
        if h == 1:
            pair = jnp.concatenate([outs[i, 0], outs[i, 1]], axis=0).T
            o_ref[i * TQ:(i + 1) * TQ, :] = (_tile(oc_ref, i).astype(F32) + pair).astype(o_ref.dtype)

    _run_pipelined([(functools.partial(prep, i, h), functools.partial(score, i, h),
                     functools.partial(finish, i, h)) for i in range(nq) for h in range(2)], PREP_LEAD)


def _nsa_overlap_t(seq):
    n_slc = seq // NSA_SLC_BLOCK
    n = jnp.arange(N_CMP_PAD)
    first = n * NSA_CMP_STRIDE
    last = first + NSA_CMP_LEN - 1
    s_start = jnp.arange(n_slc) * NSA_SLC_BLOCK
    ov = (first[None, :] <= (s_start + NSA_SLC_BLOCK - 1)[:, None]) & (last[None, :] >= s_start[:, None])
    ov = ov & (n[None, :] < (seq - NSA_CMP_LEN) // NSA_CMP_STRIDE + 1)
    return ov.astype(F32)


def _nsa_attention(q, ks, vs, kw, vw, kcmp, vcmp, gates, batch, seq):
    t = batch * seq
    g = NSA_GROUPS
    qb = NSA_REP // 2
    n_slc = seq // NSA_SLC_BLOCK
    grp = lambda: pl.BlockSpec((seq, LANES), lambda b, gg: (b, gg))
    cm = lambda: pl.BlockSpec((N_CMP_PAD, LANES), lambda b, gg: (b, gg))
    o_cmp, bias = pl.pallas_call(
        _nsa_select_kernel,
        grid=(batch, g),
        in_specs=[pl.BlockSpec((seq, qb * LANES), lambda b, gg: (b, gg)), cm(), cm(), grp(),
                  pl.BlockSpec((n_slc, N_CMP_PAD), lambda b, gg: (0, 0))],
        out_specs=[pl.BlockSpec((seq, qb * LANES), lambda b, gg: (b, gg)), grp()],
        out_shape=[jax.ShapeDtypeStruct((t, NSA_HEADS * HEAD_DIM), BF16),
                   jax.ShapeDtypeStruct((t, g * LANES), BF16)],
        compiler_params=_cparams(("parallel", "parallel")),
        name="nsa_select",
    )(q, kcmp, vcmp, gates, _nsa_overlap_t(seq))
    kv = lambda: pl.BlockSpec((seq, LANES), lambda b, gg, pr: (b, gg))
    pair = lambda: pl.BlockSpec((seq, LANES), lambda b, gg, pr: (b, gg * qb + pr))
    return pl.pallas_call(
        _nsa_attn_kernel,
        grid=(batch, g, qb),
        in_specs=[pair(), kv(), kv(), kv(), kv(), kv(), kv(), pair()],
        out_specs=pair(),
        out_shape=jax.ShapeDtypeStruct((t, NSA_HEADS * HEAD_DIM), BF16),
        scratch_shapes=[pltpu.VMEM((seq, LANES), BF16), pltpu.VMEM((2, LANES, seq), BF16)],
        compiler_params=_cparams(("parallel", "parallel", "arbitrary")),
        name="nsa_attention",
    )(q, ks, vs, kw, vw, bias, gates, o_cmp)


def _plain_blocks(n_total, n_rope, n_q):
    return [(0, b, b < n_rope, Q_SCALE if b < n_q else 1.0) for b in range(n_total)]


def _mixer_diff(x, g, w_in, lam_q1, lam_k1, lam_q2, lam_k2, subln_g, cos, sin, batch, seq, lambda_init):
    h = DIFF_HEADS
    blocks = _plain_blocks(3 * h, 2 * h, h)
    (qkv,) = _proj(x, g, w_in.astype(BF16), cos, sin, blocks, [(3 * h, BF16)], "diff_in_proj")
    pad = lambda v: jnp.pad(v.astype(F32), (0, LANES - HEAD_DIM))
    lam_rows = jnp.stack([pad(lam_q1), pad(lam_k1), pad(lam_q2), pad(lam_k2)]
                         + [jnp.zeros((LANES,), F32)] * 4)
    return _diff_attention(qkv, lam_rows, subln_g, batch, seq, lambda_init)


def _mixer_moba(x, g, w_in, cos, sin, batch, seq):
    hp = MOBA_HEADS // 2
    blocks = _plain_blocks(3 * hp, 2 * hp, hp)
    (qkv,) = _proj(x, g, w_in.astype(BF16), cos, sin, blocks, [(3 * hp, BF16)], "moba_in_proj")
    return _moba_attention(qkv, batch, seq)


def _mixer_dil(x, g, w_in, cos, sin, batch, seq):
    hp = DIL_HEADS // 2
    blocks = []
    for gi in range(len(DIL_PATTERNS)):
        for which in range(3):
            for b in range(hp):
                blocks.append((0, len(blocks), which < 2, Q_SCALE if which == 0 else 1.0))
    (qkv,) = _proj(x, g, w_in.astype(BF16), cos, sin, blocks, [(len(blocks), BF16)], "dil_in_proj")
    return _dil_attention(qkv, batch, seq)


def _mixer_nsa(x, g, w_in, ck_pos, ck_w1, ck_w2, cv_pos, cv_w1, cv_w2, cos, sin, batch, seq):
    hq, gr, dk = NSA_HEADS, NSA_GROUPS, HEAD_DIM
    kvw = gr * dk
    o_q = 0
    o_kc, o_vc, o_ks, o_vs, o_kw, o_vw, o_g = (hq * dk + n * kvw for n in range(7))
    n_gate = 3 * hq

    def dup(off):
        cols = w_in[:, off:off + kvw].reshape(D_MODEL, gr, dk)
        return jnp.concatenate([cols, cols], axis=2).reshape(D_MODEL, gr * LANES)

    per_group = n_gate // gr
    gate_cols = jnp.pad(w_in[:, o_g:o_g + n_gate].reshape(D_MODEL, gr, per_group),
                        ((0, 0), (0, 0), (0, LANES - per_group))).reshape(D_MODEL, gr * LANES)
    w = jnp.concatenate([w_in[:, o_q:o_q + hq * dk], dup(o_ks), dup(o_kw), w_in[:, o_kc:o_kc + kvw],
                         dup(o_vs), dup(o_vw), w_in[:, o_vc:o_vc + kvw], gate_cols], axis=1).astype(BF16)
    nqb = hq * dk // LANES
    blocks = [(0, b, True, Q_SCALE) for b in range(nqb)]
    blocks += [(1, b, True, 1.0) for b in range(gr)]
    blocks += [(2, b, True, 1.0) for b in range(gr)]
    blocks += [(3, b, True, 1.0) for b in range(kvw // LANES)]
    blocks += [(4, b, False, 1.0) for b in range(gr)]
    blocks += [(5, b, False, 1.0) for b in range(gr)]
    blocks += [(6, b, False, 1.0) for b in range(kvw // LANES)]
    blocks += [(7, b, False, 1.0) for b in range(gr)]
    outs = [(nqb, BF16), (gr, BF16), (gr, BF16), (kvw // LANES, BF16), (gr, BF16), (gr, BF16),
            (kvw // LANES, BF16), (gr, F32)]
    q, ks, kw, kc, vs, vw, vc, gates = _proj(x, g, w, cos, sin, blocks, outs, "nsa_in_proj")
    kcmp, vcmp = _nsa_compress(kc, vc, ck_pos, ck_w1, ck_w2, cv_pos, cv_w1, cv_w2, batch, seq)
    return _nsa_attention(q, ks, vs, kw, vw, kcmp, vcmp, gates, batch, seq)


def kernel(x, p, positions, norm_gains, ple_norm_g, ple_gate_w, ple_proj_w, ffn_w_up, ffn_conv_w, ffn_conv_b, ffn_w_down, diff_w_in, diff_w_out, diff_lam_q1, diff_lam_k1, diff_lam_q2, diff_lam_k2, diff_subln_g, nsa_w_in, nsa_w_out, nsa_ck_pos, nsa_ck_w1, nsa_ck_w2, nsa_cv_pos, nsa_cv_w1, nsa_cv_w2, moba_w_in, moba_w_out, dil_w_in, dil_w_out):
    batch, seq, d = x.shape
    t = batch * seq
    cos, sin = _rope_tables(positions)
    xt = x.reshape(t, d)
    (ffn_w_up, ffn_w_down, ple_gate_w, ple_proj_w, diff_w_in, diff_w_out, nsa_w_in, nsa_w_out, moba_w_in,
     moba_w_out, dil_w_in, dil_w_out) = (w.astype(BF16) for w in (
         ffn_w_up, ffn_w_down, ple_gate_w, ple_proj_w, diff_w_in, diff_w_out, nsa_w_in, nsa_w_out, moba_w_in,
         moba_w_out, dil_w_in, dil_w_out))
    for i in range(DEPTH):
        kind, j = i % N_MIXERS, i // N_MIXERS
        g0 = norm_gains[i, 0]
        if kind == 0:
            lambda_init = 0.8 - 0.6 * math.exp(-0.3 * i)
            o = _mixer_diff(xt, g0, diff_w_in[j], diff_lam_q1[j], diff_lam_k1[j], diff_lam_q2[j],
                            diff_lam_k2[j], diff_subln_g[j], cos, sin, batch, seq, lambda_init)
            w_out = diff_w_out[j]
        elif kind == 1:
            o = _mixer_nsa(xt, g0, nsa_w_in[j], nsa_ck_pos[j], nsa_ck_w1[j], nsa_ck_w2[j],
                           nsa_cv_pos[j], nsa_cv_w1[j], nsa_cv_w2[j], cos, sin, batch, seq)
            w_out = nsa_w_out[j]
        elif kind == 2:
            o = _mixer_moba(xt, g0, moba_w_in[j], cos, sin, batch, seq)
            w_out = moba_w_out[j]
        else:
            o = _mixer_dil(xt, g0, dil_w_in[j], cos, sin, batch, seq)
            w_out = dil_w_out[j]
        xt = _out_proj(o, w_out.astype(BF16), xt, norm_gains[i, 1], "out_proj_%d" % i)
        xt = _ffn(xt, norm_gains[i, 2], ffn_w_up[i].astype(BF16), ffn_conv_w[i], ffn_conv_b[i],
                  ffn_w_down[i].astype(BF16), norm_gains[i, 3], ple_norm_g[i],
                  ple_gate_w[i].astype(BF16), ple_proj_w[i].astype(BF16), p[i].reshape(t, PLE_DIM),
                  seq, "ffn_%d" % i)
    return xt.reshape(batch, seq, d)
```

```python
import functools
import math

import jax
import jax.numpy as jnp
from jax import lax
from jax.experimental import pallas as pl
from jax.experimental.pallas import tpu as pltpu

F32 = jnp.float32
BF16 = jnp.bfloat16

D_MODEL = 1024
DEPTH = 4
N_MIXERS = 4
HEAD_DIM = 64
ROPE_THETA = 10000.0
NORM_EPS = 1e-6
PLE_DIM = 256
NEG = -1e30

DIFF_HEADS = D_MODEL // (2 * HEAD_DIM)

NSA_HEADS = D_MODEL // HEAD_DIM
NSA_GROUPS = 4
NSA_REP = NSA_HEADS // NSA_GROUPS
NSA_CMP_LEN = 32
NSA_CMP_STRIDE = 16
NSA_CMP_HIDDEN = 128
NSA_SLC_BLOCK = 64
NSA_TOPN = 16
NSA_N_LOCAL = 2
NSA_WINDOW = 512
NSA_FORCE_BONUS = 1e4

MOBA_HEADS = D_MODEL // HEAD_DIM
MOBA_BLOCK = 256
MOBA_TOPK = 3

DIL_PATTERNS = ((128, 1), (512, 4), (2048, 16))
DIL_HEADS = 8

D_FF = 2816
CONV_WIDTH = 3

LANES = 128
MXU_COLS = 256
TQ = 256
LOG2E = 1.4426950408889634
Q_SCALE = (HEAD_DIM ** -0.5) * LOG2E
VMEM_LIMIT = 56 * 1024 * 1024


def _cparams(sem):
    return pltpu.CompilerParams(dimension_semantics=sem, vmem_limit_bytes=VMEM_LIMIT)


def _rms(x, g):
    return x * lax.rsqrt(jnp.mean(x * x, axis=-1, keepdims=True) + NORM_EPS) * g


def _dot(a, b, precision=None):
    return jnp.dot(a, b, preferred_element_type=F32, precision=precision)


def _dot_nt(a, b, precision=None):
    return lax.dot_general(a, b, (((1,), (1,)), ((), ())), preferred_element_type=F32,
                           precision=precision)


def _lane_iota(shape):
    return lax.broadcasted_iota(jnp.int32, shape, 1)


def _row_iota(shape):
    return lax.broadcasted_iota(jnp.int32, shape, 0)


def _rope_kernel(pos_ref, inv_ref, cos_ref, sin_ref):
    ang = pos_ref[...].astype(F32) * inv_ref[...]
    lane = _lane_iota(ang.shape)
    cos_ref[...] = jnp.cos(ang)
    sin_ref[...] = jnp.where((lane & 63) < 32, -jnp.sin(ang), jnp.sin(ang))


def _rope_tables(positions):
    t = positions.size
    inv = ROPE_THETA ** (-jnp.arange(0, HEAD_DIM, 2, dtype=F32) / HEAD_DIM)
    inv = jnp.tile(inv, 4)[None, :]
    tm = 2048
    return pl.pallas_call(
        _rope_kernel,
        grid=(t // tm,),
        in_specs=[pl.BlockSpec((tm, 1), lambda i: (i, 0)),
                  pl.BlockSpec((1, LANES), lambda i: (0, 0))],
        out_specs=[pl.BlockSpec((tm, LANES), lambda i: (i, 0))] * 2,
        out_shape=[jax.ShapeDtypeStruct((t, LANES), F32)] * 2,
        compiler_params=_cparams(("parallel",)),
        name="rope_tables",
    )(positions.reshape(t, 1), inv)


def _proj_kernel(x_ref, g_ref, w_ref, cos_ref, sin_ref, *o_refs, blocks):
    hn = _rms(x_ref[...], g_ref[...]).astype(BF16)
    cos = cos_ref[...]
    sin = sin_ref[...]
    first = (_lane_iota(cos.shape) & 63) < 32
    n_chunks = len(blocks) // 2
    for c in range(n_chunks):
        acc = _dot(hn, w_ref[:, c * MXU_COLS:(c + 1) * MXU_COLS])
        for s in range(2):
            out_idx, out_blk, rope, scale = blocks[2 * c + s]
            if out_idx is None:
                continue
            a = acc[:, s * LANES:(s + 1) * LANES]
            if rope:
                rot = jnp.where(first, pltpu.roll(a, 96, 1), pltpu.roll(a, 32, 1))
                a = a * cos + rot * sin
            if scale != 1.0:
                a = a * scale
            o_ref = o_refs[out_idx]
            o_ref[:, out_blk * LANES:(out_blk + 1) * LANES] = a.astype(o_ref.dtype)


def _proj(x, g, w, cos, sin, blocks, outs, name):
    t = x.shape[0]
    n = w.shape[1]
    assert n == len(blocks) * LANES and len(blocks) % 2 == 0
    tm = 512
    return pl.pallas_call(
        functools.partial(_proj_kernel, blocks=tuple(blocks)),
        grid=(t // tm,),
        in_specs=[pl.BlockSpec((tm, D_MODEL), lambda i: (i, 0)),
                  pl.BlockSpec((1, D_MODEL), lambda i: (0, 0)),
                  pl.BlockSpec((D_MODEL, n), lambda i: (0, 0)),
                  pl.BlockSpec((tm, LANES), lambda i: (i, 0)),
                  pl.BlockSpec((tm, LANES), lambda i: (i, 0))],
        out_specs=[pl.BlockSpec((tm, nb * LANES), lambda i: (i, 0)) for nb, _ in outs],
        out_shape=[jax.ShapeDtypeStruct((t, nb * LANES), dt) for nb, dt in outs],
        compiler_params=_cparams(("parallel",)),
        name=name,
    )(x, g.reshape(1, D_MODEL), w, cos, sin)


def _out_kernel(o_ref, w_ref, x_ref, g_ref, y_ref):
    y = _dot(o_ref[...], w_ref[...])
    y_ref[...] = x_ref[...] + _rms(y, g_ref[...])


def _out_proj(o, w, x, g, name):
    t, k = o.shape
    tm = 1024
    return pl.pallas_call(
        _out_kernel,
        grid=(t // tm,),
        in_specs=[pl.BlockSpec((tm, k), lambda i: (i, 0)),
                  pl.BlockSpec((k, D_MODEL), lambda i: (0, 0)),
                  pl.BlockSpec((tm, D_MODEL), lambda i: (i, 0)),
                  pl.BlockSpec((1, D_MODEL), lambda i: (0, 0))],
        out_specs=pl.BlockSpec((tm, D_MODEL), lambda i: (i, 0)),
        out_shape=jax.ShapeDtypeStruct((t, D_MODEL), F32),
        compiler_params=_cparams(("parallel",)),
        name=name,
    )(o, w, x, g.reshape(1, D_MODEL))


FF_CHUNK = 256
FF_TM = 512
FF_HALO = 16


def _gelu_tanh(x):
    k = math.sqrt(2.0 / math.pi)
    h = 0.5 * x
    return h + h * jnp.tanh(x * (k + (k * 0.044715) * (x * x)))


def _ffn_kernel(x_ref, xh_ref, g2_ref, wu_ref, cw_ref, cb_ref, wd_ref, g3_ref, pg_ref, pgw_ref, ppw_ref,
                p_ref, y_ref, hn_s, *, seq):
    i = pl.program_id(0)
    tm = x_ref.shape[0]
    n_c = D_FF // FF_CHUNK
    x = x_ref[...]
    hn_s[FF_HALO:, :] = _rms(x, g2_ref[...]).astype(BF16)
    halo = _rms(xh_ref[...], g2_ref[...])
    seq_start = (i * tm) % seq == 0
    hn_s[:FF_HALO, :] = jnp.where(seq_start, 0.0, halo).astype(BF16)

    def conv(u, col):
        cols = slice(col, col + FF_CHUNK)
        out = cb_ref[:, cols] + pltpu.roll(u, 2, 0)[FF_HALO:] * cw_ref[0:1, cols]
        out = out + pltpu.roll(u, 1, 0)[FF_HALO:] * cw_ref[1:2, cols]
        return out + u[FF_HALO:] * cw_ref[2:3, cols]

    def up(c, r):
        hn = hn_s[r * FF_ROWS:(r + 1) * FF_ROWS + FF_HALO, :]
        gcol, vcol = c * FF_CHUNK, D_FF + c * FF_CHUNK
        return (_dot(hn, wu_ref[:, gcol:gcol + FF_CHUNK]), _dot(hn, wu_ref[:, vcol:vcol + FF_CHUNK]))

    def down(c, act, acc):
        d = _dot(act, wd_ref[c * FF_CHUNK:(c + 1) * FF_CHUNK, :])
        return d if acc is None else acc + d

    units = [(c, r) for c in range(n_c) for r in range(tm // FF_ROWS)]
    accs = [None] * (tm // FF_ROWS)
    ups = {k: up(*units[k]) for k in range(FF_UP_LEAD)}
    pending = None
    for k, (c, r) in enumerate(units):
        ug, uv = ups.pop(k)
        if k + FF_UP_LEAD < len(units):
            ups[k + FF_UP_LEAD] = up(*units[k + FF_UP_LEAD])
        act = (_gelu_tanh(conv(ug, c * FF_CHUNK)) * conv(uv, D_FF + c * FF_CHUNK)).astype(BF16)
        if pending is not None:
            accs[pending[1]] = down(pending[0], pending[2], accs[pending[1]])
        pending = (c, r, act)
    accs[pending[1]] = down(pending[0], pending[2], accs[pending[1]])

    for r, acc in enumerate(accs):
        rows = slice(r * FF_ROWS, (r + 1) * FF_ROWS)
        x2 = x[rows] + _rms(acc, g3_ref[...])
        gate = jax.nn.sigmoid(_dot(_rms(x2, pg_ref[...]).astype(BF16), pgw_ref[...]))
        y_ref[rows, :] = x2 + gate * _dot(p_ref[rows, :].astype(BF16), ppw_ref[...])


FF_ROWS = 256
FF_UP_LEAD = 2


def _ffn(x, g2, w_up, conv_w, conv_b, w_down, g3, ple_g, ple_gate_w, ple_proj_w, p, seq, name):
    t = x.shape[0]
    tm = FF_TM
    assert seq % tm == 0 and tm % FF_ROWS == 0
    hb = tm // FF_HALO
    row = lambda v: v.reshape(1, -1)
    const = lambda i: (0, 0)
    resident = lambda shape: pl.BlockSpec(shape, const, pipeline_mode=pl.Buffered(1))
    return pl.pallas_call(
        functools.partial(_ffn_kernel, seq=seq),
        grid=(t // tm,),
        in_specs=[pl.BlockSpec((tm, D_MODEL), lambda i: (i, 0)),
                  pl.BlockSpec((FF_HALO, D_MODEL), lambda i: (jnp.maximum(i * hb - 1, 0), 0)),
                  pl.BlockSpec((1, D_MODEL), const),
                  resident((D_MODEL, 2 * D_FF)),
                  pl.BlockSpec((CONV_WIDTH, 2 * D_FF), const),
                  pl.BlockSpec((1, 2 * D_FF), const),
                  resident((D_FF, D_MODEL)),
                  pl.BlockSpec((1, D_MODEL), const),
                  pl.BlockSpec((1, D_MODEL), const),
                  resident((D_MODEL, D_MODEL)),
                  resident((PLE_DIM, D_MODEL)),
                  pl.BlockSpec((tm, PLE_DIM), lambda i: (i, 0))],
        out_specs=pl.BlockSpec((tm, D_MODEL), lambda i: (i, 0)),
        out_shape=jax.ShapeDtypeStruct((t, D_MODEL), F32),
        scratch_shapes=[pltpu.VMEM((tm + FF_HALO, D_MODEL), BF16)],
        compiler_params=_cparams(("parallel",)),
        name=name,
    )(x, x, row(g2), w_up, conv_w, row(conv_b), w_down, row(g3), row(ple_g), ple_gate_w, ple_proj_w, p)


def _lo_half(shape):
    return _lane_iota(shape) < HEAD_DIM


def _split_heads(q):
    lo = _lo_half(q.shape)
    qf = q.astype(F32)
    return jnp.where(lo, qf, 0.0).astype(q.dtype), jnp.where(lo, 0.0, qf).astype(q.dtype)


def _tile(ref, j):
    return ref[j * TQ:(j + 1) * TQ, :]


def _scores(q_op, k_ref, first, masks, step=1):
    s_list = []
    for t, mask in enumerate(masks):
        s = _dot_nt(_tile(k_ref, first + t * step), q_op)
        s_list.append(s if mask is None else jnp.where(mask, s, NEG))
    return s_list


def _attend(s_list, values_t):
    by_shape = {}
    for s in s_list:
        by_shape[s.shape] = s if s.shape not in by_shape else jnp.maximum(by_shape[s.shape], s)
    m = None
    for mt in by_shape.values():
        m_part = jnp.max(mt, axis=0, keepdims=True)
        m = m_part if m is None else jnp.maximum(m, m_part)
    acc = None
    for s, v_t in zip(s_list, values_t):
        d = _dot(v_t, jnp.exp2(s - m).astype(BF16))
        acc = d if acc is None else acc + d
    return acc


def _value_t(v_tile, keep=None):
    v_t = v_tile.astype(F32).T
    return (v_t if keep is None else jnp.where(keep, v_t, 1.0)).astype(BF16)


def _store_pair_values(v_ref, vt_s, j, *idx):
    lo_rows = _row_iota((LANES, TQ)) < HEAD_DIM
    v_t = _tile(v_ref, j).astype(F32).T
    cols = slice(j * TQ, (j + 1) * TQ)
    vt_s[idx + (0, slice(None), cols)] = jnp.where(lo_rows, v_t, 1.0).astype(BF16)
    vt_s[idx + (1, slice(None), cols)] = jnp.where(lo_rows, 1.0, v_t).astype(BF16)


def _pair_output(acc0, acc1):
    out0 = acc0[:HEAD_DIM] / acc0[HEAD_DIM:HEAD_DIM + 1]
    out1 = acc1[HEAD_DIM:] / acc1[0:1]
    return jnp.concatenate([out0, out1], axis=0).T


ONES_ROWS = 16


def _tile_t(vt_s, j, *idx):
    return vt_s[idx + (slice(None), slice(j * TQ, (j + 1) * TQ))]


def _run_pipelined(stages, lead=None):
    n = len(stages)
    if lead is None:
        for prep_fn, _, _ in stages:
            if prep_fn is not None:
                prep_fn()
        lead = 0
    states = {}
    for step in range(n + lead + 1):
        if lead and step < n and stages[step][0] is not None:
            stages[step][0]()
        if lead <= step < n + lead:
            states[step - lead] = stages[step - lead][1]()
        if step >= lead + 1:
            stages[step - lead - 1][2](states.pop(step - lead - 1))


PREP_LEAD = 5


def _causal_masks():
    diff = _lane_iota((TQ, TQ)) - _row_iota((TQ, TQ))
    return diff, diff >= 0


def _rank_rows(score, n_valid):
    ridx = _row_iota(score.shape)
    cnt = jnp.zeros(score.shape, F32)
    for j in range(n_valid):
        other = score[j:j + 1, :]
        cnt = cnt + jnp.where(other > score, 1.0,
                              jnp.where(other == score, jnp.where(ridx > j, 1.0, 0.0), 0.0))
    return cnt


def _bias_lanes(bias_t, first_lane):
    r, tq = bias_t.shape
    parts = []
    if first_lane:
        parts.append(jnp.zeros((first_lane, tq), F32))
    parts.append(bias_t)
    if LANES - first_lane - r:
        parts.append(jnp.zeros((LANES - first_lane - r, tq), F32))
    return jnp.concatenate(parts, axis=0).T


def _diff_kernel(q_ref, k_ref, v_ref, lam_ref, sg_ref, o_ref, vt_s, *, lambda_init):
    nq = q_ref.shape[0] // TQ
    _, causal = _causal_masks()
    lamv = lam_ref[...]
    lam = (jnp.exp(jnp.sum(lamv[0:1] * lamv[1:2], axis=1, keepdims=True))
           - jnp.exp(jnp.sum(lamv[2:3] * lamv[3:4], axis=1, keepdims=True)) + lambda_init)
    sg = sg_ref[...]

    def prep(i):
        vt_s[:LANES, i * TQ:(i + 1) * TQ] = _value_t(_tile(v_ref, i))
        vt_s[LANES:, i * TQ:(i + 1) * TQ] = jnp.ones((ONES_ROWS, TQ), BF16)

    def score(i):
        q0, q1 = _split_heads(_tile(q_ref, i))
        masks = [None] * i + [causal]
        return _scores(q0, k_ref, 0, masks), _scores(q1, k_ref, 0, masks)

    def finish(i, state):
        v_tiles = [_tile_t(vt_s, j) for j in range(i + 1)]
        a0 = _attend(state[0], v_tiles)
        a1 = _attend(state[1], v_tiles)
        o_t = a0[:LANES] / a0[LANES:LANES + 1] - lam * (a1[:LANES] / a1[LANES:LANES + 1])
        o_ref[i * TQ:(i + 1) * TQ, :] = (_rms(o_t.T, sg) * (1.0 - lambda_init)).astype(o_ref.dtype)

    _run_pipelined([(functools.partial(prep, i), functools.partial(score, i), functools.partial(finish, i))
                    for i in range(nq)], PREP_LEAD)


def _diff_attention(qkv, lam_rows, subln_g, batch, seq, lambda_init):
    t = batch * seq
    h = DIFF_HEADS
    return pl.pallas_call(
        functools.partial(_diff_kernel, lambda_init=lambda_init),
        grid=(batch, h),
        in_specs=[pl.BlockSpec((seq, LANES), lambda b, hh: (b, hh)),
                  pl.BlockSpec((seq, LANES), lambda b, hh: (b, h + hh)),
                  pl.BlockSpec((seq, LANES), lambda b, hh: (b, 2 * h + hh)),
                  pl.BlockSpec((8, LANES), lambda b, hh: (0, 0)),
                  pl.BlockSpec((1, LANES), lambda b, hh: (0, 0))],
        out_specs=pl.BlockSpec((seq, LANES), lambda b, hh: (b, hh)),
        out_shape=jax.ShapeDtypeStruct((t, h * LANES), BF16),
        scratch_shapes=[pltpu.VMEM((LANES + ONES_ROWS, seq), BF16)],
        compiler_params=_cparams(("parallel", "parallel")),
        name="diff_attention",
    )(qkv, qkv, qkv, lam_rows, subln_g.reshape(1, LANES))


def _moba_kernel(q_ref, k_ref, v_ref, o_ref, ka_s, vt_s):
    seq = k_ref.shape[0]
    nb = seq // MOBA_BLOCK
    lane = _lane_iota((TQ, LANES))
    lo = lane < HEAD_DIM
    _, causal = _causal_masks()
    km = jnp.concatenate([jnp.mean(_tile(k_ref, j).astype(F32), axis=0, keepdims=True) for j in range(nb)],
                         axis=0)
    q_aug = {}
    outs = {}

    def prep(i, a):
        if a == 0:
            _store_pair_values(v_ref, vt_s, i)
            kf = _tile(k_ref, i).astype(F32)
            ka_s[0, i * TQ:(i + 1) * TQ, :] = jnp.where(lo, kf, jnp.where(lane == HEAD_DIM + i, 1.0, 0.0)
                                                        ).astype(BF16)
            ka_s[1, i * TQ:(i + 1) * TQ, :] = jnp.where(lo, jnp.where(lane == i, 1.0, 0.0), kf).astype(BF16)
        head = _split_heads(_tile(q_ref, i))[a]
        q_aug[i, a] = head
        if i > 0:
            gate_t = _dot_nt(km, head.astype(F32), precision=lax.Precision.HIGHEST)
            rank = _rank_rows(gate_t, i)
            drop = jnp.logical_and(_row_iota(gate_t.shape) < i, rank >= MOBA_TOPK)
            bias = _bias_lanes(jnp.where(drop, NEG, 0.0), HEAD_DIM * (1 - a))
            q_aug[i, a] = (head.astype(F32) + bias).astype(BF16)

    def score(i, a):
        return _scores(q_aug[i, a], ka_s.at[a], 0, [None] * i + [causal])

    def finish(i, a, s_list):
        outs[i, a] = _attend(s_list, [_tile_t(vt_s, j, a) for j in range(i + 1)])
        if a == 1:
            o_ref[i * TQ:(i + 1) * TQ, :] = _pair_output(outs[i, 0], outs[i, 1]).astype(o_ref.dtype)

    _run_pipelined([(functools.partial(prep, i, a), functools.partial(score, i, a),
                     functools.partial(finish, i, a)) for i in range(nb) for a in range(2)], PREP_LEAD)


def _moba_attention(qkv, batch, seq):
    assert MOBA_BLOCK == TQ and seq % MOBA_BLOCK == 0 and seq // MOBA_BLOCK <= HEAD_DIM
    t = batch * seq
    hp = MOBA_HEADS // 2
    return pl.pallas_call(
        _moba_kernel,
        grid=(batch, hp),
        in_specs=[pl.BlockSpec((seq, LANES), lambda b, hh: (b, hh)),
                  pl.BlockSpec((seq, LANES), lambda b, hh: (b, hp + hh)),
                  pl.BlockSpec((seq, LANES), lambda b, hh: (b, 2 * hp + hh))],
        out_specs=pl.BlockSpec((seq, LANES), lambda b, hh: (b, hh)),
        out_shape=jax.ShapeDtypeStruct((t, hp * LANES), BF16),
        scratch_shapes=[pltpu.VMEM((2, seq, LANES), BF16), pltpu.VMEM((2, LANES, seq), BF16)],
        compiler_params=_cparams(("parallel", "parallel")),
        name="moba_attention",
    )(qkv, qkv, qkv)


def _dil_kernel(q0_ref, q1_ref, q2_ref, k0_ref, k1_ref, k2_ref, v0_ref, v1_ref, v2_ref, o_ref, vt_s):
    seq = k0_ref.shape[0]
    nq = seq // TQ
    (w0, _), (w1, d1), (w2, d2) = DIL_PATTERNS
    assert w0 < TQ and w1 == 2 * TQ and w2 >= seq
    diff, causal = _causal_masks()
    on1 = (diff & (d1 - 1)) == 0
    on2 = (diff & (d2 - 1)) == 0
    m0_diag = jnp.logical_and(causal, diff <= w0)
    m0_prev = diff + TQ <= w0
    m1_diag = jnp.logical_and(causal, on1)
    m1_prev2 = jnp.logical_and(on1, diff <= 0)
    m2_diag = jnp.logical_and(causal, on2)
    lo = _lo_half((TQ, LANES))
    k_refs = (k0_ref, k1_ref, k2_ref)
    q_refs = (q0_ref, q1_ref, q2_ref)
    outs = {}

    def runs(i):
        g0 = [m0_diag]
        g1 = ([m1_prev2] if i >= 2 else []) + ([on1] if i >= 1 else []) + [m1_diag]
        g2 = [on2] * i + [m2_diag]
        return [(g, i + 1 - len(masks), masks) for g, masks in enumerate((g0, g1, g2))]

    tail = slice(TQ - w0, TQ)

    def score(i, a):
        s_list = []
        for g, first, masks in runs(i):
            s_list += _scores(_split_heads(_tile(q_refs[g], i))[a], k_refs[g], first, masks)
        if i >= 1:
            s = _dot_nt(_tile(k0_ref, i - 1)[tail], _split_heads(_tile(q0_ref, i))[a])
            s_list.append(jnp.where(m0_prev[tail], s, NEG))
        return s_list

    def finish(i, a, s_list):
        v_tiles = [_tile_t(vt_s, first + t, g, a) for g, first, masks in runs(i) for t in range(len(masks))]
        if i >= 1:
            v_tiles.append(_tile_t(vt_s, i - 1, 0, a)[:, tail])
        outs[i, a] = _attend(s_list, v_tiles)
        if a == 1:
            o_ref[i * TQ:(i + 1) * TQ, :] = _pair_output(outs[i, 0], outs[i, 1]).astype(o_ref.dtype)

    def prep(i, a):
        if a == 0:
            for g, v_ref in enumerate((v0_ref, v1_ref, v2_ref)):
                _store_pair_values(v_ref, vt_s, i, g)

    _run_pipelined([(functools.partial(prep, i, a), functools.partial(score, i, a),
                     functools.partial(finish, i, a)) for i in reversed(range(nq)) for a in range(2)])


def _dil_attention(qkv, batch, seq):
    t = batch * seq
    hp = DIL_HEADS // 2
    per_g = 3 * hp

    def spec(g, which):
        return pl.BlockSpec((seq, LANES), lambda b, hh: (b, g * per_g + which * hp + hh))

    return pl.pallas_call(
        _dil_kernel,
        grid=(batch, hp),
        in_specs=[spec(0, 0), spec(1, 0), spec(2, 0), spec(0, 1), spec(1, 1), spec(2, 1),
                  spec(0, 2), spec(1, 2), spec(2, 2)],
        out_specs=pl.BlockSpec((seq, LANES), lambda b, hh: (b, hh)),
        out_shape=jax.ShapeDtypeStruct((t, hp * LANES), BF16),
        scratch_shapes=[pltpu.VMEM((len(DIL_PATTERNS), 2, LANES, seq), BF16)],
        compiler_params=_cparams(("parallel", "parallel")),
        name="dilated_attention",
    )(*([qkv] * 9))


N_CMP_PAD = 128


def _cmp_kernel(zk_ref, zv_ref, w1k_ref, w1v_ref, pk_ref, pv_ref, w1fk_ref, w1fv_ref, w2k_ref, w2v_ref,
                ko_ref, vo_ref):
    def one(z_ref, w1_ref, pos_ref, w1f_ref, w2_ref, o_ref):
        z = z_ref[...]
        a = _dot(z, w1_ref[0])
        b = _dot(z, w1_ref[1])
        b_next = jnp.concatenate([b[1:], jnp.zeros_like(b[:1])], axis=0)
        const = _dot(pos_ref[...], w1f_ref[...])[0:1]
        hid = a + b_next + jnp.concatenate([const] * NSA_GROUPS, axis=1)
        o_ref[...] = _dot(jax.nn.gelu(hid, approximate=True).astype(BF16), w2_ref[...]).astype(o_ref.dtype)

    one(zk_ref, w1k_ref, pk_ref, w1fk_ref, w2k_ref, ko_ref)
    one(zv_ref, w1v_ref, pv_ref, w1fv_ref, w2v_ref, vo_ref)


def _nsa_compress(kc, vc, ck_pos, ck_w1, ck_w2, cv_pos, cv_w1, cv_w2, batch, seq):
    g, dk, f = NSA_GROUPS, HEAD_DIM, NSA_CMP_HIDDEN
    half = NSA_CMP_LEN // 2
    assert NSA_CMP_STRIDE == half and seq // half == N_CMP_PAD
    zk = kc.reshape(batch * N_CMP_PAD, half * g * dk)
    zv = vc.reshape(batch * N_CMP_PAD, half * g * dk)

    def expand_w1(w1):
        w = w1.reshape(2, half, dk, f)
        eye = jnp.eye(g, dtype=w1.dtype)
        big = jnp.einsum('acdf,gh->acgdhf', w, eye)
        return big.reshape(2, half * g * dk, g * f).astype(BF16)

    def expand_w2(w2):
        dup = jnp.concatenate([w2, w2], axis=1)
        eye = jnp.eye(g, dtype=w2.dtype)
        return jnp.einsum('fd,gh->gfhd', dup, eye).reshape(g * f, g * LANES).astype(BF16)

    def pos_rows(pos):
        flat = pos.reshape(1, NSA_CMP_LEN * dk)
        return jnp.concatenate([flat, jnp.zeros((7, NSA_CMP_LEN * dk), pos.dtype)], axis=0).astype(BF16)

    kd = half * g * dk
    const2 = lambda b: (0, 0)
    const3 = lambda b: (0, 0, 0)
    return pl.pallas_call(
        _cmp_kernel,
        grid=(batch,),
        in_specs=[pl.BlockSpec((N_CMP_PAD, kd), lambda b: (b, 0)),
                  pl.BlockSpec((N_CMP_PAD, kd), lambda b: (b, 0)),
                  pl.BlockSpec((2, kd, g * f), const3),
                  pl.BlockSpec((2, kd, g * f), const3),
                  pl.BlockSpec((8, NSA_CMP_LEN * dk), const2),
                  pl.BlockSpec((8, NSA_CMP_LEN * dk), const2),
                  pl.BlockSpec((NSA_CMP_LEN * dk, f), const2),
                  pl.BlockSpec((NSA_CMP_LEN * dk, f), const2),
                  pl.BlockSpec((g * f, g * LANES), const2),
                  pl.BlockSpec((g * f, g * LANES), const2)],
        out_specs=[pl.BlockSpec((N_CMP_PAD, g * LANES), lambda b: (b, 0))] * 2,
        out_shape=[jax.ShapeDtypeStruct((batch * N_CMP_PAD, g * LANES), BF16)] * 2,
        compiler_params=_cparams(("parallel",)),
        name="nsa_compress",
    )(zk, zv, expand_w1(ck_w1), expand_w1(cv_w1), pos_rows(ck_pos), pos_rows(cv_pos),
      ck_w1.reshape(NSA_CMP_LEN * dk, f).astype(BF16), cv_w1.reshape(NSA_CMP_LEN * dk, f).astype(BF16),
      expand_w2(ck_w2), expand_w2(cv_w2))


SLC_SHIFT = NSA_SLC_BLOCK.bit_length() - 1


def _nsa_heads(q_pair):
    lo = _lo_half(q_pair.shape)
    qf = q_pair.astype(F32)
    return jnp.where(lo, qf, 0.0), jnp.where(lo, pltpu.roll(qf, HEAD_DIM, 1), 0.0)


def _lane_column(x, c):
    return jnp.sum(jnp.where(_lane_iota(x.shape) == c, x, 0.0), axis=1, keepdims=True)


def _nsa_select_kernel(q_ref, kc_ref, vc_ref, gate_ref, ov_ref, oc_ref, bias_ref):
    seq = q_ref.shape[0]
    nq = seq // TQ
    n_slc = seq // NSA_SLC_BLOCK
    assert 1 << SLC_SHIFT == NSA_SLC_BLOCK and HEAD_DIM + n_slc <= LANES
    n_idx = _row_iota((N_CMP_PAD, TQ))
    t_idx = _lane_iota((N_CMP_PAD, TQ))
    kc = kc_ref[...]
    vc_t = _value_t(vc_ref[...])
    jb = _row_iota((n_slc, TQ))
    cmask, scores = [], []
    for i in range(nq):
        cmask.append(jnp.logical_and(n_idx * NSA_CMP_STRIDE + (NSA_CMP_LEN - 1) <= i * TQ + t_idx,
                                     n_idx < N_CMP_PAD - 1))
        heads_bf = []
        for pr in range(NSA_REP // 2):
            heads_bf.extend(h.astype(BF16) for h in
                            _nsa_heads(q_ref[i * TQ:(i + 1) * TQ, pr * LANES:(pr + 1) * LANES]))
        scores.append([jnp.where(cmask[i], _dot_nt(kc, heads_bf[r]), NEG) for r in range(NSA_REP)])
    p_sums, o_cmps = [], []
    for i in range(nq):
        p_sum = jnp.zeros((N_CMP_PAD, TQ), F32)
        o_cmp = []
        for s in scores[i]:
            e = jnp.where(cmask[i], jnp.exp2(s - jnp.max(s, axis=0, keepdims=True)), 0.0)
            p = e / jnp.maximum(jnp.sum(e, axis=0, keepdims=True), 1e-30)
            p_sum = p_sum + p
            o_cmp.append(_dot(vc_t, p.astype(BF16))[:HEAD_DIM])
        p_sums.append(p_sum)
        o_cmps.append(o_cmp)
    for i in range(nq):
        o_cmp = o_cmps[i]
        imp_t = _dot(ov_ref[...], p_sums[i], precision=lax.Precision.HIGHEST)
        back = ((i * TQ + _lane_iota((n_slc, TQ))) >> SLC_SHIFT) - jb
        ok = back >= 0
        forced = jnp.logical_or(jb == 0, jnp.logical_and(ok, back < NSA_N_LOCAL))
        score = jnp.where(ok, imp_t + jnp.where(forced, NSA_FORCE_BONUS, 0.0), NEG)
        rank = _rank_rows(score, n_slc)
        keep = jnp.logical_and(ok, rank < min(NSA_TOPN, n_slc))
        bias_ref[i * TQ:(i + 1) * TQ, :] = _bias_lanes(jnp.where(keep, 0.0, NEG), HEAD_DIM).astype(bias_ref.dtype)

        gates_t = jax.nn.sigmoid(gate_ref[i * TQ:(i + 1) * TQ, :]).T
        for pr in range(NSA_REP // 2):
            pair = jnp.concatenate([o_cmp[r] * gates_t[3 * r:3 * r + 1] for r in (2 * pr, 2 * pr + 1)], axis=0)
            oc_ref[i * TQ:(i + 1) * TQ, pr * LANES:(pr + 1) * LANES] = pair.T.astype(oc_ref.dtype)


def _nsa_attn_kernel(q_ref, ks_ref, vs_ref, kw_ref, vw_ref, bias_ref, gate_ref, oc_ref, o_ref, ksa_s, vt_s):
    seq = ks_ref.shape[0]
    nq = seq // TQ
    lo_rows = _row_iota((LANES, TQ)) < HEAD_DIM
    per_tile = TQ // NSA_SLC_BLOCK
    assert NSA_WINDOW == 2 * TQ
    pr = pl.program_id(2)
    lane = _lane_iota((TQ, LANES))
    row = _row_iota((TQ, LANES))
    lo = lane < HEAD_DIM
    diff, causal = _causal_masks()
    outs = {}

    def prep(i, h):
        if h == 0:
            for b, v_ref in enumerate((vs_ref, vw_ref)):
                vt_s[b, :, i * TQ:(i + 1) * TQ] = _value_t(_tile(v_ref, i), lo_rows)
            ind = jnp.where(lane - HEAD_DIM == i * per_tile + (row >> SLC_SHIFT), 1.0, 0.0)
            ksa_s[i * TQ:(i + 1) * TQ, :] = jnp.where(lo, _tile(ks_ref, i).astype(F32), ind).astype(BF16)

    def win_masks(i):
        return [causal] + ([None] if i >= 1 else []) + ([diff < 0] if i >= 2 else [])

    def score(i, h):
        head = _nsa_heads(_tile(q_ref, i))[h]
        q_aug = (head + _tile(bias_ref, i).astype(F32)).astype(BF16)
        return (_scores(q_aug, ksa_s, 0, [None] * i + [causal]),
                _scores(head.astype(BF16), kw_ref, i, win_masks(i), step=-1))

    def finish(i, h, state):
        n_win = len(win_masks(i))
        a_slc = _attend(state[0], [_tile_t(vt_s, j, 0) for j in range(i + 1)])
        a_win = _attend(state[1], [_tile_t(vt_s, i - t, 1) for t in range(n_win)])
        gates_t = jax.nn.sigmoid(_tile(gate_ref, i)).T
        c0 = 3 * (2 * pr + h)

        def gate(c):
            return jnp.sum(jnp.where(_row_iota(gates_t.shape) == c, gates_t, 0.0), axis=0, keepdims=True)

        outs[i, h] = (a_slc[:HEAD_DIM] * (gate(c0 + 1) / a_slc[HEAD_DIM:HEAD_DIM + 1])
                      + a_win[:HEAD_DIM] * (gate(c0 + 2) / a_win[HEAD_DIM:HEAD_DIM + 1]))
        if h == 1:
            pair = jnp.concatenate([outs[i, 0], outs[i, 1]], axis=0).T
            o_ref[i * TQ:(i + 1) * TQ, :] = (_tile(oc_ref, i).astype(F32) + pair).astype(o_ref.dtype)

    _run_pipelined([(functools.partial(prep, i, h), functools.partial(score, i, h),
                     functools.partial(finish, i, h)) for i in range(nq) for h in range(2)], PREP_LEAD)


def _nsa_overlap_t(seq):
    n_slc = seq // NSA_SLC_BLOCK
    n = jnp.arange(N_CMP_PAD)
    first = n * NSA_CMP_STRIDE
    last = first + NSA_CMP_LEN - 1
    s_start = jnp.arange(n_slc) * NSA_SLC_BLOCK
    ov = (first[None, :] <= (s_start + NSA_SLC_BLOCK - 1)[:, None]) & (last[None, :] >= s_start[:, None])
    ov = ov & (n[None, :] < (seq - NSA_CMP_LEN) // NSA_CMP_STRIDE + 1)
    return ov.astype(F32)


def _nsa_attention(q, ks, vs, kw, vw, kcmp, vcmp, gates, batch, seq):
    t = batch * seq
    g = NSA_GROUPS
    qb = NSA_REP // 2
    n_slc = seq // NSA_SLC_BLOCK
    grp = lambda: pl.BlockSpec((seq, LANES), lambda b, gg: (b, gg))
    cm = lambda: pl.BlockSpec((N_CMP_PAD, LANES), lambda b, gg: (b, gg))
    o_cmp, bias = pl.pallas_call(
        _nsa_select_kernel,
        grid=(batch, g),
        in_specs=[pl.BlockSpec((seq, qb * LANES), lambda b, gg: (b, gg)), cm(), cm(), grp(),
                  pl.BlockSpec((n_slc, N_CMP_PAD), lambda b, gg: (0, 0))],
        out_specs=[pl.BlockSpec((seq, qb * LANES), lambda b, gg: (b, gg)), grp()],
        out_shape=[jax.ShapeDtypeStruct((t, NSA_HEADS * HEAD_DIM), BF16),
                   jax.ShapeDtypeStruct((t, g * LANES), BF16)],
        compiler_params=_cparams(("parallel", "parallel")),
        name="nsa_select",
    )(q, kcmp, vcmp, gates, _nsa_overlap_t(seq))
    kv = lambda: pl.BlockSpec((seq, LANES), lambda b, gg, pr: (b, gg))
    pair = lambda: pl.BlockSpec((seq, LANES), lambda b, gg, pr: (b, gg * qb + pr))
    return pl.pallas_call(
        _nsa_attn_kernel,
        grid=(batch, g, qb),
        in_specs=[pair(), kv(), kv(), kv(), kv(), kv(), kv(), pair()],
        out_specs=pair(),
        out_shape=jax.ShapeDtypeStruct((t, NSA_HEADS * HEAD_DIM), BF16),
        scratch_shapes=[pltpu.VMEM((seq, LANES), BF16), pltpu.VMEM((2, LANES, seq), BF16)],
        compiler_params=_cparams(("parallel", "parallel", "arbitrary")),
        name="nsa_attention",
    )(q, ks, vs, kw, vw, bias, gates, o_cmp)


def _plain_blocks(n_total, n_rope, n_q):
    return [(0, b, b < n_rope, Q_SCALE if b < n_q else 1.0) for b in range(n_total)]


def _mixer_diff(x, g, w_in, lam_q1, lam_k1, lam_q2, lam_k2, subln_g, cos, sin, batch, seq, lambda_init):
    h = DIFF_HEADS
    blocks = _plain_blocks(3 * h, 2 * h, h)
    (qkv,) = _proj(x, g, w_in.astype(BF16), cos, sin, blocks, [(3 * h, BF16)], "diff_in_proj")
    pad = lambda v: jnp.pad(v.astype(F32), (0, LANES - HEAD_DIM))
    lam_rows = jnp.stack([pad(lam_q1), pad(lam_k1), pad(lam_q2), pad(lam_k2)]
                         + [jnp.zeros((LANES,), F32)] * 4)
    return _diff_attention(qkv, lam_rows, subln_g, batch, seq, lambda_init)


def _mixer_moba(x, g, w_in, cos, sin, batch, seq):
    hp = MOBA_HEADS // 2
    blocks = _plain_blocks(3 * hp, 2 * hp, hp)
    (qkv,) = _proj(x, g, w_in.astype(BF16), cos, sin, blocks, [(3 * hp, BF16)], "moba_in_proj")
    return _moba_attention(qkv, batch, seq)


def _mixer_dil(x, g, w_in, cos, sin, batch, seq):
    hp = DIL_HEADS // 2
    blocks = []
    for gi in range(len(DIL_PATTERNS)):
        for which in range(3):
            for b in range(hp):
                blocks.append((0, len(blocks), which < 2, Q_SCALE if which == 0 else 1.0))
    (qkv,) = _proj(x, g, w_in.astype(BF16), cos, sin, blocks, [(len(blocks), BF16)], "dil_in_proj")
    return _dil_attention(qkv, batch, seq)


def _mixer_nsa(x, g, w_in, ck_pos, ck_w1, ck_w2, cv_pos, cv_w1, cv_w2, cos, sin, batch, seq):
    hq, gr, dk = NSA_HEADS, NSA_GROUPS, HEAD_DIM
    kvw = gr * dk
    o_q = 0
    o_kc, o_vc, o_ks, o_vs, o_kw, o_vw, o_g = (hq * dk + n * kvw for n in range(7))
    n_gate = 3 * hq

    def dup(off):
        cols = w_in[:, off:off + kvw].reshape(D_MODEL, gr, dk)
        return jnp.concatenate([cols, cols], axis=2).reshape(D_MODEL, gr * LANES)

    per_group = n_gate // gr
    gate_cols = jnp.pad(w_in[:, o_g:o_g + n_gate].reshape(D_MODEL, gr, per_group),
                        ((0, 0), (0, 0), (0, LANES - per_group))).reshape(D_MODEL, gr * LANES)
    w = jnp.concatenate([w_in[:, o_q:o_q + hq * dk], dup(o_ks), dup(o_kw), w_in[:, o_kc:o_kc + kvw],
                         dup(o_vs), dup(o_vw), w_in[:, o_vc:o_vc + kvw], gate_cols], axis=1).astype(BF16)
    nqb = hq * dk // LANES
    blocks = [(0, b, True, Q_SCALE) for b in range(nqb)]
    blocks += [(1, b, True, 1.0) for b in range(gr)]
    blocks += [(2, b, True, 1.0) for b in range(gr)]
    blocks += [(3, b, True, 1.0) for b in range(kvw // LANES)]
    blocks += [(4, b, False, 1.0) for b in range(gr)]
    blocks += [(5, b, False, 1.0) for b in range(gr)]
    blocks += [(6, b, False, 1.0) for b in range(kvw // LANES)]
    blocks += [(7, b, False, 1.0) for b in range(gr)]
    outs = [(nqb, BF16), (gr, BF16), (gr, BF16), (kvw // LANES, BF16), (gr, BF16), (gr, BF16),
            (kvw // LANES, BF16), (gr, F32)]
    q, ks, kw, kc, vs, vw, vc, gates = _proj(x, g, w, cos, sin, blocks, outs, "nsa_in_proj")
    kcmp, vcmp = _nsa_compress(kc, vc, ck_pos, ck_w1, ck_w2, cv_pos, cv_w1, cv_w2, batch, seq)
    return _nsa_attention(q, ks, vs, kw, vw, kcmp, vcmp, gates, batch, seq)


def kernel(x, p, positions, norm_gains, ple_norm_g, ple_gate_w, ple_proj_w, ffn_w_up, ffn_conv_w, ffn_conv_b, ffn_w_down, diff_w_in, diff_w_out, diff_lam_q1, diff_lam_k1, diff_lam_q2, diff_lam_k2, diff_subln_g, nsa_w_in, nsa_w_out, nsa_ck_pos, nsa_ck_w1, nsa_ck_w2, nsa_cv_pos, nsa_cv_w1, nsa_cv_w2, moba_w_in, moba_w_out, dil_w_in, dil_w_out):
    batch, seq, d = x.shape
    t = batch * seq
    cos, sin = _rope_tables(positions)
    xt = x.reshape(t, d)
    (ffn_w_up, ffn_w_down, ple_gate_w, ple_proj_w, diff_w_in, diff_w_out, nsa_w_in, nsa_w_out, moba_w_in,
     moba_w_out, dil_w_in, dil_w_out) = (w.astype(BF16) for w in (
         ffn_w_up, ffn_w_down, ple_gate_w, ple_proj_w, diff_w_in, diff_w_out, nsa_w_in, nsa_w_out, moba_w_in,
         moba_w_out, dil_w_in, dil_w_out))
    for i in range(DEPTH):
        kind, j = i % N_MIXERS, i // N_MIXERS
        g0 = norm_gains[i, 0]
        if kind == 0:
            lambda_init = 0.8 - 0.6 * math.exp(-0.3 * i)
            o = _mixer_diff(xt, g0, diff_w_in[j], diff_lam_q1[j], diff_lam_k1[j], diff_lam_q2[j],
                            diff_lam_k2[j], diff_subln_g[j], cos, sin, batch, seq, lambda_init)
            w_out = diff_w_out[j]
        elif kind == 1:
            o = _mixer_nsa(xt, g0, nsa_w_in[j], nsa_ck_pos[j], nsa_ck_w1[j], nsa_ck_w2[j],
                           nsa_cv_pos[j], nsa_cv_w1[j], nsa_cv_w2[j], cos, sin, batch, seq)
            w_out = nsa_w_out[j]
        elif kind == 2:
            o = _mixer_moba(xt, g0, moba_w_in[j], cos, sin, batch, seq)
            w_out = moba_w_out[j]
        else:
            o = _mixer_dil(xt, g0, dil_w_in[j], cos, sin, batch, seq)
            w_out = dil_w_out[j]
        xt = _out_proj(o, w_out.astype(BF16), xt, norm_gains[i, 1], "out_proj_%d" % i)
        xt = _ffn(xt, norm_gains[i, 2], ffn_w_up[i].astype(BF16), ffn_conv_w[i], ffn_conv_b[i],
                  ffn_w_down[i].astype(BF16), norm_gains[i, 3], ple_norm_g[i],
                  ple_gate_w[i].astype(BF16), ple_proj_w[i].astype(BF16), p[i].reshape(t, PLE_DIM),
                  seq, "ffn_%d" % i)
    return xt.reshape(batch, seq, d)
```
